```python
import jax
import jax.numpy as jnp
from jax import lax
import numpy as np

D_MODEL = 2048
BATCH = 2
SEQ = 4096
DEPTH = 4
DEC_BATCH = 8
DEC_SEQ = 8
PAST_LEN = 16384
PAGE_SIZE = 128

D_MIX = D_MODEL
D_NSA = D_MIX // 2
D_RWKV = D_MIX - D_NSA
NSA_HD = 128
N_Q_HEADS = D_NSA // NSA_HD
N_KV = 2
QPG = N_Q_HEADS // N_KV
CMP_LEN = 32
CMP_STRIDE = 16
CMP_HID = 2 * NSA_HD
SEL_BLOCK = 64
SEL_TOPK = 16
N_LOCAL = 2
WINDOW = 512
Q_BLOCK = 64
N_KV_SLOTS = 4
RWKV_HD = 64
RWKV_HEADS = D_RWKV // RWKV_HD
DECAY_LORA = 64
A_LORA = 64
G_LORA = 128
D_FF = ((8 * D_MODEL // 3 + 255) // 256) * 256
NSA_COLS = D_NSA + 6 * N_KV * NSA_HD + 3 * N_Q_HEADS
RWKV_COLS = 3 * D_RWKV + DECAY_LORA + A_LORA + G_LORA
IN_COLS = NSA_COLS + RWKV_COLS
N_MOD = 9
NORM_EPS = 1e-6
GN_EPS = 64e-5
NEG = -1e30
FORCED = 1e6
INVALID = -1e6

kernel_name = 'hymba_nsa_rwkv7_macaron_adaln_decode_step'


def rms_norm(x, g):
    xf = x.astype(jnp.float32)
    y = xf * lax.rsqrt(jnp.mean(xf * xf, axis=-1, keepdims=True) + NORM_EPS)
    return (y * g.astype(jnp.float32)).astype(x.dtype)


def modulation(c, w_ada, b_ada):
    m = jax.nn.silu(c) @ w_ada + b_ada
    return m.reshape(c.shape[0], N_MOD, 1, D_MODEL)


def ada_norm(x, m, slot, g):
    return rms_norm(x, g) * (1 + m[:, 3 * slot + 1]) + m[:, 3 * slot]


def swiglu(h, wi, wo):
    gu = h @ wi
    return (jax.nn.silu(gu[..., :D_FF]) * gu[..., D_FF:]) @ wo


def compress(rows, pe, w1, b1, w2):
    b, L = rows.shape[:2]
    r = CMP_LEN // CMP_STRIDE
    n_sub = L // CMP_STRIDE
    n_cmp = n_sub - r + 1
    sub = rows[:, :n_sub * CMP_STRIDE].reshape(b, n_sub, CMP_STRIDE, N_KV, NSA_HD)
    acc = b1
    for j in range(r):
        seg = sub[:, j:j + n_cmp] + pe[j * CMP_STRIDE:(j + 1) * CMP_STRIDE, None, :]
        acc = acc + jnp.einsum('bnlgd,lde->bnge', seg, w1[j * CMP_STRIDE:(j + 1) * CMP_STRIDE])
    return jnp.einsum('bnge,ed->bngd', jax.nn.gelu(acc), w2)


def nsa_project(p_nsa, q_g, k_g):
    b, t = p_nsa.shape[:2]
    q = rms_norm(p_nsa[..., :D_NSA].reshape(b, t, N_Q_HEADS, NSA_HD), q_g)
    q = q.reshape(b, t, N_KV, QPG, NSA_HD)
    kv = p_nsa[..., D_NSA:D_NSA + 6 * N_KV * NSA_HD].reshape(b, t, 6, N_KV, NSA_HD)
    gates = jax.nn.sigmoid(p_nsa[..., D_NSA + 6 * N_KV * NSA_HD:].astype(jnp.float32))
    gates = gates.reshape(b, t, N_KV, QPG, 3)
    k_sel = rms_norm(kv[:, :, 2], k_g[1])
    k_win = rms_norm(kv[:, :, 4], k_g[2])
    rows = jnp.stack([kv[:, :, 0], kv[:, :, 1], k_sel, kv[:, :, 3]], axis=2)
    win = jnp.stack([k_win, kv[:, :, 5]], axis=2)
    return q, gates, rows, win


def nsa_context(rows, k_g, pe, w1, b1, w2):
    b, L = rows.shape[:2]
    kc = rms_norm(compress(rows[:, :, 0], pe[0], w1[0], b1[0], w2[0]), k_g[0])
    vc = compress(rows[:, :, 1], pe[1], w1[1], b1[1], w2[1])
    n_sel = -(-L // SEL_BLOCK)
    sel = jnp.pad(rows[:, :, 2:4], ((0, 0), (0, n_sel * SEL_BLOCK - L), (0, 0), (0, 0), (0, 0)))
    sel = sel.reshape(b, n_sel, SEL_BLOCK, 2, N_KV, NSA_HD).transpose(3, 0, 4, 1, 2, 5)
    return kc, vc, sel[0], sel[1]


def nsa_core(q, pos_q, gates, kc, vc, ks, vs, kw, vw, pos_w):
    f32 = jnp.float32
    b, tq = q.shape[:2]
    scale = NSA_HD ** -0.5
    n_cmp = kc.shape[1]
    cmp_end = jnp.arange(n_cmp) * CMP_STRIDE + (CMP_LEN - 1)
    m_c = (cmp_end[None, :] <= pos_q[:, None])[None, :, None, None, :]
    s_c = jnp.einsum('btgqd,bngd->btgqn', q, kc, preferred_element_type=f32) * scale
    p_c = jax.nn.softmax(jnp.where(m_c, s_c, NEG), axis=-1) * m_c
    o_c = jnp.einsum('btgqn,bngd->btgqd', p_c, vc.astype(f32))
    r = CMP_LEN // CMP_STRIDE
    n_sub = n_cmp + r - 1
    p_g = jnp.sum(p_c, axis=3)
    p_pad = jnp.pad(p_g, ((0, 0), (0, 0), (0, 0), (r - 1, r - 1)))
    sub_score = p_pad[..., r - 1:r - 1 + n_sub]
    for j in range(1, r):
        sub_score = sub_score + p_pad[..., r - 1 - j:r - 1 - j + n_sub]
    spb = SEL_BLOCK // CMP_STRIDE
    n_sel = ks.shape[2]
    sub_score = jnp.pad(sub_score, ((0, 0), (0, 0), (0, 0), (0, n_sel * spb - n_sub)))
    imp = sub_score.reshape(b, tq, N_KV, n_sel, spb).sum(-1)
    blk_q = (pos_q // SEL_BLOCK)[None, :, None, None]
    j_idx = jnp.arange(n_sel)
    forced = (j_idx == 0) | ((j_idx <= blk_q) & (j_idx > blk_q - N_LOCAL))
    score = jnp.where(j_idx <= blk_q, jnp.where(forced, FORCED, imp), INVALID)
    top_s, idx = lax.top_k(score, min(SEL_TOPK, n_sel))
    idx_t = idx.transpose(0, 2, 1, 3)
    ok_t = (top_s > 0.5 * INVALID).transpose(0, 2, 1, 3)
    bi = jnp.arange(b)[:, None, None, None]
    gi = jnp.arange(N_KV)[None, :, None, None]
    k_g = ks[bi, gi, idx_t]
    v_g = vs[bi, gi, idx_t]
    key_pos = idx_t[..., None] * SEL_BLOCK + jnp.arange(SEL_BLOCK)
    m_s = ((key_pos <= pos_q[None, None, :, None, None]) & ok_t[..., None])[:, :, :, None]
    q_t = q.transpose(0, 2, 1, 3, 4)
    s_s = jnp.einsum('bgtqd,bgtkpd->bgtqkp', q_t, k_g, preferred_element_type=f32) * scale
    s_s = jnp.where(m_s, s_s, NEG)
    p_s = jax.nn.softmax(s_s.reshape(s_s.shape[:4] + (-1,)), axis=-1).reshape(s_s.shape)
    o_s = jnp.einsum('bgtqkp,bgtkpd->btgqd', p_s, v_g.astype(f32))
    dist = pos_q[:, None] - pos_w[None, :]
    m_w = ((dist >= 0) & (dist < WINDOW) & (pos_w[None, :] >= 0))[None, :, None, None, :]
    s_w = jnp.einsum('btgqd,bsgd->btgqs', q, kw, preferred_element_type=f32) * scale
    p_w = jax.nn.softmax(jnp.where(m_w, s_w, NEG), axis=-1)
    o_w = jnp.einsum('btgqs,bsgd->btgqd', p_w, vw.astype(f32))
    out = gates[..., 0:1] * o_c + gates[..., 1:2] * o_s + gates[..., 2:3] * o_w
    return out.reshape(b, tq, D_NSA).astype(q.dtype)


def nsa_prompt(q, gates, win, ctx):
    kc, vc, ks, vs = ctx
    b, t = q.shape[:2]
    kw_pad = jnp.pad(win, ((0, 0), (WINDOW, 0), (0, 0), (0, 0), (0, 0)))

    def block(i):
        start = i * Q_BLOCK
        q_i = lax.dynamic_slice_in_dim(q, start, Q_BLOCK, axis=1)
        g_i = lax.dynamic_slice_in_dim(gates, start, Q_BLOCK, axis=1)
        w_i = lax.dynamic_slice_in_dim(kw_pad, start, WINDOW + Q_BLOCK, axis=1)
        pos_q = start + jnp.arange(Q_BLOCK)
        pos_w = start - WINDOW + jnp.arange(WINDOW + Q_BLOCK)
        return nsa_core(q_i, pos_q, g_i, kc, vc, ks, vs, w_i[:, :, 0], w_i[:, :, 1], pos_w)

    out = lax.map(block, jnp.arange(t // Q_BLOCK))
    return jnp.moveaxis(out, 0, 1).reshape(b, t, D_NSA)


def wkv7_scan(r, w, k, v, a, bv, s0):
    def step(s, inp):
        r_t, w_t, k_t, v_t, a_t, b_t = inp
        sa = jnp.einsum('bhvk,bhk->bhv', s, a_t)
        s = s * w_t[:, :, None, :] + sa[..., None] * b_t[:, :, None, :] + v_t[..., None] * k_t[:, :, None, :]
        return s, jnp.einsum('bhvk,bhk->bhv', s, r_t)
    xs = tuple(jnp.moveaxis(z, 1, 0) for z in (r, w, k, v, a, bv))
    s_t, ys = lax.scan(step, s0, xs)
    return jnp.moveaxis(ys, 0, 1), s_t


def rwkv7_mix(pr, shift_prev, s0, mu, w0, w_w2, a0, w_a2, w_g2, k_k, k_a, r_k, ln_w, ln_b):
    f32 = jnp.float32
    b, t = pr.shape[:2]
    prev = jnp.concatenate([shift_prev[:, None, :].astype(pr.dtype), pr[:, :-1]], axis=1)
    xm = pr + (prev - pr) * mu
    c1, c2, c3 = D_RWKV, 2 * D_RWKV, 3 * D_RWKV
    c4 = c3 + DECAY_LORA
    c5 = c4 + A_LORA
    r, k, v = xm[..., :c1], xm[..., c1:c2], xm[..., c2:c3]
    wd, ad, gd = xm[..., c3:c4], xm[..., c4:c5], xm[..., c5:]
    w_log = -jax.nn.softplus(-(w0 + jnp.tanh(wd) @ w_w2).astype(f32)) - 0.5
    decay = jnp.exp(-jnp.exp(w_log))
    a = jax.nn.sigmoid((a0 + ad @ w_a2).astype(f32))
    g = jax.nn.sigmoid(gd) @ w_g2

    def heads(z):
        return z.astype(f32).reshape(b, t, RWKV_HEADS, RWKV_HD)

    kk = heads(k * k_k)
    kk = kk * lax.rsqrt(jnp.sum(kk * kk, axis=-1, keepdims=True) + 1e-12)
    k_h = heads(k.astype(f32) * (1 + (a - 1) * k_a))
    r_h, v_h, a_h = heads(r), heads(v), heads(a)
    y, s_t = wkv7_scan(r_h, heads(decay), k_h, v_h, -kk, kk * a_h, s0.astype(f32))
    mean = jnp.mean(y, axis=-1, keepdims=True)
    var = jnp.mean(jnp.square(y - mean), axis=-1, keepdims=True)
    yn = ((y - mean) * lax.rsqrt(var + GN_EPS)).reshape(b, t, D_RWKV) * ln_w + ln_b
    bonus = jnp.sum(r_h * k_h * r_k, axis=-1, keepdims=True) * v_h
    out = (yn + bonus.reshape(b, t, D_RWKV)) * g
    return out.astype(pr.dtype), s_t.astype(s0.dtype), pr[:, -1]


def setup_inputs(seed: int = 0) -> dict:
    key = jax.random.key(seed)
    keys = iter(jax.random.split(key, 40))

    def nrm(shape, scale):
        return jax.random.normal(next(keys), shape, jnp.float32) * scale

    def unif(shape, lo, hi):
        return jax.random.uniform(next(keys), shape, jnp.float32, lo, hi)

    n_pages = PAST_LEN // PAGE_SIZE
    n_used = DEC_BATCH * n_pages
    n_pool = n_used + max(1, n_used // 4)
    win_buf = min(WINDOW, PAST_LEN)
    perm = jax.random.permutation(next(keys), n_pool)
    page_table = perm[:n_used].reshape(DEC_BATCH, n_pages).astype(jnp.int32)
    d = D_MODEL
    return {
        'x_prompt': nrm((BATCH, SEQ, d), 1.0),
        'x_sample': nrm((DEC_BATCH, DEC_SEQ, d), 1.0),
        'cache_nsa_kv': nrm((DEPTH, n_pool, PAGE_SIZE, N_KV_SLOTS, N_KV, NSA_HD), 1.0),
        'state_win_kv': nrm((DEPTH, DEC_BATCH, win_buf, 2, N_KV, NSA_HD), 1.0),
        'state_wkv': nrm((DEPTH, DEC_BATCH, RWKV_HEADS, RWKV_HD, RWKV_HD), 0.5),
        'state_shift': nrm((DEPTH, DEC_BATCH, RWKV_COLS), 1.0),
        'page_table': page_table,
        'c_prompt': nrm((BATCH, d), 1.0),
        'c_sample': nrm((DEC_BATCH, d), 1.0),
        'w_ada': nrm((DEPTH, d, N_MOD * d), 0.5 * d ** -0.5),
        'b_ada': nrm((DEPTH, N_MOD * d), 0.02),
        'norm_g': 1.0 + nrm((DEPTH, 3, d), 0.02),
        'ffn_wi': nrm((DEPTH, 2, d, 2 * D_FF), d ** -0.5),
        'ffn_wo': nrm((DEPTH, 2, D_FF, d), D_FF ** -0.5),
        'w_in': nrm((DEPTH, d, IN_COLS), d ** -0.5),
        'w_out': nrm((DEPTH, D_MIX, d), D_MIX ** -0.5),
        'q_norm_g': 1.0 + nrm((DEPTH, NSA_HD), 0.02),
        'k_norm_g': 1.0 + nrm((DEPTH, 3, NSA_HD), 0.02),
        'cmp_pe': nrm((DEPTH, 2, CMP_LEN, NSA_HD), 0.1),
        'cmp_w1': nrm((DEPTH, 2, CMP_LEN, NSA_HD, CMP_HID), (CMP_LEN * NSA_HD) ** -0.5),
        'cmp_b1': nrm((DEPTH, 2, CMP_HID), 0.02),
        'cmp_w2': nrm((DEPTH, 2, CMP_HID, NSA_HD), 1.5 * CMP_HID ** -0.5),
        'rwkv_mu': unif((DEPTH, RWKV_COLS), 0.0, 1.0),
        'rwkv_w0': unif((DEPTH, D_RWKV), -6.5, -1.5),
        'rwkv_w_w2': nrm((DEPTH, DECAY_LORA, D_RWKV), 0.5 * DECAY_LORA ** -0.5),
        'rwkv_a0': nrm((DEPTH, D_RWKV), 0.5),
        'rwkv_w_a2': nrm((DEPTH, A_LORA, D_RWKV), 0.5 * A_LORA ** -0.5),
        'rwkv_w_g2': nrm((DEPTH, G_LORA, D_RWKV), G_LORA ** -0.5),
        'rwkv_k_k': 0.85 + nrm((DEPTH, D_RWKV), 0.05),
        'rwkv_k_a': 1.0 + nrm((DEPTH, D_RWKV), 0.05),
        'rwkv_r_k': nrm((DEPTH, RWKV_HEADS, RWKV_HD), 0.1),
        'rwkv_ln_w': 1.0 + nrm((DEPTH, D_RWKV), 0.02),
        'rwkv_ln_b': nrm((DEPTH, D_RWKV), 0.02),
    }


def reference(x_prompt, x_sample, cache_nsa_kv, state_win_kv, state_wkv, state_shift, page_table,
              c_prompt, c_sample, w_ada, b_ada, norm_g, ffn_wi, ffn_wo, w_in, w_out,
              q_norm_g, k_norm_g, cmp_pe, cmp_w1, cmp_b1, cmp_w2,
              rwkv_mu, rwkv_w0, rwkv_w_w2, rwkv_a0, rwkv_w_a2, rwkv_w_g2,
              rwkv_k_k, rwkv_k_a, rwkv_r_k, rwkv_ln_w, rwkv_ln_b):
    bp, tp = x_prompt.shape[:2]
    bs, ts = x_sample.shape[:2]
    win_buf = state_win_kv.shape[2]
    win_keep = min(WINDOW, tp)
    pos_q_s = PAST_LEN + jnp.arange(ts)
    pos_w_s = PAST_LEN - win_buf + jnp.arange(win_buf + ts)
    xp, xs = x_prompt, x_sample
    kv_p, kv_s, win_p, win_s = [], [], [], []
    wkv_p, wkv_s, sh_p, sh_s = [], [], [], []
    for l in range(DEPTH):
        rw = (rwkv_mu[l], rwkv_w0[l], rwkv_w_w2[l], rwkv_a0[l], rwkv_w_a2[l], rwkv_w_g2[l],
              rwkv_k_k[l], rwkv_k_a[l], rwkv_r_k[l], rwkv_ln_w[l], rwkv_ln_b[l])
        cmp_p = (cmp_pe[l], cmp_w1[l], cmp_b1[l], cmp_w2[l])
        mp = modulation(c_prompt, w_ada[l], b_ada[l])
        ms = modulation(c_sample, w_ada[l], b_ada[l])
        xp = xp + 0.5 * mp[:, 2] * swiglu(ada_norm(xp, mp, 0, norm_g[l, 0]), ffn_wi[l, 0], ffn_wo[l, 0])
        xs = xs + 0.5 * ms[:, 2] * swiglu(ada_norm(xs, ms, 0, norm_g[l, 0]), ffn_wi[l, 0], ffn_wo[l, 0])
        pp = ada_norm(xp, mp, 1, norm_g[l, 1]) @ w_in[l]
        q, gts, rows, win = nsa_project(pp[..., :NSA_COLS], q_norm_g[l], k_norm_g[l])
        ctx = nsa_context(rows, k_norm_g[l], *cmp_p)
        o_nsa = nsa_prompt(q, gts, win, ctx)
        o_rw, s_fin, sh = rwkv7_mix(pp[..., NSA_COLS:], jnp.zeros((bp, RWKV_COLS), pp.dtype),
                                    jnp.zeros((bp, RWKV_HEADS, RWKV_HD, RWKV_HD), jnp.float32), *rw)
        xp = xp + mp[:, 5] * (jnp.concatenate([o_nsa, o_rw], axis=-1) @ w_out[l])
        kv_p.append(rows.reshape(bp, tp // PAGE_SIZE, PAGE_SIZE, N_KV_SLOTS, N_KV, NSA_HD))
        win_p.append(win[:, tp - win_keep:])
        wkv_p.append(s_fin)
        sh_p.append(sh)
        ps = ada_norm(xs, ms, 1, norm_g[l, 1]) @ w_in[l]
        q, gts, rows, win = nsa_project(ps[..., :NSA_COLS], q_norm_g[l], k_norm_g[l])
        past = cache_nsa_kv[l][page_table].reshape(bs, PAST_LEN, N_KV_SLOTS, N_KV, NSA_HD)
        ctx = nsa_context(jnp.concatenate([past, rows.astype(past.dtype)], axis=1), k_norm_g[l], *cmp_p)
        win_all = jnp.concatenate([state_win_kv[l], win.astype(state_win_kv.dtype)], axis=1)
        o_nsa = nsa_core(q, pos_q_s, gts, *ctx, win_all[:, :, 0], win_all[:, :, 1], pos_w_s)
        o_rw, s_fin, sh = rwkv7_mix(ps[..., NSA_COLS:], state_shift[l], state_wkv[l], *rw)
        xs = xs + ms[:, 5] * (jnp.concatenate([o_nsa.astype(o_rw.dtype), o_rw], axis=-1) @ w_out[l])
        kv_s.append(rows)
        win_s.append(win_all[:, ts:])
        wkv_s.append(s_fin)
        sh_s.append(sh)
        xp = xp + 0.5 * mp[:, 8] * swiglu(ada_norm(xp, mp, 2, norm_g[l, 2]), ffn_wi[l, 1], ffn_wo[l, 1])
        xs = xs + 0.5 * ms[:, 8] * swiglu(ada_norm(xs, ms, 2, norm_g[l, 2]), ffn_wi[l, 1], ffn_wo[l, 1])
    y_prompt = xp
    y_sample = xs
    kv_prompt = jnp.stack(kv_p)
    kv_sample = jnp.stack(kv_s)
    win_prompt = jnp.stack(win_p)
    win_sample = jnp.stack(win_s)
    wkv_prompt = jnp.stack(wkv_p)
    wkv_sample = jnp.stack(wkv_s)
    shift_prompt = jnp.stack(sh_p)
    shift_sample = jnp.stack(sh_s)
    return (y_prompt, y_sample, kv_prompt, kv_sample, win_prompt, win_sample, wkv_prompt, wkv_sample, shift_prompt, shift_sample)
```

```python
import functools
import math

import jax
import jax.numpy as jnp
from jax import lax
from jax.experimental import pallas as pl
from jax.experimental.pallas import tpu as pltpu

F32 = jnp.float32
BF16 = jnp.bfloat16

NSA_HD = 128
N_KV = 2
QPG = 4
N_QH = N_KV * QPG
CMP_LEN = 32
CMP_STRIDE = 16
CMP_HID = 2 * NSA_HD
SEL_BLOCK = 64
SEL_TOPK = 16
N_LOCAL = 2
WINDOW = 512
PAGE = 128
RWKV_HD = 64
N_MOD = 9
NORM_EPS = 1e-6
GN_EPS = 64e-5
NEG = -1e30
FORCED = 1e6
INVALID = -1e6

LANES = 128
VMEM_LIMIT = 56 * 1024 * 1024

P_Q = 0
P_KV = 1024
P_AUX = 2560
P_AUX_W = 512
P_RKV = 3072
P_COLS = 6144
AUX_WD = 128
AUX_AD = 192
AUX_GD = 256


def _cparams(sem):
    return pltpu.CompilerParams(dimension_semantics=sem, vmem_limit_bytes=VMEM_LIMIT)


def _mm(a, b, dims=((1,), (0,))):
    return lax.dot_general(a, b, (dims, ((), ())), preferred_element_type=F32)


NT = ((1,), (1,))
TN = ((0,), (0,))


def _split2(x):
    hi = x.astype(BF16)
    lo = (x - hi.astype(F32)).astype(BF16)
    return hi, lo


def _split3(x):
    hi = x.astype(BF16)
    r1 = x - hi.astype(F32)
    mid = r1.astype(BF16)
    lo = (r1 - mid.astype(F32)).astype(BF16)
    return hi, mid, lo


def _mm3(a, b, dims=((1,), (0,))):
    ah, al = _split2(a)
    bh, bl = _split2(b)
    return _mm(ah, bh, dims) + (_mm(ah, bl, dims) + _mm(al, bh, dims))


def _mm_lhs_exact(a, b_bf16, dims=((1,), (0,))):
    a1, a2, a3 = _split3(a)
    return _mm(a1, b_bf16, dims) + (_mm(a2, b_bf16, dims) + _mm(a3, b_bf16, dims))


def _mm_rhs_exact(a_bf16, b, dims=((1,), (0,))):
    b1, b2, b3 = _split3(b)
    return _mm(a_bf16, b1, dims) + (_mm(a_bf16, b2, dims) + _mm(a_bf16, b3, dims))


def _silu(x):
    return x * jax.nn.sigmoid(x)


def _rms(x, g):
    return x * lax.rsqrt(jnp.mean(x * x, axis=-1, keepdims=True) + NORM_EPS) * g


def _ada_norm(x, m_ref, slot, g):
    return _rms(x, g) * (1.0 + m_ref[3 * slot + 1]) + m_ref[3 * slot]


def _mod_kernel(c_ref, w_ref, b_ref, o_ref):
    s = _silu(c_ref[...]).astype(BF16)
    o_ref[0] = _mm(s, w_ref[0].astype(BF16)) + b_ref[0]


def _modulation(c_all, w_ada, b_ada):
    depth, d, n = w_ada.shape
    rows = c_all.shape[0]
    tn = 1024
    return pl.pallas_call(
        _mod_kernel,
        grid=(depth, n // tn),
        in_specs=[pl.BlockSpec((rows, d), lambda l, j: (0, 0)),
                  pl.BlockSpec((1, d, tn), lambda l, j: (l, 0, j)),
                  pl.BlockSpec((1, 1, tn), lambda l, j: (l, 0, j))],
        out_specs=pl.BlockSpec((1, rows, tn), lambda l, j: (l, 0, j)),
        out_shape=jax.ShapeDtypeStruct((depth, rows, n), F32),
        compiler_params=_cparams(("arbitrary", "arbitrary")),
    )(c_all, w_ada, b_ada.reshape(depth, 1, n))


def _ffn_kernel(x_ref, m_ref, g_ref, wg_ref, wu_ref, wo_ref, o_ref, h_scr, acc_scr, *, slot):
    f = pl.program_id(1)

    @pl.when(f == 0)
    def _():
        h_scr[...] = _ada_norm(x_ref[...], m_ref, slot, g_ref[...]).astype(BF16)
        acc_scr[...] = jnp.zeros_like(acc_scr)

    h = h_scr[...]
    gate = _mm(h, wg_ref[...])
    up = _mm(h, wu_ref[...])
    act = (_silu(gate) * up).astype(BF16)
    acc_scr[...] += _mm(act, wo_ref[...])

    @pl.when(f == pl.num_programs(1) - 1)
    def _():
        o_ref[...] = x_ref[...] + 0.5 * m_ref[3 * slot + 2] * acc_scr[...]


def _ffn(x, mod, g, wi, wo, slot, tm, tiles_per_seq):
    m_rows, d = x.shape
    d_ff = wo.shape[0]
    tf = 512
    nf = d_ff // tf
    mr = mod.shape[2]
    return pl.pallas_call(
        functools.partial(_ffn_kernel, slot=slot),
        grid=(m_rows // tm, nf),
        in_specs=[pl.BlockSpec((tm, d), lambda i, f: (i, 0)),
                  pl.BlockSpec((None, N_MOD, mr, d), lambda i, f: (i // tiles_per_seq, 0, 0, 0)),
                  pl.BlockSpec((1, d), lambda i, f: (0, 0)),
                  pl.BlockSpec((d, tf), lambda i, f: (0, f)),
                  pl.BlockSpec((d, tf), lambda i, f: (0, nf + f)),
                  pl.BlockSpec((tf, d), lambda i, f: (f, 0))],
        out_specs=pl.BlockSpec((tm, d), lambda i, f: (i, 0)),
        out_shape=jax.ShapeDtypeStruct((m_rows, d), F32),
        scratch_shapes=[pltpu.VMEM((tm, d), BF16), pltpu.VMEM((tm, d), F32)],
        compiler_params=_cparams(("arbitrary", "arbitrary")),
    )(x, mod, g, wi, wi, wo)


def _proj_kernel(x_ref, m_ref, g_ref, w_ref, o_ref, h_scr):
    @pl.when(pl.program_id(1) == 0)
    def _():
        h_scr[...] = _ada_norm(x_ref[...], m_ref, 1, g_ref[...]).astype(BF16)

    o_ref[...] = _mm(h_scr[...], w_ref[...])


def _proj(x, mod, g, w, tm, tiles_per_seq):
    m_rows, d = x.shape
    n = w.shape[1]
    tn = 1536
    mr = mod.shape[2]
    return pl.pallas_call(
        _proj_kernel,
        grid=(m_rows // tm, n // tn),
        in_specs=[pl.BlockSpec((tm, d), lambda i, j: (i, 0)),
                  pl.BlockSpec((None, N_MOD, mr, d), lambda i, j: (i // tiles_per_seq, 0, 0, 0)),
                  pl.BlockSpec((1, d), lambda i, j: (0, 0)),
                  pl.BlockSpec((d, tn), lambda i, j: (0, j))],
        out_specs=pl.BlockSpec((tm, tn), lambda i, j: (i, j)),
        out_shape=jax.ShapeDtypeStruct((m_rows, n), F32),
        scratch_shapes=[pltpu.VMEM((tm, d), BF16)],
        compiler_params=_cparams(("arbitrary", "arbitrary")),
    )(x, mod, g, w)


def _rms_heads(x, g):
    outs = []
    for h in range(x.shape[1] // NSA_HD):
        outs.append(_rms(x[:, h * NSA_HD:(h + 1) * NSA_HD], g))
    return jnp.concatenate(outs, axis=1)


def _nsa_prep_kernel(q_ref, kva_ref, kvb_ref, kvc_ref, gt_ref, qg_ref, kg_ref,
                     qn_ref, rows_ref, win_ref, selbf_ref, winbf_ref, gates_ref):
    qn_ref[...] = _rms_heads(q_ref[...], qg_ref[...]).astype(qn_ref.dtype)
    kvb = kvb_ref[...]
    ksel = _rms_heads(kvb[:, :2 * NSA_HD], kg_ref[1:2, :])
    selrows = jnp.concatenate([ksel, kvb[:, 2 * NSA_HD:]], axis=1)
    rows_ref[...] = jnp.concatenate([kva_ref[...], selrows], axis=1)
    selbf_ref[...] = selrows.astype(selbf_ref.dtype)
    kvc = kvc_ref[...]
    kwin = _rms_heads(kvc[:, :2 * NSA_HD], kg_ref[2:3, :])
    winrows = jnp.concatenate([kwin, kvc[:, 2 * NSA_HD:]], axis=1)
    win_ref[...] = winrows
    winbf_ref[...] = winrows.astype(winbf_ref.dtype)
    gates_ref[...] = jax.nn.sigmoid(gt_ref[...])


def _nsa_prep(p, q_g, k_g, tm, act_dtype):
    m_rows = p.shape[0]
    row = lambda w, j: pl.BlockSpec((tm, w), lambda i: (i, j))
    return pl.pallas_call(
        _nsa_prep_kernel,
        grid=(m_rows // tm,),
        in_specs=[row(1024, 0), row(512, 2), row(512, 3), row(512, 4), row(LANES, P_AUX // LANES),
                  pl.BlockSpec((1, NSA_HD), lambda i: (0, 0)),
                  pl.BlockSpec((3, NSA_HD), lambda i: (0, 0))],
        out_specs=[row(1024, 0), row(1024, 0), row(512, 0), row(512, 0), row(512, 0), row(LANES, 0)],
        out_shape=[jax.ShapeDtypeStruct((m_rows, 1024), act_dtype),
                   jax.ShapeDtypeStruct((m_rows, 1024), F32),
                   jax.ShapeDtypeStruct((m_rows, 512), F32),
                   jax.ShapeDtypeStruct((m_rows, 512), act_dtype),
                   jax.ShapeDtypeStruct((m_rows, 512), act_dtype),
                   jax.ShapeDtypeStruct((m_rows, LANES), F32)],
        compiler_params=_cparams(("arbitrary",)),
    )(p, p, p, p, p, q_g, k_g)


def _cmp_kernel(pt_ref, *refs, n_sub):
    pages = refs[:8]
    w1_ref, pe_ref, b1_ref, w2_ref, kg_ref, kc_ref, vc_ref, x_scr = refs[8:]
    p = pl.program_id(1)
    base = pl.multiple_of(p * 16, 16)
    for c in range(4):
        for l in range(CMP_STRIDE):
            lo = pages[c][pl.ds(l, 8, stride=CMP_STRIDE), :]
            hi = pages[4 + c][pl.ds(l, 8, stride=CMP_STRIDE), :]
            x_scr[c, pl.ds(base, 16), l * NSA_HD:(l + 1) * NSA_HD] = (
                jnp.concatenate([lo, hi], axis=0).astype(BF16))

    @pl.when(p == pl.num_programs(1) - 1)
    def _():
        row = lax.broadcasted_iota(jnp.int32, (n_sub, NSA_HD), 0)
        for kv in range(2):
            w1 = w1_ref[kv]
            const = b1_ref[kv]
            for j in range(2):
                pe = jnp.broadcast_to(pe_ref[kv, j], (8, CMP_STRIDE * NSA_HD)).astype(BF16)
                const = const + _mm(pe, w1[:, j * CMP_HID:(j + 1) * CMP_HID])[0:1]
            for g in range(N_KV):
                ab = _mm(x_scr[kv * 2 + g], w1)
                nxt = pltpu.roll(ab[:, CMP_HID:], n_sub - 1, 0)
                acc = ab[:, :CMP_HID] + nxt + const
                o = _mm(jax.nn.gelu(acc).astype(BF16), w2_ref[kv])
                if kv == 0:
                    o = _rms(o, kg_ref[0:1, :])
                o = jnp.where(row < n_sub - 1, o, 0.0)
                if kv == 0:
                    kc_ref[g] = o
                else:
                    vc_ref[g] = o


def _compress(pool, page_idx, w1cat, pe2, b1, w2, k_g):
    nb, n_pages = page_idx.shape
    n_sub = n_pages * (PAGE // CMP_STRIDE)
    page = lambda k, c: pl.BlockSpec((None, PAGE, NSA_HD), lambda b, p, pt: (pt[b, 2 * p + k], 0, c))
    page_specs = [page(k, c) for k in range(2) for c in range(4)]
    full = lambda shape: pl.BlockSpec(shape, lambda b, p, pt: (0,) * len(shape))
    out = pl.BlockSpec((None, N_KV, n_sub, NSA_HD), lambda b, p, pt: (b, 0, 0, 0))
    return pl.pallas_call(
        functools.partial(_cmp_kernel, n_sub=n_sub),
        grid_spec=pltpu.PrefetchScalarGridSpec(
            num_scalar_prefetch=1,
            grid=(nb, n_pages // 2),
            in_specs=page_specs + [full(w1cat.shape), full(pe2.shape), full(b1.shape),
                      full(w2.shape), full(k_g.shape)],
            out_specs=[out, out],
            scratch_shapes=[pltpu.VMEM((4, n_sub, CMP_STRIDE * NSA_HD), BF16)]),
        out_shape=[jax.ShapeDtypeStruct((nb, N_KV, n_sub, NSA_HD), F32)] * 2,
        compiler_params=_cparams(("arbitrary", "arbitrary")),
    )(page_idx, *([pool] * 8), w1cat, pe2, b1, w2, k_g)


def _imp_matrix(n_sub, n_cmp, width):
    n = lax.broadcasted_iota(jnp.int32, (n_sub, width), 0)
    j = lax.broadcasted_iota(jnp.int32, (n_sub, width), 1)
    spb = SEL_BLOCK // CMP_STRIDE
    m = jnp.where(n // spb == j, 1.0, 0.0) + jnp.where((n + 1) // spb == j, 1.0, 0.0)
    return jnp.where(n < n_cmp, m, 0.0).astype(BF16)


def _topk_mask_t(st_scr, n_iter, width):
    jj = st_scr.shape[0]
    st = st_scr[...]
    jrow = lax.broadcasted_iota(jnp.int32, (jj, width), 0)

    def body(jp, rank):
        r = st_scr[pl.ds(jp, 1), :]
        beats = (r > st) | ((r == st) & (jp < jrow))
        return rank + jnp.where(beats, 1.0, 0.0)

    return lax.fori_loop(0, n_iter, body, jnp.zeros((jj, width), F32))


def _softmax_rows(s, ok):
    sm = jnp.where(ok, s, NEG)
    mx = jnp.max(sm, axis=-1, keepdims=True)
    e = jnp.where(ok, jnp.exp(sm - mx), 0.0)
    den = jnp.sum(e, axis=-1, keepdims=True)
    return e / jnp.where(den > 0.0, den, 1.0)


def _nsa_prompt_kernel(q_ref, gt_ref, sel_ref, win_ref, kc_ref, vc_ref, o_ref, st_scr,
                       *, t_len, n_sub, ck):
    tq = SEL_BLOCK
    qi = pl.program_id(1)
    s0 = qi * tq
    scale = NSA_HD ** -0.5
    n_cmp = n_sub - 1
    rows = QPG * tq
    q = q_ref[...]
    rowpos = s0 + lax.broadcasted_iota(jnp.int32, (rows, 1), 0) % tq

    def q_group(g):
        return jnp.concatenate(
            [q[:, (g * QPG + h) * NSA_HD:(g * QPG + h + 1) * NSA_HD] for h in range(QPG)], axis=0)

    imp_m = _imp_matrix(n_sub, n_cmp, LANES)
    n_idx = lax.broadcasted_iota(jnp.int32, (rows, n_sub), 1)
    ok_c = (n_idx * CMP_STRIDE + (CMP_LEN - 1) <= rowpos) & (n_idx < n_cmp)
    o_cmp, scores = [], []
    jl = lax.broadcasted_iota(jnp.int32, (tq, LANES), 1)
    forced = (jl == 0) | ((jl <= qi) & (jl > qi - N_LOCAL))
    for g in range(N_KV):
        qg = q_group(g)
        s_c = _mm(qg, kc_ref[g].astype(BF16), NT) * scale
        p_c = _softmax_rows(s_c, ok_c)
        o_cmp.append(_mm(p_c.astype(BF16), vc_ref[g].astype(BF16)))
        p_g = p_c[0:tq] + p_c[tq:2 * tq] + p_c[2 * tq:3 * tq] + p_c[3 * tq:4 * tq]
        imp = _mm_lhs_exact(p_g, imp_m)
        scores.append(jnp.where(jl <= qi, jnp.where(forced, FORCED, imp), INVALID))
    st_scr[...] = jnp.concatenate(scores, axis=0).T
    rank = _topk_mask_t(st_scr, qi + 1, LANES)
    jrow = lax.broadcasted_iota(jnp.int32, (LANES, N_KV * tq), 0)
    sel_t = jnp.where((rank < SEL_TOPK) & (jrow <= qi), 1.0, 0.0)
    sel = sel_t.T.astype(BF16)

    n_chunks = (s0 + tq + ck - 1) // ck
    qgs = [q_group(g) for g in range(N_KV)]

    def chunk(c, carry):
        k0 = pl.multiple_of(c * ck, ck)
        keypos = k0 + lax.broadcasted_iota(jnp.int32, (1, ck), 1)
        ej = lax.broadcasted_iota(jnp.int32, (LANES, ck), 0)
        expand = jnp.where(ej == keypos // SEL_BLOCK, 1.0, 0.0).astype(BF16)
        mask2 = _mm(sel, expand)
        causal = keypos <= rowpos
        new = []
        for g in range(N_KV):
            m_i, l_i, acc = carry[g]
            kk = sel_ref[pl.ds(k0, ck), g * NSA_HD:(g + 1) * NSA_HD]
            vv = sel_ref[pl.ds(k0, ck), (N_KV + g) * NSA_HD:(N_KV + g + 1) * NSA_HD]
            mg = mask2[g * tq:(g + 1) * tq]
            ok = (jnp.concatenate([mg] * QPG, axis=0) > 0.5) & causal
            s = jnp.where(ok, _mm(qgs[g], kk, NT) * scale, NEG)
            m_new = jnp.maximum(m_i, jnp.max(s, axis=-1, keepdims=True))
            pr = jnp.where(ok, jnp.exp(s - m_new), 0.0)
            alpha = jnp.exp(m_i - m_new)
            l_new = alpha * l_i + jnp.sum(pr, axis=-1, keepdims=True)
            acc_new = alpha * acc + _mm(pr.astype(BF16), vv)
            new.append((m_new, l_new, acc_new))
        return tuple(new)

    init = tuple((jnp.full((rows, 1), NEG, F32), jnp.zeros((rows, 1), F32),
                  jnp.zeros((rows, NSA_HD), F32)) for _ in range(N_KV))
    fin = lax.fori_loop(0, n_chunks, chunk, init)

    wlen = WINDOW + 2 * tq
    w0 = pl.multiple_of(jnp.clip(s0 - WINDOW, 0, t_len - wlen), tq)
    wpos = w0 + lax.broadcasted_iota(jnp.int32, (1, wlen), 1)
    dist = rowpos - wpos
    ok_w = (dist >= 0) & (dist < WINDOW)
    gt = gt_ref[...]
    outs = []
    for g in range(N_KV):
        kw = win_ref[pl.ds(w0, wlen), g * NSA_HD:(g + 1) * NSA_HD]
        vw = win_ref[pl.ds(w0, wlen), (N_KV + g) * NSA_HD:(N_KV + g + 1) * NSA_HD]
        p_w = _softmax_rows(_mm(qgs[g], kw, NT) * scale, ok_w)
        o_w = _mm(p_w.astype(BF16), vw)
        _, l_i, acc = fin[g]
        o_s = acc / l_i
        for h in range(QPG):
            c0 = (g * QPG + h) * 3
            r0 = slice(h * tq, (h + 1) * tq)
            outs.append(gt[:, c0:c0 + 1] * o_cmp[g][r0] + gt[:, c0 + 1:c0 + 2] * o_s[r0]
                        + gt[:, c0 + 2:c0 + 3] * o_w[r0])
    o_ref[...] = jnp.concatenate(outs, axis=1).astype(o_ref.dtype)


def _nsa_prompt(qn, gates, selbf, winbf, kc, vc, nb, t_len):
    n_sub = kc.shape[2]
    tq = SEL_BLOCK
    nq = t_len // tq
    ck = 256
    per_b = lambda w: pl.BlockSpec((None, t_len, w), lambda b, i: (b, 0, 0))
    cmp_spec = pl.BlockSpec((None, N_KV, n_sub, NSA_HD), lambda b, i: (b, 0, 0, 0))
    return pl.pallas_call(
        functools.partial(_nsa_prompt_kernel, t_len=t_len, n_sub=n_sub, ck=ck),
        grid=(nb, nq),
        in_specs=[pl.BlockSpec((tq, 1024), lambda b, i: (b * nq + i, 0)),
                  pl.BlockSpec((tq, LANES), lambda b, i: (b * nq + i, 0)),
                  per_b(512), per_b(512), cmp_spec, cmp_spec],
        out_specs=pl.BlockSpec((tq, 1024), lambda b, i: (b * nq + i, 0)),
        out_shape=jax.ShapeDtypeStruct((nb * t_len, 1024), BF16),
        scratch_shapes=[pltpu.VMEM((LANES, N_KV * tq), F32)],
        compiler_params=_cparams(("arbitrary", "arbitrary")),
    )(qn, gates, selbf.reshape(nb, t_len, 512), winbf.reshape(nb, t_len, 512), kc, vc)


def _nsa_sample_select_kernel(q_ref, kc_ref, vc_ref, oc_ref, sel_ref, sc_scr, st_scr,
                              *, n_sub, n_sel, ts, past, jw):
    b = pl.program_id(0)
    nb = pl.num_programs(0)
    scale = NSA_HD ** -0.5
    n_cmp = n_sub - 1
    q = q_ref[...]
    imp_m = _imp_matrix(n_sub, n_cmp, jw)
    rows = QPG * ts
    rowpos = past + lax.broadcasted_iota(jnp.int32, (rows, 1), 0) % ts
    n_idx = lax.broadcasted_iota(jnp.int32, (rows, n_sub), 1)
    ok_c = (n_idx * CMP_STRIDE + (CMP_LEN - 1) <= rowpos) & (n_idx < n_cmp)
    jl = lax.broadcasted_iota(jnp.int32, (ts, jw), 1)
    blk = (past + lax.broadcasted_iota(jnp.int32, (ts, 1), 0)) // SEL_BLOCK
    forced = (jl == 0) | ((jl <= blk) & (jl > blk - N_LOCAL))

    @pl.when(b == 0)
    def _():
        sc_scr[...] = jnp.full(sc_scr.shape, INVALID, F32)

    for g in range(N_KV):
        qg = jnp.concatenate(
            [q[:, (g * QPG + h) * NSA_HD:(g * QPG + h + 1) * NSA_HD] for h in range(QPG)],
            axis=0).astype(BF16)
        s_c = _mm(qg, kc_ref[g].astype(BF16), NT) * scale
        p_c = _softmax_rows(s_c, ok_c)
        oc_ref[g * rows:(g + 1) * rows, :] = _mm(p_c.astype(BF16), vc_ref[g].astype(BF16))
        p_g = p_c[0:ts] + p_c[ts:2 * ts] + p_c[2 * ts:3 * ts] + p_c[3 * ts:4 * ts]
        imp = _mm_lhs_exact(p_g, imp_m)
        score = jnp.where((jl <= blk) & (jl < n_sel), jnp.where(forced, FORCED, imp), INVALID)
        sc_scr[pl.ds(pl.multiple_of((b * N_KV + g) * ts, ts), ts), :] = score

    @pl.when(b == nb - 1)
    def _():
        st_scr[...] = sc_scr[...].T
        rank = _topk_mask_t(st_scr, n_sel, LANES)
        sel_t = jnp.where((rank < SEL_TOPK) & (st_scr[...] > 0.5 * INVALID), 1.0, 0.0)
        sel_ref[...] = sel_t.T


def _nsa_sample_select(qn_s, kc, vc, ts, past, n_sel):
    nb = kc.shape[0]
    n_sub = kc.shape[2]
    jw = -(-n_sel // LANES) * LANES
    assert nb * N_KV * ts <= LANES
    cmp_spec = pl.BlockSpec((None, N_KV, n_sub, NSA_HD), lambda b: (b, 0, 0, 0))
    return pl.pallas_call(
        functools.partial(_nsa_sample_select_kernel, n_sub=n_sub, n_sel=n_sel, ts=ts, past=past, jw=jw),
        grid=(nb,),
        in_specs=[pl.BlockSpec((None, ts, 1024), lambda b: (b, 0, 0)), cmp_spec, cmp_spec],
        out_specs=[pl.BlockSpec((None, N_QH * ts, NSA_HD), lambda b: (b, 0, 0)),
                   pl.BlockSpec((LANES, jw), lambda b: (0, 0))],
        out_shape=[jax.ShapeDtypeStruct((nb, N_QH * ts, NSA_HD), F32),
                   jax.ShapeDtypeStruct((LANES, jw), F32)],
        scratch_shapes=[pltpu.VMEM((LANES, jw), F32), pltpu.VMEM((jw, LANES), F32)],
        compiler_params=_cparams(("arbitrary",)),
    )(qn_s.reshape(nb, ts, 1024), kc, vc)


def _nsa_sample_sweep_kernel(pt_ref, *refs, pp, ts, past, n_sel, wbuf):
    pages = refs[:pp]
    (q_ref, gt_ref, sel_ref, oc_ref, rows_ref, wnew_ref, wst_ref, o_ref,
     m_scr, l_scr, acc_scr) = refs[pp:]
    step = pl.program_id(1)
    scale = NSA_HD ** -0.5
    rows = QPG * ts
    jw = sel_ref.shape[1]
    q = q_ref[...]
    qgs = [jnp.concatenate(
        [q[:, (g * QPG + h) * NSA_HD:(g * QPG + h + 1) * NSA_HD] for h in range(QPG)],
        axis=0).astype(BF16) for g in range(N_KV)]
    sel = sel_ref[...].astype(BF16)

    @pl.when(step == 0)
    def _():
        m_scr[...] = jnp.full(m_scr.shape, NEG, F32)
        l_scr[...] = jnp.zeros_like(l_scr)
        acc_scr[...] = jnp.zeros_like(acc_scr)

    def update(g, s, ok, vv):
        r0 = slice(g * rows, (g + 1) * rows)
        s = jnp.where(ok, s, NEG)
        m_i = m_scr[r0]
        m_new = jnp.maximum(m_i, jnp.max(s, axis=-1, keepdims=True))
        pr = jnp.where(ok, jnp.exp(s - m_new), 0.0)
        alpha = jnp.exp(m_i - m_new)
        l_scr[r0] = alpha * l_scr[r0] + jnp.sum(pr, axis=-1, keepdims=True)
        acc_scr[r0] = alpha * acc_scr[r0] + _mm(pr.astype(BF16), vv)
        m_scr[r0] = m_new

    ej = lax.broadcasted_iota(jnp.int32, (jw, PAGE), 0)
    kl = lax.broadcasted_iota(jnp.int32, (1, PAGE), 1)
    for k in range(pp):
        pidx = step * pp + k
        expand = jnp.where(ej == pidx * (PAGE // SEL_BLOCK) + kl // SEL_BLOCK, 1.0, 0.0).astype(BF16)
        mask2 = _mm(sel, expand)
        pg = pages[k]
        for g in range(N_KV):
            kk = pg[:, g * NSA_HD:(g + 1) * NSA_HD].astype(BF16)
            vv = pg[:, (N_KV + g) * NSA_HD:(N_KV + g + 1) * NSA_HD].astype(BF16)
            mg = mask2[g * ts:(g + 1) * ts]
            ok = jnp.concatenate([mg] * QPG, axis=0) > 0.5
            update(g, _mm(qgs[g], kk, NT) * scale, ok, vv)

    @pl.when(step == pl.num_programs(1) - 1)
    def _():
        tpos = lax.broadcasted_iota(jnp.int32, (rows, 1), 0) % ts
        newrows = rows_ref[...]
        pad = jnp.zeros((PAGE - ts, NSA_HD), BF16)
        il = lax.broadcasted_iota(jnp.int32, (1, PAGE), 1)
        last_sel = sel_ref[:, n_sel - 1:n_sel]
        gt = gt_ref[...]
        wlen = wbuf + PAGE
        wl = lax.broadcasted_iota(jnp.int32, (1, wlen), 1)
        dist = (past + tpos) - (past - wbuf + wl)
        ok_w = (dist >= 0) & (dist < WINDOW) & (wl < wbuf + ts)
        wst = wst_ref[...]
        wnew = wnew_ref[...]
        outs = []
        for g in range(N_KV):
            kn = jnp.concatenate(
                [newrows[:, (2 * N_KV + g) * NSA_HD:(2 * N_KV + g + 1) * NSA_HD].astype(BF16), pad], axis=0)
            vn = jnp.concatenate(
                [newrows[:, (3 * N_KV + g) * NSA_HD:(3 * N_KV + g + 1) * NSA_HD].astype(BF16), pad], axis=0)
            lsel = jnp.concatenate([last_sel[g * ts:(g + 1) * ts]] * QPG, axis=0) > 0.5
            ok = (il <= tpos) & (il < ts) & lsel
            update(g, _mm(qgs[g], kn, NT) * scale, ok, vn)
            r0 = slice(g * rows, (g + 1) * rows)
            o_s = acc_scr[r0] / l_scr[r0]
            zpad = jnp.zeros((PAGE - ts, NSA_HD), F32)
            kw = jnp.concatenate([wst[:, g * NSA_HD:(g + 1) * NSA_HD],
                                  wnew[:, g * NSA_HD:(g + 1) * NSA_HD], zpad], axis=0).astype(BF16)
            vw = jnp.concatenate([wst[:, (N_KV + g) * NSA_HD:(N_KV + g + 1) * NSA_HD],
                                  wnew[:, (N_KV + g) * NSA_HD:(N_KV + g + 1) * NSA_HD], zpad],
                                 axis=0).astype(BF16)
            p_w = _softmax_rows(_mm(qgs[g], kw, NT) * scale, ok_w)
            o_w = _mm(p_w.astype(BF16), vw)
            o_c = oc_ref[r0, :]
            for h in range(QPG):
                c0 = (g * QPG + h) * 3
                rh = slice(h * ts, (h + 1) * ts)
                outs.append(gt[:, c0:c0 + 1] * o_c[rh] + gt[:, c0 + 1:c0 + 2] * o_s[rh]
                            + gt[:, c0 + 2:c0 + 3] * o_w[rh])
        o_ref[...] = jnp.concatenate(outs, axis=1)


def _nsa_sample_sweep(pool, page_idx, qn_s, gates_s, sel, o_cmp, rows_s, win_s, win_state,
                      ts, past, n_sel):
    nb, n_pages = page_idx.shape
    pp = 4 if n_pages % 4 == 0 else 2
    wbuf = win_state.shape[1]
    jw = sel.shape[1]
    page = lambda k: pl.BlockSpec((None, PAGE, 512), lambda b, s, pt: (pt[b, pp * s + k], 0, 1))
    per_b = lambda r, w: pl.BlockSpec((None, r, w), lambda b, s, pt: (b, 0, 0))
    return pl.pallas_call(
        functools.partial(_nsa_sample_sweep_kernel, pp=pp, ts=ts, past=past, n_sel=n_sel, wbuf=wbuf),
        grid_spec=pltpu.PrefetchScalarGridSpec(
            num_scalar_prefetch=1,
            grid=(nb, n_pages // pp),
            in_specs=[page(k) for k in range(pp)] + [
                per_b(ts, 1024), per_b(ts, LANES),
                pl.BlockSpec((N_KV * ts, jw), lambda b, s, pt: (b, 0)),
                per_b(N_QH * ts, NSA_HD), per_b(ts, 1024), per_b(ts, 512), per_b(wbuf, 512)],
            out_specs=per_b(ts, 1024),
            scratch_shapes=[pltpu.VMEM((N_QH * ts, 1), F32), pltpu.VMEM((N_QH * ts, 1), F32),
                            pltpu.VMEM((N_QH * ts, NSA_HD), F32)]),
        out_shape=jax.ShapeDtypeStruct((nb, ts, 1024), F32),
        compiler_params=_cparams(("arbitrary", "arbitrary")),
    )(page_idx, *([pool] * pp), qn_s.reshape(nb, ts, 1024), gates_s.reshape(nb, ts, LANES), sel,
      o_cmp, rows_s.reshape(nb, ts, 1024), win_s.reshape(nb, ts, 512), win_state)


def _softplus(z):
    return jnp.maximum(z, 0.0) + jnp.log1p(jnp.exp(-jnp.abs(z)))


def _rwkv_kernel(rkv_ref, aux_ref, sh_rkv_ref, sh_aux_ref, s0_ref, mu_rkv_ref, mu_aux_ref,
                 vec_ref, ww_ref, wa_ref, wg_ref, o_ref, sfin_ref, s_scr, c_rkv, c_aux, *, cs, n_valid):
    ci = pl.program_id(1)
    n_pairs = s_scr.shape[0]
    dr = n_pairs * LANES

    @pl.when(ci == 0)
    def _():
        s_scr[...] = s0_ref[...]
        c_rkv[...] = sh_rkv_ref[...]
        c_aux[...] = sh_aux_ref[...]

    def shift_mix(x, carry, mu):
        first = lax.broadcasted_iota(jnp.int32, x.shape, 0) == 0
        prev = jnp.where(first, carry, pltpu.roll(x, 1, 0))
        return x + (prev - x) * mu

    rkv = rkv_ref[...]
    aux = aux_ref[...]
    xm = shift_mix(rkv, c_rkv[...], mu_rkv_ref[...])
    xa = shift_mix(aux, c_aux[...], mu_aux_ref[...])
    c_rkv[...] = rkv[cs - 1:cs, :]
    c_aux[...] = aux[cs - 1:cs, :]

    w0, a0, k_k, k_a = vec_ref[0:1, :], vec_ref[1:2, :], vec_ref[2:3, :], vec_ref[3:4, :]
    r_k, ln_w, ln_b = vec_ref[4:5, :], vec_ref[5:6, :], vec_ref[6:7, :]
    r = xm[:, :dr]
    k = xm[:, dr:2 * dr]
    v = xm[:, 2 * dr:]
    u = w0 + _mm(jnp.tanh(xa).astype(BF16), ww_ref[...])
    lw = -jnp.exp(-_softplus(-u) - 0.5)
    a = jax.nn.sigmoid(a0 + _mm(xa.astype(BF16), wa_ref[...]))
    gate = _mm(jax.nn.sigmoid(xa).astype(BF16), wg_ref[...])

    lane = lax.broadcasted_iota(jnp.int32, (1, LANES), 1)
    head0 = lane < RWKV_HD
    rr = lax.broadcasted_iota(jnp.int32, (LANES, LANES), 0)
    cc = lax.broadcasted_iota(jnp.int32, (LANES, LANES), 1)
    seg = jnp.where((rr // RWKV_HD) == (cc // RWKV_HD), 1.0, 0.0).astype(BF16)
    eye = jnp.where(rr == cc, 1.0, 0.0)

    def seg_sum(x):
        return jnp.concatenate(
            [_mm_lhs_exact(x[:, p * LANES:(p + 1) * LANES], seg) for p in range(n_pairs)], axis=1)

    kk = k * k_k
    kkn = kk * lax.rsqrt(seg_sum(kk * kk) + 1e-12)
    kh = k * (1.0 + (a - 1.0) * k_a)
    bb = kkn * a
    bonus = seg_sum(r * kh * r_k) * v
    if n_valid < cs:
        live = lax.broadcasted_iota(jnp.int32, (cs, 1), 0) < n_valid
        lw = jnp.where(live, lw, 0.0)
        kh = jnp.where(live, kh, 0.0)
        kkn = jnp.where(live, kkn, 0.0)
        bb = jnp.where(live, bb, 0.0)
        v = jnp.where(live, v, 0.0)

    t_r = lax.broadcasted_iota(jnp.int32, (cs, cs), 0)
    t_c = lax.broadcasted_iota(jnp.int32, (cs, cs), 1)
    tri = jnp.where(t_r >= t_c, 1.0, 0.0).astype(BF16)
    cum = _mm_rhs_exact(tri, lw)
    tot = cum[cs - 1:cs, :]
    e_inc = jnp.exp(cum)
    e_inv = jnp.exp(-cum)
    e_rem = jnp.exp(tot - cum)
    q_t = r * e_inc
    a_t = -kkn * jnp.exp(cum - lw)
    k_t = kh * e_inv
    b_t = bb * e_inv
    k_hat = kh * e_rem
    b_hat = bb * e_rem
    w_tot = jnp.exp(tot)

    s2 = 2 * cs
    sr = lax.broadcasted_iota(jnp.int32, (s2, s2), 0) % cs
    sc = lax.broadcasted_iota(jnp.int32, (s2, s2), 1) % cs
    strict = sr > sc
    incl = sr >= sc
    eye2 = jnp.where(lax.broadcasted_iota(jnp.int32, (s2, s2), 0)
                     == lax.broadcasted_iota(jnp.int32, (s2, s2), 1), 1.0, 0.0)

    def stack(x):
        return jnp.concatenate([jnp.where(head0, x, 0.0), jnp.where(head0, 0.0, x)], axis=0)

    n_dbl = int(math.log2(cs)) - 1
    ys = []
    for p in range(n_pairs):
        ls = slice(p * LANES, (p + 1) * LANES)
        a_s, q_s, k_s, b_s = stack(a_t[:, ls]), stack(q_t[:, ls]), stack(k_t[:, ls]), stack(b_t[:, ls])
        v_s, kh_s, bh_s = stack(v[:, ls]), stack(k_hat[:, ls]), stack(b_hat[:, ls])
        a_ab = jnp.where(strict, _mm3(a_s, b_s, NT), 0.0)
        a_ak = jnp.where(strict, _mm3(a_s, k_s, NT), 0.0)
        b_rb = jnp.where(incl, _mm3(q_s, b_s, NT), 0.0)
        b_rk = jnp.where(incl, _mm3(q_s, k_s, NT), 0.0)
        tinv = eye2 + a_ab
        apow = a_ab
        for _ in range(n_dbl):
            apow = _mm3(apow, apow)
            tinv = tinv + _mm3(apow, tinv)
        abar = _mm3(tinv, a_s)
        u0 = _mm3(tinv, _mm3(a_ak, v_s))
        rbar = q_s + _mm3(b_rb, abar)
        y0 = _mm3(b_rb, u0) + _mm3(b_rk, v_s)
        m_p = eye * w_tot[:, ls] + _mm3(abar, bh_s, TN)
        n_p = _mm3(u0, bh_s, TN) + _mm3(v_s, kh_s, TN)
        s_old = s_scr[p]
        y_s = _mm3(rbar, s_old, NT) + y0
        s_scr[p] = _mm3(s_old, m_p) + n_p
        ys.append(y_s[:cs] + y_s[cs:])
    y = jnp.concatenate(ys, axis=1)

    mean = seg_sum(y) * (1.0 / RWKV_HD)
    dy = y - mean
    var = seg_sum(dy * dy) * (1.0 / RWKV_HD)
    yn = dy * lax.rsqrt(var + GN_EPS) * ln_w + ln_b
    o_ref[...] = ((yn + bonus) * gate).astype(o_ref.dtype)

    @pl.when(ci == pl.num_programs(1) - 1)
    def _():
        sfin_ref[...] = s_scr[...]


def _rwkv(p, shift_rkv, shift_aux, s0_pairs, mu_rkv, mu_aux, vecs, ww, wa, wg, n_seq, t_len, cs, n_valid, out_dtype):
    n_chunks = t_len // cs
    n_pairs = s0_pairs.shape[1]
    dr = n_pairs * LANES
    full = lambda shape: pl.BlockSpec(shape, lambda b, c: (0,) * len(shape))
    return pl.pallas_call(
        functools.partial(_rwkv_kernel, cs=cs, n_valid=n_valid),
        grid=(n_seq, n_chunks),
        in_specs=[pl.BlockSpec((cs, 3 * dr), lambda b, c: (b * n_chunks + c, P_RKV // (3 * dr))),
                  pl.BlockSpec((cs, P_AUX_W), lambda b, c: (b * n_chunks + c, P_AUX // P_AUX_W)),
                  pl.BlockSpec((None, 1, 3 * dr), lambda b, c: (b, 0, 0)),
                  pl.BlockSpec((None, 1, P_AUX_W), lambda b, c: (b, 0, 0)),
                  pl.BlockSpec((None, n_pairs, LANES, LANES), lambda b, c: (b, 0, 0, 0)),
                  full(mu_rkv.shape), full(mu_aux.shape), full(vecs.shape),
                  full(ww.shape), full(wa.shape), full(wg.shape)],
        out_specs=[pl.BlockSpec((cs, dr), lambda b, c: (b * n_chunks + c, 0)),
                   pl.BlockSpec((None, n_pairs, LANES, LANES), lambda b, c: (b, 0, 0, 0))],
        out_shape=[jax.ShapeDtypeStruct((n_seq * t_len, dr), out_dtype),
                   jax.ShapeDtypeStruct((n_seq, n_pairs, LANES, LANES), F32)],
        scratch_shapes=[pltpu.VMEM((n_pairs, LANES, LANES), F32),
                        pltpu.VMEM((1, 3 * dr), F32), pltpu.VMEM((1, P_AUX_W), F32)],
        compiler_params=_cparams(("arbitrary", "arbitrary")),
    )(p, p, shift_rkv, shift_aux, s0_pairs, mu_rkv, mu_aux, vecs, ww, wa, wg)


def _out_kernel(x_ref, m_ref, on_ref, or_ref, wn_ref, wr_ref, o_ref):
    mix = _mm(on_ref[...], wn_ref[...]) + _mm(or_ref[...], wr_ref[...])
    o_ref[...] = x_ref[...] + m_ref[5] * mix


def _out_proj(x, mod, o_nsa, o_rw, w_top, w_bot, tm, tiles_per_seq):
    m_rows, d = x.shape
    dn = o_nsa.shape[1]
    mr = mod.shape[2]
    return pl.pallas_call(
        _out_kernel,
        grid=(m_rows // tm,),
        in_specs=[pl.BlockSpec((tm, d), lambda i: (i, 0)),
                  pl.BlockSpec((None, N_MOD, mr, d), lambda i: (i // tiles_per_seq, 0, 0, 0)),
                  pl.BlockSpec((tm, dn), lambda i: (i, 0)),
                  pl.BlockSpec((tm, dn), lambda i: (i, 0)),
                  pl.BlockSpec((dn, d), lambda i: (0, 0)),
                  pl.BlockSpec((dn, d), lambda i: (0, 0))],
        out_specs=pl.BlockSpec((tm, d), lambda i: (i, 0)),
        out_shape=jax.ShapeDtypeStruct((m_rows, d), F32),
        compiler_params=_cparams(("arbitrary",)),
    )(x, mod, o_nsa, o_rw, w_top, w_bot)


def _pad_cols(x, n):
    return jnp.pad(x, [(0, 0)] * (x.ndim - 1) + [(0, n)])


def _reorder_cols(w, d_rwkv, n_gate):
    nsa_main = P_AUX
    rw0 = nsa_main + n_gate
    lora0 = rw0 + 3 * d_rwkv
    n_lora = w.shape[-1] - lora0
    aux = jnp.concatenate([_pad_cols(w[..., nsa_main:rw0], AUX_WD - n_gate),
                           _pad_cols(w[..., lora0:], P_AUX_W - AUX_WD - n_lora)], axis=-1)
    return jnp.concatenate([w[..., :nsa_main], aux, w[..., rw0:lora0]], axis=-1)


def _pairs_from_heads(s):
    n, h = s.shape[:2]
    s = s.reshape(n, h // 2, 2, RWKV_HD, RWKV_HD)
    z = jnp.zeros_like(s[:, :, 0])
    top = jnp.concatenate([s[:, :, 0], z], axis=-1)
    bot = jnp.concatenate([z, s[:, :, 1]], axis=-1)
    return jnp.concatenate([top, bot], axis=-2)


def _heads_from_pairs(s):
    n, hp = s.shape[:2]
    a = s[:, :, :RWKV_HD, :RWKV_HD]
    b = s[:, :, RWKV_HD:, RWKV_HD:]
    return jnp.stack([a, b], axis=2).reshape(n, 2 * hp, RWKV_HD, RWKV_HD)


def kernel(x_prompt, x_sample, cache_nsa_kv, state_win_kv, state_wkv, state_shift, page_table,
           c_prompt, c_sample, w_ada, b_ada, norm_g, ffn_wi, ffn_wo, w_in, w_out,
           q_norm_g, k_norm_g, cmp_pe, cmp_w1, cmp_b1, cmp_w2,
           rwkv_mu, rwkv_w0, rwkv_w_w2, rwkv_a0, rwkv_w_a2, rwkv_w_g2,
           rwkv_k_k, rwkv_k_a, rwkv_r_k, rwkv_ln_w, rwkv_ln_b):
    bp, tp, d = x_prompt.shape
    bs, ts, _ = x_sample.shape
    depth = w_ada.shape[0]
    n_pool = cache_nsa_kv.shape[1]
    n_pages = page_table.shape[1]
    past = n_pages * PAGE
    wbuf = state_win_kv.shape[2]
    d_rwkv = rwkv_w0.shape[1]
    n_heads = d_rwkv // RWKV_HD
    d_nsa = N_QH * NSA_HD
    n_gate = 3 * N_QH
    n_dlora = rwkv_w_w2.shape[1]
    n_alora = rwkv_w_a2.shape[1]
    n_glora = rwkv_w_g2.shape[1]
    mp_rows, ms_rows = bp * tp, bs * ts
    tm = 512
    tiles_per_seq = tp // tm
    cs = 64
    ts_pad = -(-ts // 16) * 16
    win_keep = min(WINDOW, tp)

    c_all = jnp.concatenate([c_prompt, c_sample], axis=0)
    c_rows = -(-c_all.shape[0] // 8) * 8
    c_all = jnp.pad(c_all, ((0, c_rows - c_all.shape[0]), (0, 0)))
    mod = _modulation(c_all, w_ada, b_ada).reshape(depth, c_rows, N_MOD, d)
    mod_p = mod[:, :bp].reshape(depth, bp, N_MOD, 1, d)
    mod_s = jnp.repeat(mod[:, bp:bp + bs].transpose(0, 2, 1, 3), ts, axis=2).reshape(depth, 1, N_MOD, ms_rows, d)

    xp = x_prompt.reshape(mp_rows, d)
    xs = x_sample.reshape(ms_rows, d)
    pool = cache_nsa_kv.reshape(depth * n_pool, PAGE, 4 * N_KV * NSA_HD)
    prompt_pages = jnp.arange(bp * (tp // PAGE), dtype=jnp.int32).reshape(bp, tp // PAGE)
    n_sel_s = -(-(past + ts) // SEL_BLOCK)

    kv_p, kv_s, win_p, win_s, wkv_p, wkv_s, sh_p, sh_s = [], [], [], [], [], [], [], []
    for l in range(depth):
        wi = [ffn_wi[l, i].astype(BF16) for i in range(2)]
        wo = [ffn_wo[l, i].astype(BF16) for i in range(2)]
        w_in_l = _reorder_cols(w_in[l], d_rwkv, n_gate).astype(BF16)
        w_top = w_out[l, :d_nsa].astype(BF16)
        w_bot = w_out[l, d_nsa:].astype(BF16)
        g_rows = [norm_g[l, i].reshape(1, d) for i in range(3)]
        q_g = q_norm_g[l].reshape(1, NSA_HD)
        k_g = k_norm_g[l]
        w1 = cmp_w1[l].reshape(2, 2, CMP_STRIDE * NSA_HD, CMP_HID)
        w1cat = jnp.concatenate([w1[:, 0], w1[:, 1]], axis=-1).astype(BF16)
        pe2 = cmp_pe[l].reshape(2, 2, 1, CMP_STRIDE * NSA_HD)
        b1 = cmp_b1[l].reshape(2, 1, CMP_HID)
        w2 = cmp_w2[l].astype(BF16)
        def shift_parts(sh):
            aux = jnp.pad(sh[:, 3 * d_rwkv:], ((0, 0), (AUX_WD, P_AUX_W - AUX_WD - (sh.shape[1] - 3 * d_rwkv))))
            return sh[:, None, :3 * d_rwkv], aux[:, None, :]

        mu_rkv, mu_aux = (m[:, 0] for m in shift_parts(rwkv_mu[l].reshape(1, -1)))
        vecs = jnp.stack([rwkv_w0[l], rwkv_a0[l], rwkv_k_k[l], rwkv_k_a[l], rwkv_r_k[l].reshape(-1),
                          rwkv_ln_w[l], rwkv_ln_b[l], jnp.zeros_like(rwkv_w0[l])])
        lora_rows = lambda w, off: jnp.pad(w, ((off, P_AUX_W - off - w.shape[0]), (0, 0))).astype(BF16)
        ww = lora_rows(rwkv_w_w2[l], AUX_WD)
        wa = lora_rows(rwkv_w_a2[l], AUX_WD + n_dlora)
        wg = lora_rows(rwkv_w_g2[l], AUX_WD + n_dlora + n_alora)

        def shift_out(p_last):
            return jnp.concatenate([p_last[:, P_RKV:P_RKV + 3 * d_rwkv],
                                    p_last[:, P_AUX + AUX_WD:P_AUX + AUX_WD + n_dlora + n_alora + n_glora]],
                                   axis=1)

        xp = _ffn(xp, mod_p[l], g_rows[0], wi[0], wo[0], 0, tm, tiles_per_seq)
        xs = _ffn(xs, mod_s[l], g_rows[0], wi[0], wo[0], 0, ms_rows, 1)

        pp = _proj(xp, mod_p[l], g_rows[1], w_in_l, tm, tiles_per_seq)
        qn, rows, win, selbf, winbf, gates = _nsa_prep(pp, q_g, k_g, tm, BF16)
        kc, vc = _compress(rows.reshape(mp_rows // PAGE, PAGE, 1024), prompt_pages, w1cat, pe2, b1, w2, k_g)
        o_nsa = _nsa_prompt(qn, gates, selbf, winbf, kc, vc, bp, tp)
        zero_rkv, zero_aux = shift_parts(jnp.zeros((bp, state_shift.shape[2]), F32))
        o_rw, s_fin = _rwkv(pp, zero_rkv, zero_aux, jnp.zeros((bp, n_heads // 2, LANES, LANES), F32),
                            mu_rkv, mu_aux, vecs, ww, wa, wg, bp, tp, cs, cs, BF16)
        xp = _out_proj(xp, mod_p[l], o_nsa, o_rw, w_top, w_bot, tm, tiles_per_seq)
        kv_p.append(rows.reshape(bp, tp // PAGE, PAGE, 4, N_KV, NSA_HD))
        win_p.append(win.reshape(bp, tp, 2, N_KV, NSA_HD)[:, tp - win_keep:])
        wkv_p.append(_heads_from_pairs(s_fin))
        sh_p.append(shift_out(pp.reshape(bp, tp, P_COLS)[:, -1]))

        ps = _proj(xs, mod_s[l], g_rows[1], w_in_l, ms_rows, 1)
        qn_s, rows_s, win_new, _, _, gates_s = _nsa_prep(ps, q_g, k_g, ms_rows, F32)
        page_idx = page_table + l * n_pool
        kc_s, vc_s = _compress(pool, page_idx, w1cat, pe2, b1, w2, k_g)
        o_cmp, sel = _nsa_sample_select(qn_s, kc_s, vc_s, ts, past, n_sel_s)
        o_nsa_s = _nsa_sample_sweep(pool, page_idx, qn_s, gates_s, sel, o_cmp, rows_s, win_new,
                                    state_win_kv[l].reshape(bs, wbuf, 512), ts, past, n_sel_s)
        sh_rkv, sh_aux = shift_parts(state_shift[l])
        ps_pad = jnp.pad(ps.reshape(bs, ts, P_COLS), ((0, 0), (0, ts_pad - ts), (0, 0)))
        o_rw_s, s_fin_s = _rwkv(ps_pad.reshape(bs * ts_pad, P_COLS), sh_rkv, sh_aux,
                                _pairs_from_heads(state_wkv[l]), mu_rkv, mu_aux, vecs, ww, wa, wg,
                                bs, ts_pad, ts_pad, ts, F32)
        o_rw_s = o_rw_s.reshape(bs, ts_pad, d_rwkv)[:, :ts].reshape(ms_rows, d_rwkv)
        xs = _out_proj(xs, mod_s[l], o_nsa_s.reshape(ms_rows, d_nsa).astype(BF16), o_rw_s.astype(BF16),
                       w_top, w_bot, ms_rows, 1)
        kv_s.append(rows_s.reshape(bs, ts, 4, N_KV, NSA_HD))
        win_s.append(jnp.concatenate([state_win_kv[l][:, ts:],
                                      win_new.reshape(bs, ts, 2, N_KV, NSA_HD)], axis=1))
        wkv_s.append(_heads_from_pairs(s_fin_s))
        sh_s.append(shift_out(ps.reshape(bs, ts, P_COLS)[:, -1]))

        xp = _ffn(xp, mod_p[l], g_rows[2], wi[1], wo[1], 2, tm, tiles_per_seq)
        xs = _ffn(xs, mod_s[l], g_rows[2], wi[1], wo[1], 2, ms_rows, 1)

    return (xp.reshape(bp, tp, d), xs.reshape(bs, ts, d),
            jnp.stack(kv_p), jnp.stack(kv_s), jnp.stack(win_p), jnp.stack(win_s),
            jnp.stack(wkv_p), jnp.stack(wkv_s), jnp.stack(sh_p), jnp.stack(sh_s))
```

```python
import functools
import math

import jax
import jax.numpy as jnp
from jax import lax
from jax.experimental import pallas as pl
from jax.experimental.pallas import tpu as pltpu

F32 = jnp.float32
BF16 = jnp.bfloat16

NSA_HD = 128
N_KV = 2
QPG = 4
N_QH = N_KV * QPG
CMP_LEN = 32
CMP_STRIDE = 16
CMP_HID = 2 * NSA_HD
SEL_BLOCK = 64
SEL_TOPK = 16
N_LOCAL = 2
WINDOW = 512
PAGE = 128
RWKV_HD = 64
N_MOD = 9
NORM_EPS = 1e-6
GN_EPS = 64e-5
NEG = -1e30
FORCED = 1e6
INVALID = -1e6

LANES = 128
VMEM_LIMIT = 56 * 1024 * 1024
PAGES_PER_STEP = 8

P_Q = 0
P_KV = 1024
P_AUX = 2560
P_AUX_W = 512
P_RKV = 3072
P_COLS = 6144
AUX_WD = 128
AUX_AD = 192
AUX_GD = 256


def _cparams(sem):
    return pltpu.CompilerParams(dimension_semantics=sem, vmem_limit_bytes=VMEM_LIMIT)


def _mm(a, b, dims=((1,), (0,))):
    return lax.dot_general(a, b, (dims, ((), ())), preferred_element_type=F32)


NT = ((1,), (1,))
TN = ((0,), (0,))


def _split2(x):
    hi = x.astype(BF16)
    lo = (x - hi.astype(F32)).astype(BF16)
    return hi, lo


def _split3(x):
    hi = x.astype(BF16)
    r1 = x - hi.astype(F32)
    mid = r1.astype(BF16)
    lo = (r1 - mid.astype(F32)).astype(BF16)
    return hi, mid, lo


def _mm3(a, b, dims=((1,), (0,))):
    ah, al = _split2(a)
    bh, bl = _split2(b)
    return _mm(ah, bh, dims) + (_mm(ah, bl, dims) + _mm(al, bh, dims))


def _mmp(a, b, dims=((1,), (0,)), passes=1):
    if passes == 3:
        return _mm3(a, b, dims)
    return _mm(a.astype(BF16), b.astype(BF16), dims)


RW_PASSES = {"gram": 1, "inv": 1, "mix": 1, "trans": 1, "out": 1, "state": 3}


def _mm_lhs_exact(a, b_bf16, dims=((1,), (0,))):
    a1, a2, a3 = _split3(a)
    return _mm(a1, b_bf16, dims) + (_mm(a2, b_bf16, dims) + _mm(a3, b_bf16, dims))


def _mm_rhs_exact(a_bf16, b, dims=((1,), (0,))):
    b1, b2, b3 = _split3(b)
    return _mm(a_bf16, b1, dims) + (_mm(a_bf16, b2, dims) + _mm(a_bf16, b3, dims))


def _silu(x):
    return x * jax.nn.sigmoid(x)


def _rms(x, g):
    return x * lax.rsqrt(jnp.mean(x * x, axis=-1, keepdims=True) + NORM_EPS) * g


def _ada_norm(x, m_ref, slot, g):
    return _rms(x, g) * (1.0 + m_ref[3 * slot + 1]) + m_ref[3 * slot]


def _mod_kernel(c_ref, w_ref, b_ref, o_ref):
    s = _silu(c_ref[...]).astype(BF16)
    o_ref[0] = _mm(s, w_ref[0].astype(BF16)) + b_ref[0]


def _modulation(c_all, w_ada, b_ada):
    depth, d, n = w_ada.shape
    rows = c_all.shape[0]
    tn = 1024
    return pl.pallas_call(
        _mod_kernel,
        grid=(depth, n // tn),
        in_specs=[pl.BlockSpec((rows, d), lambda l, j: (0, 0)),
                  pl.BlockSpec((1, d, tn), lambda l, j: (l, 0, j)),
                  pl.BlockSpec((1, 1, tn), lambda l, j: (l, 0, j))],
        out_specs=pl.BlockSpec((1, rows, tn), lambda l, j: (l, 0, j)),
        out_shape=jax.ShapeDtypeStruct((depth, rows, n), F32),
        compiler_params=_cparams(("arbitrary", "arbitrary")),
    )(c_all, w_ada, b_ada.reshape(depth, 1, n))


def _ffn_kernel(x_ref, m_ref, g_ref, wg_ref, wu_ref, wo_ref, o_ref, h_scr, acc_scr, *, slot):
    f = pl.program_id(1)

    @pl.when(f == 0)
    def _():
        h_scr[...] = _ada_norm(x_ref[...], m_ref, slot, g_ref[...]).astype(BF16)
        acc_scr[...] = jnp.zeros_like(acc_scr)

    h = h_scr[...]
    gate = _mm(h, wg_ref[...])
    up = _mm(h, wu_ref[...])
    act = (_silu(gate) * up).astype(BF16)
    acc_scr[...] += _mm(act, wo_ref[...])

    @pl.when(f == pl.num_programs(1) - 1)
    def _():
        o_ref[...] = x_ref[...] + 0.5 * m_ref[3 * slot + 2] * acc_scr[...]


def _ffn(x, mod, g, wi, wo, slot, tm, tiles_per_seq):
    m_rows, d = x.shape
    d_ff = wo.shape[0]
    tf = 512
    nf = d_ff // tf
    mr = mod.shape[2]
    return pl.pallas_call(
        functools.partial(_ffn_kernel, slot=slot),
        grid=(m_rows // tm, nf),
        in_specs=[pl.BlockSpec((tm, d), lambda i, f: (i, 0)),
                  pl.BlockSpec((None, N_MOD, mr, d), lambda i, f: (i // tiles_per_seq, 0, 0, 0)),
                  pl.BlockSpec((1, d), lambda i, f: (0, 0)),
                  pl.BlockSpec((d, tf), lambda i, f: (0, f)),
                  pl.BlockSpec((d, tf), lambda i, f: (0, nf + f)),
                  pl.BlockSpec((tf, d), lambda i, f: (f, 0))],
        out_specs=pl.BlockSpec((tm, d), lambda i, f: (i, 0)),
        out_shape=jax.ShapeDtypeStruct((m_rows, d), F32),
        scratch_shapes=[pltpu.VMEM((tm, d), BF16), pltpu.VMEM((tm, d), F32)],
        compiler_params=_cparams(("arbitrary", "arbitrary")),
    )(x, mod, g, wi, wi, wo)


def _proj_kernel(x_ref, m_ref, g_ref, w_ref, o_ref, h_scr):
    @pl.when(pl.program_id(1) == 0)
    def _():
        h_scr[...] = _ada_norm(x_ref[...], m_ref, 1, g_ref[...]).astype(BF16)

    o_ref[...] = _mm(h_scr[...], w_ref[...])


def _proj(x, mod, g, w, tm, tiles_per_seq):
    m_rows, d = x.shape
    n = w.shape[1]
    tn = 1536
    mr = mod.shape[2]
    return pl.pallas_call(
        _proj_kernel,
        grid=(m_rows // tm, n // tn),
        in_specs=[pl.BlockSpec((tm, d), lambda i, j: (i, 0)),
                  pl.BlockSpec((None, N_MOD, mr, d), lambda i, j: (i // tiles_per_seq, 0, 0, 0)),
                  pl.BlockSpec((1, d), lambda i, j: (0, 0)),
                  pl.BlockSpec((d, tn), lambda i, j: (0, j))],
        out_specs=pl.BlockSpec((tm, tn), lambda i, j: (i, j)),
        out_shape=jax.ShapeDtypeStruct((m_rows, n), F32),
        scratch_shapes=[pltpu.VMEM((tm, d), BF16)],
        compiler_params=_cparams(("arbitrary", "arbitrary")),
    )(x, mod, g, w)


def _rms_heads(x, g):
    outs = []
    for h in range(x.shape[1] // NSA_HD):
        outs.append(_rms(x[:, h * NSA_HD:(h + 1) * NSA_HD], g))
    return jnp.concatenate(outs, axis=1)


def _nsa_prep_kernel(q_ref, kva_ref, kvb_ref, kvc_ref, gt_ref, qg_ref, kg_ref,
                     qn_ref, rows_ref, win_ref, selbf_ref, winbf_ref, gates_ref):
    qn_ref[...] = _rms_heads(q_ref[...], qg_ref[...]).astype(qn_ref.dtype)
    kvb = kvb_ref[...]
    ksel = _rms_heads(kvb[:, :2 * NSA_HD], kg_ref[1:2, :])
    selrows = jnp.concatenate([ksel, kvb[:, 2 * NSA_HD:]], axis=1)
    rows = jnp.concatenate([kva_ref[...], selrows], axis=1)
    tm = rows.shape[0]
    for c in range(4 * N_KV):
        rows_ref[pl.ds(c, tm, stride=4 * N_KV), :] = rows[:, c * NSA_HD:(c + 1) * NSA_HD]
    selbf_ref[...] = selrows.astype(selbf_ref.dtype)
    kvc = kvc_ref[...]
    kwin = _rms_heads(kvc[:, :2 * NSA_HD], kg_ref[2:3, :])
    winrows = jnp.concatenate([kwin, kvc[:, 2 * NSA_HD:]], axis=1)
    for c in range(2 * N_KV):
        win_ref[pl.ds(c, tm, stride=2 * N_KV), :] = winrows[:, c * NSA_HD:(c + 1) * NSA_HD]
    winbf_ref[...] = winrows.astype(winbf_ref.dtype)
    gates_ref[...] = jax.nn.sigmoid(gt_ref[...])


def _nsa_prep(p, q_g, k_g, tm, act_dtype):
    m_rows = p.shape[0]
    row = lambda w, j: pl.BlockSpec((tm, w), lambda i: (i, j))
    return pl.pallas_call(
        _nsa_prep_kernel,
        grid=(m_rows // tm,),
        in_specs=[row(1024, 0), row(512, 2), row(512, 3), row(512, 4), row(LANES, P_AUX // LANES),
                  pl.BlockSpec((1, NSA_HD), lambda i: (0, 0)),
                  pl.BlockSpec((3, NSA_HD), lambda i: (0, 0))],
        out_specs=[row(1024, 0), pl.BlockSpec((tm * 8, NSA_HD), lambda i: (i, 0)),
                   pl.BlockSpec((tm * 4, NSA_HD), lambda i: (i, 0)),
                   row(512, 0), row(512, 0), row(LANES, 0)],
        out_shape=[jax.ShapeDtypeStruct((m_rows, 1024), act_dtype),
                   jax.ShapeDtypeStruct((m_rows * 8, NSA_HD), F32),
                   jax.ShapeDtypeStruct((m_rows * 4, NSA_HD), F32),
                   jax.ShapeDtypeStruct((m_rows, 512), act_dtype),
                   jax.ShapeDtypeStruct((m_rows, 512), act_dtype),
                   jax.ShapeDtypeStruct((m_rows, LANES), F32)],
        compiler_params=_cparams(("arbitrary",)),
    )(p, p, p, p, p, q_g, k_g)


def _cmp_kernel(pt_ref, *refs, n_sub, pp):
    pages = refs[:pp]
    w1_ref, pe_ref, b1_ref, w2_ref, kg_ref, kc_ref, vc_ref, x_scr = refs[pp:]
    p = pl.program_id(1)
    spp = PAGE // CMP_STRIDE
    base = pl.multiple_of(p * (pp * spp), pp * spp)
    heads = 4 * N_KV
    for c in range(2 * N_KV):
        for l in range(CMP_STRIDE):
            pieces = [pg[pl.ds(l * heads + c, spp, stride=CMP_STRIDE * heads), :] for pg in pages]
            x_scr[c, pl.ds(base, pp * spp), l * NSA_HD:(l + 1) * NSA_HD] = (
                jnp.concatenate(pieces, axis=0).astype(BF16))

    @pl.when(p == pl.num_programs(1) - 1)
    def _():
        row = lax.broadcasted_iota(jnp.int32, (n_sub, NSA_HD), 0)
        for kv in range(2):
            w1 = w1_ref[kv]
            const = b1_ref[kv]
            for j in range(2):
                pe = jnp.broadcast_to(pe_ref[kv, j], (8, CMP_STRIDE * NSA_HD)).astype(BF16)
                const = const + _mm(pe, w1[:, j * CMP_HID:(j + 1) * CMP_HID])[0:1]
            for g in range(N_KV):
                ab = _mm(x_scr[kv * 2 + g], w1)
                nxt = pltpu.roll(ab[:, CMP_HID:], n_sub - 1, 0)
                acc = ab[:, :CMP_HID] + nxt + const
                o = _mm(jax.nn.gelu(acc).astype(BF16), w2_ref[kv])
                if kv == 0:
                    o = _rms(o, kg_ref[0:1, :])
                o = jnp.where(row < n_sub - 1, o, 0.0)
                if kv == 0:
                    kc_ref[g] = o
                else:
                    vc_ref[g] = o


def _compress(pool, page_idx, w1cat, pe2, b1, w2, k_g):
    nb, n_pages = page_idx.shape
    n_sub = n_pages * (PAGE // CMP_STRIDE)
    pp = PAGES_PER_STEP
    page = lambda k: pl.BlockSpec((PAGE * 4 * N_KV, NSA_HD), lambda b, p, pt: (pt[b, pp * p + k], 0))
    page_specs = [page(k) for k in range(pp)]
    full = lambda shape: pl.BlockSpec(shape, lambda b, p, pt: (0,) * len(shape))
    out = pl.BlockSpec((None, N_KV, n_sub, NSA_HD), lambda b, p, pt: (b, 0, 0, 0))
    return pl.pallas_call(
        functools.partial(_cmp_kernel, n_sub=n_sub, pp=pp),
        grid_spec=pltpu.PrefetchScalarGridSpec(
            num_scalar_prefetch=1,
            grid=(nb, n_pages // pp),
            in_specs=page_specs + [full(w1cat.shape), full(pe2.shape), full(b1.shape),
                      full(w2.shape), full(k_g.shape)],
            out_specs=[out, out],
            scratch_shapes=[pltpu.VMEM((4, n_sub, CMP_STRIDE * NSA_HD), BF16)]),
        out_shape=[jax.ShapeDtypeStruct((nb, N_KV, n_sub, NSA_HD), F32)] * 2,
        compiler_params=_cparams(("arbitrary", "arbitrary")),
    )(page_idx, *([pool] * pp), w1cat, pe2, b1, w2, k_g)


def _imp_matrix(n_sub, n_cmp, width):
    n = lax.broadcasted_iota(jnp.int32, (n_sub, width), 0)
    j = lax.broadcasted_iota(jnp.int32, (n_sub, width), 1)
    spb = SEL_BLOCK // CMP_STRIDE
    m = jnp.where(n // spb == j, 1.0, 0.0) + jnp.where((n + 1) // spb == j, 1.0, 0.0)
    return jnp.where(n < n_cmp, m, 0.0).astype(BF16)


def _topk_mask_t(st_scr, n_iter, width):
    jj = st_scr.shape[0]
    st = st_scr[...]
    jrow = lax.broadcasted_iota(jnp.int32, (jj, width), 0)

    def body(jp, rank):
        r = st_scr[pl.ds(jp, 1), :]
        beats = (r > st) | ((r == st) & (jp < jrow))
        return rank + jnp.where(beats, 1.0, 0.0)

    return lax.fori_loop(0, n_iter, body, jnp.zeros((jj, width), F32))


def _softmax_rows(s, ok):
    sm = jnp.where(ok, s, NEG)
    mx = jnp.max(sm, axis=-1, keepdims=True)
    e = jnp.where(ok, jnp.exp(sm - mx), 0.0)
    den = jnp.sum(e, axis=-1, keepdims=True)
    return e / jnp.where(den > 0.0, den, 1.0)


def _nsa_prompt_kernel(q_ref, gt_ref, sel_ref, win_ref, kc_ref, vc_ref, o_ref, st_scr,
                       *, t_len, n_sub, ck):
    tq = SEL_BLOCK
    qi = pl.program_id(1)
    s0 = qi * tq
    scale = NSA_HD ** -0.5
    n_cmp = n_sub - 1
    rows = QPG * tq
    q = q_ref[...]
    rowpos = s0 + lax.broadcasted_iota(jnp.int32, (rows, 1), 0) % tq

    def q_group(g):
        return jnp.concatenate(
            [q[:, (g * QPG + h) * NSA_HD:(g * QPG + h + 1) * NSA_HD] for h in range(QPG)], axis=0)

    imp_m = _imp_matrix(n_sub, n_cmp, LANES)
    n_idx = lax.broadcasted_iota(jnp.int32, (rows, n_sub), 1)
    ok_c = (n_idx * CMP_STRIDE + (CMP_LEN - 1) <= rowpos) & (n_idx < n_cmp)
    o_cmp, scores = [], []
    jl = lax.broadcasted_iota(jnp.int32, (tq, LANES), 1)
    forced = (jl == 0) | ((jl <= qi) & (jl > qi - N_LOCAL))
    for g in range(N_KV):
        qg = q_group(g)
        s_c = _mm(qg, kc_ref[g].astype(BF16), NT) * scale
        p_c = _softmax_rows(s_c, ok_c)
        o_cmp.append(_mm(p_c.astype(BF16), vc_ref[g].astype(BF16)))
        p_g = p_c[0:tq] + p_c[tq:2 * tq] + p_c[2 * tq:3 * tq] + p_c[3 * tq:4 * tq]
        imp = _mm_lhs_exact(p_g, imp_m)
        scores.append(jnp.where(jl <= qi, jnp.where(forced, FORCED, imp), INVALID))
    st_scr[...] = jnp.concatenate(scores, axis=0).T
    rank = _topk_mask_t(st_scr, qi + 1, LANES)
    jrow = lax.broadcasted_iota(jnp.int32, (LANES, N_KV * tq), 0)
    sel_t = jnp.where((rank < SEL_TOPK) & (jrow <= qi), 1.0, 0.0)
    sel = sel_t.T.astype(BF16)

    n_chunks = (s0 + tq + ck - 1) // ck
    qgs = [q_group(g) for g in range(N_KV)]

    def chunk(c, carry):
        k0 = pl.multiple_of(c * ck, ck)
        keypos = k0 + lax.broadcasted_iota(jnp.int32, (1, ck), 1)
        ej = lax.broadcasted_iota(jnp.int32, (LANES, ck), 0)
        expand = jnp.where(ej == keypos // SEL_BLOCK, 1.0, 0.0).astype(BF16)
        mask2 = _mm(sel, expand)
        causal = keypos <= rowpos
        new = []
        for g in range(N_KV):
            m_i, l_i, acc = carry[g]
            kk = sel_ref[pl.ds(k0, ck), g * NSA_HD:(g + 1) * NSA_HD]
            vv = sel_ref[pl.ds(k0, ck), (N_KV + g) * NSA_HD:(N_KV + g + 1) * NSA_HD]
            mg = mask2[g * tq:(g + 1) * tq]
            ok = (jnp.concatenate([mg] * QPG, axis=0) > 0.5) & causal
            s = jnp.where(ok, _mm(qgs[g], kk, NT) * scale, NEG)
            m_new = jnp.maximum(m_i, jnp.max(s, axis=-1, keepdims=True))
            pr = jnp.where(ok, jnp.exp(s - m_new), 0.0)
            alpha = jnp.exp(m_i - m_new)
            l_new = alpha * l_i + jnp.sum(pr, axis=-1, keepdims=True)
            acc_new = alpha * acc + _mm(pr.astype(BF16), vv)
            new.append((m_new, l_new, acc_new))
        return tuple(new)

    init = tuple((jnp.full((rows, 1), NEG, F32), jnp.zeros((rows, 1), F32),
                  jnp.zeros((rows, NSA_HD), F32)) for _ in range(N_KV))
    fin = lax.fori_loop(0, n_chunks, chunk, init)

    wlen = WINDOW + 2 * tq
    w0 = pl.multiple_of(jnp.clip(s0 - WINDOW, 0, t_len - wlen), tq)
    wpos = w0 + lax.broadcasted_iota(jnp.int32, (1, wlen), 1)
    dist = rowpos - wpos
    ok_w = (dist >= 0) & (dist < WINDOW)
    gt = gt_ref[...]
    outs = []
    for g in range(N_KV):
        kw = win_ref[pl.ds(w0, wlen), g * NSA_HD:(g + 1) * NSA_HD]
        vw = win_ref[pl.ds(w0, wlen), (N_KV + g) * NSA_HD:(N_KV + g + 1) * NSA_HD]
        p_w = _softmax_rows(_mm(qgs[g], kw, NT) * scale, ok_w)
        o_w = _mm(p_w.astype(BF16), vw)
        _, l_i, acc = fin[g]
        o_s = acc / l_i
        for h in range(QPG):
            c0 = (g * QPG + h) * 3
            r0 = slice(h * tq, (h + 1) * tq)
            outs.append(gt[:, c0:c0 + 1] * o_cmp[g][r0] + gt[:, c0 + 1:c0 + 2] * o_s[r0]
                        + gt[:, c0 + 2:c0 + 3] * o_w[r0])
    o_ref[...] = jnp.concatenate(outs, axis=1).astype(o_ref.dtype)


def _nsa_prompt(qn, gates, selbf, winbf, kc, vc, nb, t_len):
    n_sub = kc.shape[2]
    tq = SEL_BLOCK
    nq = t_len // tq
    ck = 512
    per_b = lambda w: pl.BlockSpec((None, t_len, w), lambda b, i: (b, 0, 0))
    cmp_spec = pl.BlockSpec((None, N_KV, n_sub, NSA_HD), lambda b, i: (b, 0, 0, 0))
    return pl.pallas_call(
        functools.partial(_nsa_prompt_kernel, t_len=t_len, n_sub=n_sub, ck=ck),
        grid=(nb, nq),
        in_specs=[pl.BlockSpec((tq, 1024), lambda b, i: (b * nq + i, 0)),
                  pl.BlockSpec((tq, LANES), lambda b, i: (b * nq + i, 0)),
                  per_b(512), per_b(512), cmp_spec, cmp_spec],
        out_specs=pl.BlockSpec((tq, 1024), lambda b, i: (b * nq + i, 0)),
        out_shape=jax.ShapeDtypeStruct((nb * t_len, 1024), BF16),
        scratch_shapes=[pltpu.VMEM((LANES, N_KV * tq), F32)],
        compiler_params=_cparams(("arbitrary", "arbitrary")),
    )(qn, gates, selbf.reshape(nb, t_len, 512), winbf.reshape(nb, t_len, 512), kc, vc)


def _nsa_sample_select_kernel(q_ref, kc_ref, vc_ref, oc_ref, sel_ref, sc_scr, st_scr,
                              *, n_sub, n_sel, ts, past, jw):
    b = pl.program_id(0)
    nb = pl.num_programs(0)
    scale = NSA_HD ** -0.5
    n_cmp = n_sub - 1
    q = q_ref[...]
    imp_m = _imp_matrix(n_sub, n_cmp, jw)
    rows = QPG * ts
    rowpos = past + lax.broadcasted_iota(jnp.int32, (rows, 1), 0) % ts
    n_idx = lax.broadcasted_iota(jnp.int32, (rows, n_sub), 1)
    ok_c = (n_idx * CMP_STRIDE + (CMP_LEN - 1) <= rowpos) & (n_idx < n_cmp)
    jl = lax.broadcasted_iota(jnp.int32, (ts, jw), 1)
    blk = (past + lax.broadcasted_iota(jnp.int32, (ts, 1), 0)) // SEL_BLOCK
    forced = (jl == 0) | ((jl <= blk) & (jl > blk - N_LOCAL))

    @pl.when(b == 0)
    def _():
        sc_scr[...] = jnp.full(sc_scr.shape, INVALID, F32)

    for g in range(N_KV):
        qg = jnp.concatenate(
            [q[:, (g * QPG + h) * NSA_HD:(g * QPG + h + 1) * NSA_HD] for h in range(QPG)],
            axis=0).astype(BF16)
        s_c = _mm(qg, kc_ref[g].astype(BF16), NT) * scale
        p_c = _softmax_rows(s_c, ok_c)
        oc_ref[g * rows:(g + 1) * rows, :] = _mm(p_c.astype(BF16), vc_ref[g].astype(BF16))
        p_g = p_c[0:ts] + p_c[ts:2 * ts] + p_c[2 * ts:3 * ts] + p_c[3 * ts:4 * ts]
        imp = _mm_lhs_exact(p_g, imp_m)
        score = jnp.where((jl <= blk) & (jl < n_sel), jnp.where(forced, FORCED, imp), INVALID)
        sc_scr[pl.ds(pl.multiple_of((b * N_KV + g) * ts, ts), ts), :] = score

    @pl.when(b == nb - 1)
    def _():
        st_scr[...] = sc_scr[...].T
        rank = _topk_mask_t(st_scr, n_sel, LANES)
        sel_t = jnp.where((rank < SEL_TOPK) & (st_scr[...] > 0.5 * INVALID), 1.0, 0.0)
        sel_ref[...] = sel_t.T


def _nsa_sample_select(qn_s, kc, vc, ts, past, n_sel):
    nb = kc.shape[0]
    n_sub = kc.shape[2]
    jw = -(-n_sel // LANES) * LANES
    assert nb * N_KV * ts <= LANES
    cmp_spec = pl.BlockSpec((None, N_KV, n_sub, NSA_HD), lambda b: (b, 0, 0, 0))
    return pl.pallas_call(
        functools.partial(_nsa_sample_select_kernel, n_sub=n_sub, n_sel=n_sel, ts=ts, past=past, jw=jw),
        grid=(nb,),
        in_specs=[pl.BlockSpec((None, ts, 1024), lambda b: (b, 0, 0)), cmp_spec, cmp_spec],
        out_specs=[pl.BlockSpec((None, N_QH * ts, NSA_HD), lambda b: (b, 0, 0)),
                   pl.BlockSpec((LANES, jw), lambda b: (0, 0))],
        out_shape=[jax.ShapeDtypeStruct((nb, N_QH * ts, NSA_HD), F32),
                   jax.ShapeDtypeStruct((LANES, jw), F32)],
        scratch_shapes=[pltpu.VMEM((LANES, jw), F32), pltpu.VMEM((jw, LANES), F32)],
        compiler_params=_cparams(("arbitrary",)),
    )(qn_s.reshape(nb, ts, 1024), kc, vc)


def _nsa_sample_sweep_kernel(pt_ref, *refs, pp, ts, past, n_sel, wbuf):
    pages = refs[:pp]
    (q_ref, gt_ref, sel_ref, oc_ref, rows_ref, wnew_ref, wst_ref, o_ref,
     m_scr, l_scr, acc_scr) = refs[pp:]
    step = pl.program_id(1)
    scale = NSA_HD ** -0.5
    rows = QPG * ts
    heads = 4 * N_KV
    jw = sel_ref.shape[1]
    q = q_ref[...]
    qgs = [jnp.concatenate(
        [q[:, (g * QPG + h) * NSA_HD:(g * QPG + h + 1) * NSA_HD] for h in range(QPG)],
        axis=0).astype(BF16) for g in range(N_KV)]
    sel = sel_ref[...].astype(BF16)

    @pl.when(step == 0)
    def _():
        m_scr[...] = jnp.full(m_scr.shape, NEG, F32)
        l_scr[...] = jnp.zeros_like(l_scr)
        acc_scr[...] = jnp.zeros_like(acc_scr)

    def update(g, s, ok, vv):
        r0 = slice(g * rows, (g + 1) * rows)
        s = jnp.where(ok, s, NEG)
        m_i = m_scr[r0]
        m_new = jnp.maximum(m_i, jnp.max(s, axis=-1, keepdims=True))
        pr = jnp.where(ok, jnp.exp(s - m_new), 0.0)
        alpha = jnp.exp(m_i - m_new)
        l_scr[r0] = alpha * l_scr[r0] + jnp.sum(pr, axis=-1, keepdims=True)
        acc_scr[r0] = alpha * acc_scr[r0] + _mm(pr.astype(BF16), vv)
        m_scr[r0] = m_new

    nk = pp * PAGE
    ej = lax.broadcasted_iota(jnp.int32, (jw, nk), 0)
    kl = lax.broadcasted_iota(jnp.int32, (1, nk), 1)
    expand = jnp.where(ej == (step * nk + kl) // SEL_BLOCK, 1.0, 0.0).astype(BF16)
    mask2 = _mm(sel, expand)
    for g in range(N_KV):
        kk = jnp.concatenate([pg[pl.ds(2 * N_KV + g, PAGE, stride=heads), :] for pg in pages],
                             axis=0).astype(BF16)
        vv = jnp.concatenate([pg[pl.ds(3 * N_KV + g, PAGE, stride=heads), :] for pg in pages],
                             axis=0).astype(BF16)
        mg = mask2[g * ts:(g + 1) * ts]
        ok = jnp.concatenate([mg] * QPG, axis=0) > 0.5
        update(g, _mm(qgs[g], kk, NT) * scale, ok, vv)

    @pl.when(step == pl.num_programs(1) - 1)
    def _():
        tpos = lax.broadcasted_iota(jnp.int32, (rows, 1), 0) % ts
        pad = jnp.zeros((PAGE - ts, NSA_HD), F32)
        il = lax.broadcasted_iota(jnp.int32, (1, PAGE), 1)
        last_sel = sel_ref[:, n_sel - 1:n_sel]
        gt = gt_ref[...]
        wlen = wbuf + PAGE
        wl = lax.broadcasted_iota(jnp.int32, (1, wlen), 1)
        dist = (past + tpos) - (past - wbuf + wl)
        ok_w = (dist >= 0) & (dist < WINDOW) & (wl < wbuf + ts)
        outs = []
        for g in range(N_KV):
            kn = jnp.concatenate([rows_ref[pl.ds(2 * N_KV + g, ts, stride=heads), :], pad], axis=0)
            vn = jnp.concatenate([rows_ref[pl.ds(3 * N_KV + g, ts, stride=heads), :], pad], axis=0)
            lsel = jnp.concatenate([last_sel[g * ts:(g + 1) * ts]] * QPG, axis=0) > 0.5
            ok = (il <= tpos) & (il < ts) & lsel
            update(g, _mm(qgs[g], kn.astype(BF16), NT) * scale, ok, vn.astype(BF16))
            r0 = slice(g * rows, (g + 1) * rows)
            o_s = acc_scr[r0] / l_scr[r0]
            kw = jnp.concatenate([wst_ref[pl.ds(g, wbuf, stride=2 * N_KV), :],
                                  wnew_ref[pl.ds(g, ts, stride=2 * N_KV), :], pad], axis=0).astype(BF16)
            vw = jnp.concatenate([wst_ref[pl.ds(N_KV + g, wbuf, stride=2 * N_KV), :],
                                  wnew_ref[pl.ds(N_KV + g, ts, stride=2 * N_KV), :], pad],
                                 axis=0).astype(BF16)
            p_w = _softmax_rows(_mm(qgs[g], kw, NT) * scale, ok_w)
            o_w = _mm(p_w.astype(BF16), vw)
            o_c = oc_ref[r0, :]
            for h in range(QPG):
                c0 = (g * QPG + h) * 3
                rh = slice(h * ts, (h + 1) * ts)
                outs.append(gt[:, c0:c0 + 1] * o_c[rh] + gt[:, c0 + 1:c0 + 2] * o_s[rh]
                            + gt[:, c0 + 2:c0 + 3] * o_w[rh])
        o_ref[...] = jnp.concatenate(outs, axis=1)


def _nsa_sample_sweep(pool, page_idx, qn_s, gates_s, sel, o_cmp, rows_s, win_s, win_state,
                      ts, past, n_sel):
    nb, n_pages = page_idx.shape
    pp = PAGES_PER_STEP
    wbuf = win_state.shape[1] // (2 * N_KV)
    jw = sel.shape[1]
    page = lambda k: pl.BlockSpec((PAGE * 4 * N_KV, NSA_HD), lambda b, s, pt: (pt[b, pp * s + k], 0))
    per_b = lambda r, w: pl.BlockSpec((None, r, w), lambda b, s, pt: (b, 0, 0))
    return pl.pallas_call(
        functools.partial(_nsa_sample_sweep_kernel, pp=pp, ts=ts, past=past, n_sel=n_sel, wbuf=wbuf),
        grid_spec=pltpu.PrefetchScalarGridSpec(
            num_scalar_prefetch=1,
            grid=(nb, n_pages // pp),
            in_specs=[page(k) for k in range(pp)] + [
                per_b(ts, 1024), per_b(ts, LANES),
                pl.BlockSpec((N_KV * ts, jw), lambda b, s, pt: (b, 0)),
                per_b(N_QH * ts, NSA_HD), per_b(ts * 4 * N_KV, NSA_HD), per_b(ts * 2 * N_KV, NSA_HD),
                per_b(wbuf * 2 * N_KV, NSA_HD)],
            out_specs=per_b(ts, 1024),
            scratch_shapes=[pltpu.VMEM((N_QH * ts, 1), F32), pltpu.VMEM((N_QH * ts, 1), F32),
                            pltpu.VMEM((N_QH * ts, NSA_HD), F32)]),
        out_shape=jax.ShapeDtypeStruct((nb, ts, 1024), F32),
        compiler_params=_cparams(("arbitrary", "arbitrary")),
    )(page_idx, *([pool] * pp), qn_s.reshape(nb, ts, 1024), gates_s.reshape(nb, ts, LANES), sel,
      o_cmp, rows_s.reshape(nb, ts * 4 * N_KV, NSA_HD), win_s.reshape(nb, ts * 2 * N_KV, NSA_HD),
      win_state)


def _softplus(z):
    return jnp.maximum(z, 0.0) + jnp.log1p(jnp.exp(-jnp.abs(z)))


def _rwkv_kernel(rkv_ref, aux_ref, sh_rkv_ref, sh_aux_ref, s0_ref, mu_rkv_ref, mu_aux_ref,
                 vec_ref, ww_ref, wa_ref, wg_ref, o_ref, sfin_ref, s_scr, c_rkv, c_aux, *, cs, n_valid):
    ci = pl.program_id(1)
    n_pairs = s_scr.shape[0]
    dr = n_pairs * LANES

    @pl.when(ci == 0)
    def _():
        s_scr[...] = s0_ref[...]
        c_rkv[...] = sh_rkv_ref[...]
        c_aux[...] = sh_aux_ref[...]

    def shift_mix(x, carry, mu):
        first = lax.broadcasted_iota(jnp.int32, x.shape, 0) == 0
        prev = jnp.where(first, carry, pltpu.roll(x, 1, 0))
        return x + (prev - x) * mu

    rkv = rkv_ref[...]
    aux = aux_ref[...]
    xm = shift_mix(rkv, c_rkv[...], mu_rkv_ref[...])
    xa = shift_mix(aux, c_aux[...], mu_aux_ref[...])
    c_rkv[...] = rkv[cs - 1:cs, :]
    c_aux[...] = aux[cs - 1:cs, :]

    w0, a0, k_k, k_a = vec_ref[0:1, :], vec_ref[1:2, :], vec_ref[2:3, :], vec_ref[3:4, :]
    r_k, ln_w, ln_b = vec_ref[4:5, :], vec_ref[5:6, :], vec_ref[6:7, :]
    r = xm[:, :dr]
    k = xm[:, dr:2 * dr]
    v = xm[:, 2 * dr:]
    u = w0 + _mm(jnp.tanh(xa).astype(BF16), ww_ref[...])
    lw = -jnp.exp(-_softplus(-u) - 0.5)
    a = jax.nn.sigmoid(a0 + _mm(xa.astype(BF16), wa_ref[...]))
    gate = _mm(jax.nn.sigmoid(xa).astype(BF16), wg_ref[...])

    lane = lax.broadcasted_iota(jnp.int32, (1, LANES), 1)
    head0 = lane < RWKV_HD
    rr = lax.broadcasted_iota(jnp.int32, (LANES, LANES), 0)
    cc = lax.broadcasted_iota(jnp.int32, (LANES, LANES), 1)
    seg = jnp.where((rr // RWKV_HD) == (cc // RWKV_HD), 1.0, 0.0).astype(BF16)
    eye = jnp.where(rr == cc, 1.0, 0.0)

    def seg_sum(x):
        return jnp.concatenate(
            [_mm_lhs_exact(x[:, p * LANES:(p + 1) * LANES], seg) for p in range(n_pairs)], axis=1)

    kk = k * k_k
    kkn = kk * lax.rsqrt(seg_sum(kk * kk) + 1e-12)
    kh = k * (1.0 + (a - 1.0) * k_a)
    bb = kkn * a
    bonus = seg_sum(r * kh * r_k) * v
    if n_valid < cs:
        live = lax.broadcasted_iota(jnp.int32, (cs, 1), 0) < n_valid
        lw = jnp.where(live, lw, 0.0)
        kh = jnp.where(live, kh, 0.0)
        kkn = jnp.where(live, kkn, 0.0)
        bb = jnp.where(live, bb, 0.0)
        v = jnp.where(live, v, 0.0)

    t_r = lax.broadcasted_iota(jnp.int32, (cs, cs), 0)
    t_c = lax.broadcasted_iota(jnp.int32, (cs, cs), 1)
    tri = jnp.where(t_r >= t_c, 1.0, 0.0).astype(BF16)
    cum = _mm_rhs_exact(tri, lw)
    tot = cum[cs - 1:cs, :]
    e_inc = jnp.exp(cum)
    e_inv = jnp.exp(-cum)
    e_rem = jnp.exp(tot - cum)
    q_t = r * e_inc
    a_t = -kkn * jnp.exp(cum - lw)
    k_t = kh * e_inv
    b_t = bb * e_inv
    k_hat = kh * e_rem
    b_hat = bb * e_rem
    w_tot = jnp.exp(tot)

    s2 = 2 * cs
    sr = lax.broadcasted_iota(jnp.int32, (s2, s2), 0) % cs
    sc = lax.broadcasted_iota(jnp.int32, (s2, s2), 1) % cs
    strict = sr > sc
    incl = sr >= sc
    eye2 = jnp.where(lax.broadcasted_iota(jnp.int32, (s2, s2), 0)
                     == lax.broadcasted_iota(jnp.int32, (s2, s2), 1), 1.0, 0.0)

    def stack(x):
        return jnp.concatenate([jnp.where(head0, x, 0.0), jnp.where(head0, 0.0, x)], axis=0)

    n_dbl = int(math.log2(cs)) - 1
    zeros_s = jnp.zeros((s2, LANES), F32)
    ys = []
    for p in range(n_pairs):
        ls = slice(p * LANES, (p + 1) * LANES)
        a_s, q_s, k_s, b_s = stack(a_t[:, ls]), stack(q_t[:, ls]), stack(k_t[:, ls]), stack(b_t[:, ls])
        v_s, kh_s, bh_s = stack(v[:, ls]), stack(k_hat[:, ls]), stack(b_hat[:, ls])
        gram = _mmp(jnp.concatenate([a_s, q_s], axis=0), jnp.concatenate([b_s, k_s], axis=0), NT,
                    RW_PASSES["gram"])
        a_ab = jnp.where(strict, gram[:s2, :s2], 0.0)
        a_ak = jnp.where(strict, gram[:s2, s2:], 0.0)
        b_rb = jnp.where(incl, gram[s2:, :s2], 0.0)
        b_rk = jnp.where(incl, gram[s2:, s2:], 0.0)
        tinv = eye2 + a_ab
        apow = a_ab
        for _ in range(n_dbl):
            apow = _mmp(apow, apow, passes=RW_PASSES["inv"])
            tinv = tinv + _mmp(apow, tinv, passes=RW_PASSES["inv"])
        au = _mmp(tinv, jnp.concatenate([a_s, _mmp(a_ak, v_s, passes=RW_PASSES["mix"])], axis=1),
                  passes=RW_PASSES["mix"])
        abar, u0 = au[:, :LANES], au[:, LANES:]
        ry = _mmp(jnp.concatenate([b_rb, b_rk], axis=1),
                  jnp.concatenate([au, jnp.concatenate([zeros_s, v_s], axis=1)], axis=0),
                  passes=RW_PASSES["mix"])
        rbar = q_s + ry[:, :LANES]
        y0 = ry[:, LANES:]
        m_p = eye * w_tot[:, ls] + _mmp(abar, bh_s, TN, RW_PASSES["trans"])
        n_p = _mmp(jnp.concatenate([u0, v_s], axis=0), jnp.concatenate([bh_s, kh_s], axis=0), TN,
                   RW_PASSES["trans"])
        s_old = s_scr[p]
        y_s = _mmp(rbar, s_old, NT, RW_PASSES["out"]) + y0
        s_scr[p] = _mmp(s_old, m_p, passes=RW_PASSES["state"]) + n_p
        ys.append(y_s[:cs] + y_s[cs:])
    y = jnp.concatenate(ys, axis=1)

    mean = seg_sum(y) * (1.0 / RWKV_HD)
    dy = y - mean
    var = seg_sum(dy * dy) * (1.0 / RWKV_HD)
    yn = dy * lax.rsqrt(var + GN_EPS) * ln_w + ln_b
    o_ref[...] = ((yn + bonus) * gate).astype(o_ref.dtype)

    @pl.when(ci == pl.num_programs(1) - 1)
    def _():
        sfin_ref[...] = s_scr[...]


def _rwkv(p, shift_rkv, shift_aux, s0_pairs, mu_rkv, mu_aux, vecs, ww, wa, wg, n_seq, t_len, cs, n_valid, out_dtype):
    n_chunks = t_len // cs
    n_pairs = s0_pairs.shape[1]
    dr = n_pairs * LANES
    full = lambda shape: pl.BlockSpec(shape, lambda b, c: (0,) * len(shape))
    return pl.pallas_call(
        functools.partial(_rwkv_kernel, cs=cs, n_valid=n_valid),
        grid=(n_seq, n_chunks),
        in_specs=[pl.BlockSpec((cs, 3 * dr), lambda b, c: (b * n_chunks + c, P_RKV // (3 * dr))),
                  pl.BlockSpec((cs, P_AUX_W), lambda b, c: (b * n_chunks + c, P_AUX // P_AUX_W)),
                  pl.BlockSpec((None, 1, 3 * dr), lambda b, c: (b, 0, 0)),
                  pl.BlockSpec((None, 1, P_AUX_W), lambda b, c: (b, 0, 0)),
                  pl.BlockSpec((None, n_pairs, LANES, LANES), lambda b, c: (b, 0, 0, 0)),
                  full(mu_rkv.shape), full(mu_aux.shape), full(vecs.shape),
                  full(ww.shape), full(wa.shape), full(wg.shape)],
        out_specs=[pl.BlockSpec((cs, dr), lambda b, c: (b * n_chunks + c, 0)),
                   pl.BlockSpec((None, n_pairs, LANES, LANES), lambda b, c: (b, 0, 0, 0))],
        out_shape=[jax.ShapeDtypeStruct((n_seq * t_len, dr), out_dtype),
                   jax.ShapeDtypeStruct((n_seq, n_pairs, LANES, LANES), F32)],
        scratch_shapes=[pltpu.VMEM((n_pairs, LANES, LANES), F32),
                        pltpu.VMEM((1, 3 * dr), F32), pltpu.VMEM((1, P_AUX_W), F32)],
        compiler_params=_cparams(("arbitrary", "arbitrary")),
    )(p, p, shift_rkv, shift_aux, s0_pairs, mu_rkv, mu_aux, vecs, ww, wa, wg)


def _out_kernel(x_ref, m_ref, on_ref, or_ref, wn_ref, wr_ref, o_ref):
    mix = _mm(on_ref[...], wn_ref[...]) + _mm(or_ref[...], wr_ref[...])
    o_ref[...] = x_ref[...] + m_ref[5] * mix


def _out_proj(x, mod, o_nsa, o_rw, w_top, w_bot, tm, tiles_per_seq):
    m_rows, d = x.shape
    dn = o_nsa.shape[1]
    mr = mod.shape[2]
    return pl.pallas_call(
        _out_kernel,
        grid=(m_rows // tm,),
        in_specs=[pl.BlockSpec((tm, d), lambda i: (i, 0)),
                  pl.BlockSpec((None, N_MOD, mr, d), lambda i: (i // tiles_per_seq, 0, 0, 0)),
                  pl.BlockSpec((tm, dn), lambda i: (i, 0)),
                  pl.BlockSpec((tm, dn), lambda i: (i, 0)),
                  pl.BlockSpec((dn, d), lambda i: (0, 0)),
                  pl.BlockSpec((dn, d), lambda i: (0, 0))],
        out_specs=pl.BlockSpec((tm, d), lambda i: (i, 0)),
        out_shape=jax.ShapeDtypeStruct((m_rows, d), F32),
        compiler_params=_cparams(("arbitrary",)),
    )(x, mod, o_nsa, o_rw, w_top, w_bot)


def _pad_cols(x, n):
    return jnp.pad(x, [(0, 0)] * (x.ndim - 1) + [(0, n)])


def _reorder_cols(w, d_rwkv, n_gate):
    nsa_main = P_AUX
    rw0 = nsa_main + n_gate
    lora0 = rw0 + 3 * d_rwkv
    n_lora = w.shape[-1] - lora0
    aux = jnp.concatenate([_pad_cols(w[..., nsa_main:rw0], AUX_WD - n_gate),
                           _pad_cols(w[..., lora0:], P_AUX_W - AUX_WD - n_lora)], axis=-1)
    return jnp.concatenate([w[..., :nsa_main], aux, w[..., rw0:lora0]], axis=-1)


def _pairs_from_heads(s):
    n, h = s.shape[:2]
    s = s.reshape(n, h // 2, 2, RWKV_HD, RWKV_HD)
    z = jnp.zeros_like(s[:, :, 0])
    top = jnp.concatenate([s[:, :, 0], z], axis=-1)
    bot = jnp.concatenate([z, s[:, :, 1]], axis=-1)
    return jnp.concatenate([top, bot], axis=-2)


def _heads_from_pairs(s):
    n, hp = s.shape[:2]
    a = s[:, :, :RWKV_HD, :RWKV_HD]
    b = s[:, :, RWKV_HD:, RWKV_HD:]
    return jnp.stack([a, b], axis=2).reshape(n, 2 * hp, RWKV_HD, RWKV_HD)


def kernel(x_prompt, x_sample, cache_nsa_kv, state_win_kv, state_wkv, state_shift, page_table,
           c_prompt, c_sample, w_ada, b_ada, norm_g, ffn_wi, ffn_wo, w_in, w_out,
           q_norm_g, k_norm_g, cmp_pe, cmp_w1, cmp_b1, cmp_w2,
           rwkv_mu, rwkv_w0, rwkv_w_w2, rwkv_a0, rwkv_w_a2, rwkv_w_g2,
           rwkv_k_k, rwkv_k_a, rwkv_r_k, rwkv_ln_w, rwkv_ln_b):
    bp, tp, d = x_prompt.shape
    bs, ts, _ = x_sample.shape
    depth = w_ada.shape[0]
    n_pool = cache_nsa_kv.shape[1]
    n_pages = page_table.shape[1]
    past = n_pages * PAGE
    wbuf = state_win_kv.shape[2]
    d_rwkv = rwkv_w0.shape[1]
    n_heads = d_rwkv // RWKV_HD
    d_nsa = N_QH * NSA_HD
    n_gate = 3 * N_QH
    n_dlora = rwkv_w_w2.shape[1]
    n_alora = rwkv_w_a2.shape[1]
    n_glora = rwkv_w_g2.shape[1]
    mp_rows, ms_rows = bp * tp, bs * ts
    tm = 512
    tiles_per_seq = tp // tm
    cs = 64
    ts_pad = -(-ts // cs) * cs
    win_keep = min(WINDOW, tp)

    c_all = jnp.concatenate([c_prompt, c_sample], axis=0)
    c_rows = -(-c_all.shape[0] // 8) * 8
    c_all = jnp.pad(c_all, ((0, c_rows - c_all.shape[0]), (0, 0)))
    mod = _modulation(c_all, w_ada, b_ada).reshape(depth, c_rows, N_MOD, d)
    mod_p = mod[:, :bp].reshape(depth, bp, N_MOD, 1, d)
    mod_s = jnp.repeat(mod[:, bp:bp + bs].transpose(0, 2, 1, 3), ts, axis=2).reshape(depth, 1, N_MOD, ms_rows, d)

    xp = x_prompt.reshape(mp_rows, d)
    xs = x_sample.reshape(ms_rows, d)
    pool = cache_nsa_kv.reshape(depth * n_pool * PAGE * 4 * N_KV, NSA_HD)
    prompt_pages = jnp.arange(bp * (tp // PAGE), dtype=jnp.int32).reshape(bp, tp // PAGE)
    n_sel_s = -(-(past + ts) // SEL_BLOCK)

    kv_p, kv_s, win_p, win_s, wkv_p, wkv_s, sh_p, sh_s = [], [], [], [], [], [], [], []
    for l in range(depth):
        wi = [ffn_wi[l, i].astype(BF16) for i in range(2)]
        wo = [ffn_wo[l, i].astype(BF16) for i in range(2)]
        w_in_l = _reorder_cols(w_in[l], d_rwkv, n_gate).astype(BF16)
        w_top = w_out[l, :d_nsa].astype(BF16)
        w_bot = w_out[l, d_nsa:].astype(BF16)
        g_rows = [norm_g[l, i].reshape(1, d) for i in range(3)]
        q_g = q_norm_g[l].reshape(1, NSA_HD)
        k_g = k_norm_g[l]
        w1 = cmp_w1[l].reshape(2, 2, CMP_STRIDE * NSA_HD, CMP_HID)
        w1cat = jnp.concatenate([w1[:, 0], w1[:, 1]], axis=-1).astype(BF16)
        pe2 = cmp_pe[l].reshape(2, 2, 1, CMP_STRIDE * NSA_HD)
        b1 = cmp_b1[l].reshape(2, 1, CMP_HID)
        w2 = cmp_w2[l].astype(BF16)
        def shift_parts(sh):
            aux = jnp.pad(sh[:, 3 * d_rwkv:], ((0, 0), (AUX_WD, P_AUX_W - AUX_WD - (sh.shape[1] - 3 * d_rwkv))))
            return sh[:, None, :3 * d_rwkv], aux[:, None, :]

        mu_rkv, mu_aux = (m[:, 0] for m in shift_parts(rwkv_mu[l].reshape(1, -1)))
        vecs = jnp.stack([rwkv_w0[l], rwkv_a0[l], rwkv_k_k[l], rwkv_k_a[l], rwkv_r_k[l].reshape(-1),
                          rwkv_ln_w[l], rwkv_ln_b[l], jnp.zeros_like(rwkv_w0[l])])
        lora_rows = lambda w, off: jnp.pad(w, ((off, P_AUX_W - off - w.shape[0]), (0, 0))).astype(BF16)
        ww = lora_rows(rwkv_w_w2[l], AUX_WD)
        wa = lora_rows(rwkv_w_a2[l], AUX_WD + n_dlora)
        wg = lora_rows(rwkv_w_g2[l], AUX_WD + n_dlora + n_alora)

        def shift_out(p_last):
            return jnp.concatenate([p_last[:, P_RKV:P_RKV + 3 * d_rwkv],
                                    p_last[:, P_AUX + AUX_WD:P_AUX + AUX_WD + n_dlora + n_alora + n_glora]],
                                   axis=1)

        xp = _ffn(xp, mod_p[l], g_rows[0], wi[0], wo[0], 0, tm, tiles_per_seq)
        xs = _ffn(xs, mod_s[l], g_rows[0], wi[0], wo[0], 0, ms_rows, 1)

        pp = _proj(xp, mod_p[l], g_rows[1], w_in_l, tm, tiles_per_seq)
        qn, rows, win, selbf, winbf, gates = _nsa_prep(pp, q_g, k_g, tm, BF16)
        kc, vc = _compress(rows, prompt_pages, w1cat, pe2, b1, w2, k_g)
        o_nsa = _nsa_prompt(qn, gates, selbf, winbf, kc, vc, bp, tp)
        zero_rkv, zero_aux = shift_parts(jnp.zeros((bp, state_shift.shape[2]), F32))
        o_rw, s_fin = _rwkv(pp, zero_rkv, zero_aux, jnp.zeros((bp, n_heads // 2, LANES, LANES), F32),
                            mu_rkv, mu_aux, vecs, ww, wa, wg, bp, tp, cs, cs, BF16)
        xp = _out_proj(xp, mod_p[l], o_nsa, o_rw, w_top, w_bot, tm, tiles_per_seq)
        kv_p.append(rows.reshape(bp, tp // PAGE, PAGE, 4, N_KV, NSA_HD))
        win_p.append(win.reshape(bp, tp, 2, N_KV, NSA_HD)[:, tp - win_keep:])
        wkv_p.append(_heads_from_pairs(s_fin))
        sh_p.append(shift_out(pp.reshape(bp, tp, P_COLS)[:, -1]))

        ps = _proj(xs, mod_s[l], g_rows[1], w_in_l, ms_rows, 1)
        qn_s, rows_s, win_new, _, _, gates_s = _nsa_prep(ps, q_g, k_g, ms_rows, F32)
        page_idx = page_table + l * n_pool
        kc_s, vc_s = _compress(pool, page_idx, w1cat, pe2, b1, w2, k_g)
        o_cmp, sel = _nsa_sample_select(qn_s, kc_s, vc_s, ts, past, n_sel_s)
        o_nsa_s = _nsa_sample_sweep(pool, page_idx, qn_s, gates_s, sel, o_cmp, rows_s, win_new,
                                    state_win_kv[l].reshape(bs, wbuf * 2 * N_KV, NSA_HD), ts, past, n_sel_s)
        sh_rkv, sh_aux = shift_parts(state_shift[l])
        ps_pad = jnp.pad(ps.reshape(bs, ts, P_COLS), ((0, 0), (0, ts_pad - ts), (0, 0)))
        o_rw_s, s_fin_s = _rwkv(ps_pad.reshape(bs * ts_pad, P_COLS), sh_rkv, sh_aux,
                                _pairs_from_heads(state_wkv[l]), mu_rkv, mu_aux, vecs, ww, wa, wg,
                                bs, ts_pad, ts_pad, ts, F32)
        o_rw_s = o_rw_s.reshape(bs, ts_pad, d_rwkv)[:, :ts].reshape(ms_rows, d_rwkv)
        xs = _out_proj(xs, mod_s[l], o_nsa_s.reshape(ms_rows, d_nsa).astype(BF16), o_rw_s.astype(BF16),
                       w_top, w_bot, ms_rows, 1)
        kv_s.append(rows_s.reshape(bs, ts, 4, N_KV, NSA_HD))
        win_s.append(jnp.concatenate([state_win_kv[l][:, ts:],
                                      win_new.reshape(bs, ts, 2, N_KV, NSA_HD)], axis=1))
        wkv_s.append(_heads_from_pairs(s_fin_s))
        sh_s.append(shift_out(ps.reshape(bs, ts, P_COLS)[:, -1]))

        xp = _ffn(xp, mod_p[l], g_rows[2], wi[1], wo[1], 2, tm, tiles_per_seq)
        xs = _ffn(xs, mod_s[l], g_rows[2], wi[1], wo[1], 2, ms_rows, 1)

    return (xp.reshape(bp, tp, d), xs.reshape(bs, ts, d),
            jnp.stack(kv_p), jnp.stack(kv_s), jnp.stack(win_p), jnp.stack(win_s),
            jnp.stack(wkv_p), jnp.stack(wkv_s), jnp.stack(sh_p), jnp.stack(sh_s))
```

```python
import functools
import math

import jax
import jax.numpy as jnp
from jax import lax
from jax.experimental import pallas as pl
from jax.experimental.pallas import tpu as pltpu

F32 = jnp.float32
BF16 = jnp.bfloat16

NSA_HD = 128
N_KV = 2
QPG = 4
N_QH = N_KV * QPG
CMP_LEN = 32
CMP_STRIDE = 16
CMP_HID = 2 * NSA_HD
SEL_BLOCK = 64
SEL_TOPK = 16
N_LOCAL = 2
WINDOW = 512
PAGE = 128
RWKV_HD = 64
N_MOD = 9
NORM_EPS = 1e-6
GN_EPS = 64e-5
NEG = -1e30
FORCED = 1e6
INVALID = -1e6

LANES = 128
VMEM_LIMIT = 56 * 1024 * 1024
PAGES_PER_STEP = 8

P_Q = 0
P_KV = 1024
P_AUX = 2560
P_AUX_W = 512
P_RKV = 3072
P_COLS = 6144
AUX_WD = 128
AUX_AD = 192
AUX_GD = 256


def _cparams(sem):
    return pltpu.CompilerParams(dimension_semantics=sem, vmem_limit_bytes=VMEM_LIMIT)


def _mm(a, b, dims=((1,), (0,))):
    return lax.dot_general(a, b, (dims, ((), ())), preferred_element_type=F32)


NT = ((1,), (1,))
TN = ((0,), (0,))


def _split2(x):
    hi = x.astype(BF16)
    lo = (x - hi.astype(F32)).astype(BF16)
    return hi, lo


def _split3(x):
    hi = x.astype(BF16)
    r1 = x - hi.astype(F32)
    mid = r1.astype(BF16)
    lo = (r1 - mid.astype(F32)).astype(BF16)
    return hi, mid, lo


def _mm3(a, b, dims=((1,), (0,))):
    ah, al = _split2(a)
    bh, bl = _split2(b)
    return _mm(ah, bh, dims) + (_mm(ah, bl, dims) + _mm(al, bh, dims))


def _mmp(a, b, dims=((1,), (0,)), passes=1):
    if passes == 3:
        return _mm3(a, b, dims)
    return _mm(a.astype(BF16), b.astype(BF16), dims)


RW_PASSES = {"gram": 1, "inv": 1, "mix": 1, "trans": 1, "out": 1, "state": 3}


def _mm_lhs_exact(a, b_bf16, dims=((1,), (0,))):
    a1, a2, a3 = _split3(a)
    return _mm(a1, b_bf16, dims) + (_mm(a2, b_bf16, dims) + _mm(a3, b_bf16, dims))


def _mm_rhs_exact(a_bf16, b, dims=((1,), (0,))):
    b1, b2, b3 = _split3(b)
    return _mm(a_bf16, b1, dims) + (_mm(a_bf16, b2, dims) + _mm(a_bf16, b3, dims))


def _silu(x):
    return x * jax.nn.sigmoid(x)


def _rms(x, g):
    return x * lax.rsqrt(jnp.mean(x * x, axis=-1, keepdims=True) + NORM_EPS) * g


def _ada_norm(x, m_ref, slot, g):
    return _rms(x, g) * (1.0 + m_ref[3 * slot + 1]) + m_ref[3 * slot]


def _mod_kernel(c_ref, w_ref, b_ref, o_ref):
    s = _silu(c_ref[...]).astype(BF16)
    o_ref[0] = _mm(s, w_ref[0].astype(BF16)) + b_ref[0]


def _modulation(c_all, w_ada, b_ada):
    depth, d, n = w_ada.shape
    rows = c_all.shape[0]
    tn = 1024
    return pl.pallas_call(
        _mod_kernel,
        grid=(depth, n // tn),
        in_specs=[pl.BlockSpec((rows, d), lambda l, j: (0, 0)),
                  pl.BlockSpec((1, d, tn), lambda l, j: (l, 0, j)),
                  pl.BlockSpec((1, 1, tn), lambda l, j: (l, 0, j))],
        out_specs=pl.BlockSpec((1, rows, tn), lambda l, j: (l, 0, j)),
        out_shape=jax.ShapeDtypeStruct((depth, rows, n), F32),
        compiler_params=_cparams(("arbitrary", "arbitrary")),
    )(c_all, w_ada, b_ada.reshape(depth, 1, n))


def _ffn_kernel(x_ref, m_ref, g_ref, wg_ref, wu_ref, wo_ref, o_ref, h_scr, acc_scr, *, slot):
    f = pl.program_id(1)

    @pl.when(f == 0)
    def _():
        h_scr[...] = _ada_norm(x_ref[...], m_ref, slot, g_ref[...]).astype(BF16)
        acc_scr[...] = jnp.zeros_like(acc_scr)

    h = h_scr[...]
    gate = _mm(h, wg_ref[...])
    up = _mm(h, wu_ref[...])
    act = (_silu(gate) * up).astype(BF16)
    acc_scr[...] += _mm(act, wo_ref[...])

    @pl.when(f == pl.num_programs(1) - 1)
    def _():
        o_ref[...] = x_ref[...] + 0.5 * m_ref[3 * slot + 2] * acc_scr[...]


def _ffn(x, mod, g, wi, wo, slot, tm, tiles_per_seq):
    m_rows, d = x.shape
    d_ff = wo.shape[0]
    tf = 512
    nf = d_ff // tf
    mr = mod.shape[2]
    return pl.pallas_call(
        functools.partial(_ffn_kernel, slot=slot),
        grid=(m_rows // tm, nf),
        in_specs=[pl.BlockSpec((tm, d), lambda i, f: (i, 0)),
                  pl.BlockSpec((None, N_MOD, mr, d), lambda i, f: (i // tiles_per_seq, 0, 0, 0)),
                  pl.BlockSpec((1, d), lambda i, f: (0, 0)),
                  pl.BlockSpec((d, tf), lambda i, f: (0, f)),
                  pl.BlockSpec((d, tf), lambda i, f: (0, nf + f)),
                  pl.BlockSpec((tf, d), lambda i, f: (f, 0))],
        out_specs=pl.BlockSpec((tm, d), lambda i, f: (i, 0)),
        out_shape=jax.ShapeDtypeStruct((m_rows, d), F32),
        scratch_shapes=[pltpu.VMEM((tm, d), BF16), pltpu.VMEM((tm, d), F32)],
        compiler_params=_cparams(("arbitrary", "arbitrary")),
    )(x, mod, g, wi, wi, wo)


def _proj_kernel(x_ref, m_ref, g_ref, w_ref, o_ref, h_scr):
    @pl.when(pl.program_id(1) == 0)
    def _():
        h_scr[...] = _ada_norm(x_ref[...], m_ref, 1, g_ref[...]).astype(BF16)

    o_ref[...] = _mm(h_scr[...], w_ref[...])


def _proj(x, mod, g, w, tm, tiles_per_seq):
    m_rows, d = x.shape
    n = w.shape[1]
    tn = 1536
    mr = mod.shape[2]
    return pl.pallas_call(
        _proj_kernel,
        grid=(m_rows // tm, n // tn),
        in_specs=[pl.BlockSpec((tm, d), lambda i, j: (i, 0)),
                  pl.BlockSpec((None, N_MOD, mr, d), lambda i, j: (i // tiles_per_seq, 0, 0, 0)),
                  pl.BlockSpec((1, d), lambda i, j: (0, 0)),
                  pl.BlockSpec((d, tn), lambda i, j: (0, j))],
        out_specs=pl.BlockSpec((tm, tn), lambda i, j: (i, j)),
        out_shape=jax.ShapeDtypeStruct((m_rows, n), F32),
        scratch_shapes=[pltpu.VMEM((tm, d), BF16)],
        compiler_params=_cparams(("arbitrary", "arbitrary")),
    )(x, mod, g, w)


def _rms_heads(x, g):
    outs = []
    for h in range(x.shape[1] // NSA_HD):
        outs.append(_rms(x[:, h * NSA_HD:(h + 1) * NSA_HD], g))
    return jnp.concatenate(outs, axis=1)


def _nsa_prep_kernel(q_ref, kva_ref, kvb_ref, kvc_ref, gt_ref, qg_ref, kg_ref,
                     qn_ref, rows_ref, win_ref, selbf_ref, winbf_ref, gates_ref):
    qn_ref[...] = (_rms_heads(q_ref[...], qg_ref[...]) * (NSA_HD ** -0.5)).astype(qn_ref.dtype)
    kvb = kvb_ref[...]
    ksel = _rms_heads(kvb[:, :2 * NSA_HD], kg_ref[1:2, :])
    selrows = jnp.concatenate([ksel, kvb[:, 2 * NSA_HD:]], axis=1)
    rows = jnp.concatenate([kva_ref[...], selrows], axis=1)
    tm = rows.shape[0]
    for c in range(4 * N_KV):
        rows_ref[pl.ds(c, tm, stride=4 * N_KV), :] = rows[:, c * NSA_HD:(c + 1) * NSA_HD]
    selbf_ref[...] = selrows.astype(selbf_ref.dtype)
    kvc = kvc_ref[...]
    kwin = _rms_heads(kvc[:, :2 * NSA_HD], kg_ref[2:3, :])
    winrows = jnp.concatenate([kwin, kvc[:, 2 * NSA_HD:]], axis=1)
    for c in range(2 * N_KV):
        win_ref[pl.ds(c, tm, stride=2 * N_KV), :] = winrows[:, c * NSA_HD:(c + 1) * NSA_HD]
    winbf_ref[...] = winrows.astype(winbf_ref.dtype)
    gates_ref[...] = jax.nn.sigmoid(gt_ref[...])


def _nsa_prep(p, q_g, k_g, tm, act_dtype):
    m_rows = p.shape[0]
    row = lambda w, j: pl.BlockSpec((tm, w), lambda i: (i, j))
    return pl.pallas_call(
        _nsa_prep_kernel,
        grid=(m_rows // tm,),
        in_specs=[row(1024, 0), row(512, 2), row(512, 3), row(512, 4), row(LANES, P_AUX // LANES),
                  pl.BlockSpec((1, NSA_HD), lambda i: (0, 0)),
                  pl.BlockSpec((3, NSA_HD), lambda i: (0, 0))],
        out_specs=[row(1024, 0), pl.BlockSpec((tm * 8, NSA_HD), lambda i: (i, 0)),
                   pl.BlockSpec((tm * 4, NSA_HD), lambda i: (i, 0)),
                   row(512, 0), row(512, 0), row(LANES, 0)],
        out_shape=[jax.ShapeDtypeStruct((m_rows, 1024), act_dtype),
                   jax.ShapeDtypeStruct((m_rows * 8, NSA_HD), F32),
                   jax.ShapeDtypeStruct((m_rows * 4, NSA_HD), F32),
                   jax.ShapeDtypeStruct((m_rows, 512), act_dtype),
                   jax.ShapeDtypeStruct((m_rows, 512), act_dtype),
                   jax.ShapeDtypeStruct((m_rows, LANES), F32)],
        compiler_params=_cparams(("arbitrary",)),
    )(p, p, p, p, p, q_g, k_g)


def _cmp_kernel(pt_ref, *refs, n_sub, pp):
    pages = refs[:pp]
    w1_ref, pe_ref, b1_ref, w2_ref, kg_ref, kc_ref, vc_ref, x_scr = refs[pp:]
    p = pl.program_id(1)
    spp = PAGE // CMP_STRIDE
    base = pl.multiple_of(p * (pp * spp), pp * spp)
    heads = 4 * N_KV
    for c in range(2 * N_KV):
        for l in range(CMP_STRIDE):
            pieces = [pg[pl.ds(l * heads + c, spp, stride=CMP_STRIDE * heads), :] for pg in pages]
            x_scr[c, pl.ds(base, pp * spp), l * NSA_HD:(l + 1) * NSA_HD] = (
                jnp.concatenate(pieces, axis=0).astype(BF16))

    @pl.when(p == pl.num_programs(1) - 1)
    def _():
        row = lax.broadcasted_iota(jnp.int32, (n_sub, NSA_HD), 0)
        for kv in range(2):
            w1 = w1_ref[kv]
            const = b1_ref[kv]
            for j in range(2):
                pe = jnp.broadcast_to(pe_ref[kv, j], (8, CMP_STRIDE * NSA_HD)).astype(BF16)
                const = const + _mm(pe, w1[:, j * CMP_HID:(j + 1) * CMP_HID])[0:1]
            for g in range(N_KV):
                ab = _mm(x_scr[kv * 2 + g], w1)
                nxt = pltpu.roll(ab[:, CMP_HID:], n_sub - 1, 0)
                acc = ab[:, :CMP_HID] + nxt + const
                o = _mm(jax.nn.gelu(acc).astype(BF16), w2_ref[kv])
                if kv == 0:
                    o = _rms(o, kg_ref[0:1, :])
                o = jnp.where(row < n_sub - 1, o, 0.0)
                if kv == 0:
                    kc_ref[g] = o
                else:
                    vc_ref[g] = o


def _compress(pool, page_idx, w1cat, pe2, b1, w2, k_g):
    nb, n_pages = page_idx.shape
    n_sub = n_pages * (PAGE // CMP_STRIDE)
    pp = PAGES_PER_STEP
    page = lambda k: pl.BlockSpec((PAGE * 4 * N_KV, NSA_HD), lambda b, p, pt: (pt[b, pp * p + k], 0))
    page_specs = [page(k) for k in range(pp)]
    full = lambda shape: pl.BlockSpec(shape, lambda b, p, pt: (0,) * len(shape))
    out = pl.BlockSpec((None, N_KV, n_sub, NSA_HD), lambda b, p, pt: (b, 0, 0, 0))
    return pl.pallas_call(
        functools.partial(_cmp_kernel, n_sub=n_sub, pp=pp),
        grid_spec=pltpu.PrefetchScalarGridSpec(
            num_scalar_prefetch=1,
            grid=(nb, n_pages // pp),
            in_specs=page_specs + [full(w1cat.shape), full(pe2.shape), full(b1.shape),
                      full(w2.shape), full(k_g.shape)],
            out_specs=[out, out],
            scratch_shapes=[pltpu.VMEM((4, n_sub, CMP_STRIDE * NSA_HD), BF16)]),
        out_shape=[jax.ShapeDtypeStruct((nb, N_KV, n_sub, NSA_HD), F32)] * 2,
        compiler_params=_cparams(("arbitrary", "arbitrary")),
    )(page_idx, *([pool] * pp), w1cat, pe2, b1, w2, k_g)


def _imp_matrix(n_sub, n_cmp, width):
    n = lax.broadcasted_iota(jnp.int32, (n_sub, width), 0)
    j = lax.broadcasted_iota(jnp.int32, (n_sub, width), 1)
    spb = SEL_BLOCK // CMP_STRIDE
    m = jnp.where(n // spb == j, 1.0, 0.0) + jnp.where((n + 1) // spb == j, 1.0, 0.0)
    return jnp.where(n < n_cmp, m, 0.0).astype(BF16)


def _topk_mask_t(st_scr, n_iter, width):
    jj = st_scr.shape[0]
    st = st_scr[...]
    jrow = lax.broadcasted_iota(jnp.int32, (jj, width), 0)

    def body(jp, rank):
        r = st_scr[pl.ds(jp, 1), :]
        beats = (r > st) | ((r == st) & (jp < jrow))
        return rank + jnp.where(beats, 1.0, 0.0)

    return lax.fori_loop(0, n_iter, body, jnp.zeros((jj, width), F32))


def _softmax_rows(s, ok):
    sm = jnp.where(ok, s, NEG)
    mx = jnp.max(sm, axis=-1, keepdims=True)
    e = jnp.where(ok, jnp.exp(sm - mx), 0.0)
    den = jnp.sum(e, axis=-1, keepdims=True)
    return e / jnp.where(den > 0.0, den, 1.0)


def _nsa_prompt_kernel(q_ref, gt_ref, sel_ref, win_ref, kc_ref, vc_ref, o_ref, st_scr,
                       *, t_len, n_sub, ck):
    tq = SEL_BLOCK
    qi = pl.program_id(1)
    s0 = qi * tq
    n_cmp = n_sub - 1
    rows = QPG * tq
    q = q_ref[...]
    rowpos = s0 + lax.broadcasted_iota(jnp.int32, (rows, 1), 0) % tq

    def q_group(g):
        return jnp.concatenate(
            [q[:, (g * QPG + h) * NSA_HD:(g * QPG + h + 1) * NSA_HD] for h in range(QPG)], axis=0)

    imp_m = _imp_matrix(n_sub, n_cmp, LANES)
    n_idx = lax.broadcasted_iota(jnp.int32, (rows, n_sub), 1)
    ok_c = (n_idx * CMP_STRIDE + (CMP_LEN - 1) <= rowpos) & (n_idx < n_cmp)
    o_cmp, scores = [], []
    jl = lax.broadcasted_iota(jnp.int32, (tq, LANES), 1)
    forced = (jl == 0) | ((jl <= qi) & (jl > qi - N_LOCAL))
    for g in range(N_KV):
        qg = q_group(g)
        s_c = _mm(qg, kc_ref[g].astype(BF16), NT)
        p_c = _softmax_rows(s_c, ok_c)
        o_cmp.append(_mm(p_c.astype(BF16), vc_ref[g].astype(BF16)))
        p_g = p_c[0:tq] + p_c[tq:2 * tq] + p_c[2 * tq:3 * tq] + p_c[3 * tq:4 * tq]
        imp = _mm_lhs_exact(p_g, imp_m)
        scores.append(jnp.where(jl <= qi, jnp.where(forced, FORCED, imp), INVALID))
    st_scr[...] = jnp.concatenate(scores, axis=0).T
    rank = _topk_mask_t(st_scr, qi + 1, LANES)
    jrow = lax.broadcasted_iota(jnp.int32, (LANES, N_KV * tq), 0)
    sel_t = jnp.where((rank < SEL_TOPK) & (jrow <= qi), 1.0, 0.0)
    sel = sel_t.T.astype(BF16)

    qgs = [q_group(g) for g in range(N_KV)]
    tokpos = s0 + lax.broadcasted_iota(jnp.int32, (tq, 1), 0)

    def chunk(c, carry, diagonal):
        k0 = pl.multiple_of(c * ck, ck)
        keypos = k0 + lax.broadcasted_iota(jnp.int32, (1, ck), 1)
        ej = lax.broadcasted_iota(jnp.int32, (LANES, ck), 0)
        expand = jnp.where(ej == keypos // SEL_BLOCK, 1.0, 0.0).astype(BF16)
        bias2 = _mm(sel, expand) * (-NEG) + NEG
        new = []
        for g in range(N_KV):
            m_i, l_i, acc = carry[g]
            kk = sel_ref[pl.ds(k0, ck), g * NSA_HD:(g + 1) * NSA_HD]
            vv = sel_ref[pl.ds(k0, ck), (N_KV + g) * NSA_HD:(N_KV + g + 1) * NSA_HD]
            bg = bias2[g * tq:(g + 1) * tq]
            if diagonal:
                bg = jnp.where(keypos <= tokpos, bg, NEG)
            s = _mm(qgs[g], kk, NT) + jnp.concatenate([bg] * QPG, axis=0)
            m_new = jnp.maximum(m_i, jnp.max(s, axis=-1, keepdims=True))
            pr = jnp.exp(s - m_new)
            alpha = jnp.exp(m_i - m_new)
            l_new = alpha * l_i + jnp.sum(pr, axis=-1, keepdims=True)
            acc_new = alpha * acc + _mm(pr.astype(BF16), vv)
            new.append((m_new, l_new, acc_new))
        return tuple(new)

    init = tuple((jnp.full((rows, 1), NEG, F32), jnp.zeros((rows, 1), F32),
                  jnp.zeros((rows, NSA_HD), F32)) for _ in range(N_KV))
    n_full = s0 // ck
    mid = lax.fori_loop(0, n_full, functools.partial(chunk, diagonal=False), init)
    fin = chunk(n_full, mid, True)

    wlen = WINDOW + 2 * tq
    w0 = pl.multiple_of(jnp.clip(s0 - WINDOW, 0, t_len - wlen), tq)
    wpos = w0 + lax.broadcasted_iota(jnp.int32, (1, wlen), 1)
    dist = rowpos - wpos
    bias_w = jnp.where((dist >= 0) & (dist < WINDOW), 0.0, NEG)
    gt = gt_ref[...]
    outs = []
    for g in range(N_KV):
        kw = win_ref[pl.ds(w0, wlen), g * NSA_HD:(g + 1) * NSA_HD]
        vw = win_ref[pl.ds(w0, wlen), (N_KV + g) * NSA_HD:(N_KV + g + 1) * NSA_HD]
        s_w = _mm(qgs[g], kw, NT) + bias_w
        e_w = jnp.exp(s_w - jnp.max(s_w, axis=-1, keepdims=True))
        p_w = e_w / jnp.sum(e_w, axis=-1, keepdims=True)
        o_w = _mm(p_w.astype(BF16), vw)
        _, l_i, acc = fin[g]
        o_s = acc / l_i
        for h in range(QPG):
            c0 = (g * QPG + h) * 3
            r0 = slice(h * tq, (h + 1) * tq)
            outs.append(gt[:, c0:c0 + 1] * o_cmp[g][r0] + gt[:, c0 + 1:c0 + 2] * o_s[r0]
                        + gt[:, c0 + 2:c0 + 3] * o_w[r0])
    o_ref[...] = jnp.concatenate(outs, axis=1).astype(o_ref.dtype)


def _nsa_prompt(qn, gates, selbf, winbf, kc, vc, nb, t_len):
    n_sub = kc.shape[2]
    tq = SEL_BLOCK
    nq = t_len // tq
    ck = 512
    per_b = lambda w: pl.BlockSpec((None, t_len, w), lambda b, i: (b, 0, 0))
    cmp_spec = pl.BlockSpec((None, N_KV, n_sub, NSA_HD), lambda b, i: (b, 0, 0, 0))
    return pl.pallas_call(
        functools.partial(_nsa_prompt_kernel, t_len=t_len, n_sub=n_sub, ck=ck),
        grid=(nb, nq),
        in_specs=[pl.BlockSpec((tq, 1024), lambda b, i: (b * nq + i, 0)),
                  pl.BlockSpec((tq, LANES), lambda b, i: (b * nq + i, 0)),
                  per_b(512), per_b(512), cmp_spec, cmp_spec],
        out_specs=pl.BlockSpec((tq, 1024), lambda b, i: (b * nq + i, 0)),
        out_shape=jax.ShapeDtypeStruct((nb * t_len, 1024), BF16),
        scratch_shapes=[pltpu.VMEM((LANES, N_KV * tq), F32)],
        compiler_params=_cparams(("arbitrary", "arbitrary")),
    )(qn, gates, selbf.reshape(nb, t_len, 512), winbf.reshape(nb, t_len, 512), kc, vc)


def _nsa_sample_select_kernel(q_ref, kc_ref, vc_ref, oc_ref, sel_ref, sc_scr, st_scr,
                              *, n_sub, n_sel, ts, past, jw):
    b = pl.program_id(0)
    nb = pl.num_programs(0)
    n_cmp = n_sub - 1
    q = q_ref[...]
    imp_m = _imp_matrix(n_sub, n_cmp, jw)
    rows = QPG * ts
    rowpos = past + lax.broadcasted_iota(jnp.int32, (rows, 1), 0) % ts
    n_idx = lax.broadcasted_iota(jnp.int32, (rows, n_sub), 1)
    ok_c = (n_idx * CMP_STRIDE + (CMP_LEN - 1) <= rowpos) & (n_idx < n_cmp)
    jl = lax.broadcasted_iota(jnp.int32, (ts, jw), 1)
    blk = (past + lax.broadcasted_iota(jnp.int32, (ts, 1), 0)) // SEL_BLOCK
    forced = (jl == 0) | ((jl <= blk) & (jl > blk - N_LOCAL))

    @pl.when(b == 0)
    def _():
        sc_scr[...] = jnp.full(sc_scr.shape, INVALID, F32)

    for g in range(N_KV):
        qg = jnp.concatenate(
            [q[:, (g * QPG + h) * NSA_HD:(g * QPG + h + 1) * NSA_HD] for h in range(QPG)],
            axis=0).astype(BF16)
        s_c = _mm(qg, kc_ref[g].astype(BF16), NT)
        p_c = _softmax_rows(s_c, ok_c)
        oc_ref[g * rows:(g + 1) * rows, :] = _mm(p_c.astype(BF16), vc_ref[g].astype(BF16))
        p_g = p_c[0:ts] + p_c[ts:2 * ts] + p_c[2 * ts:3 * ts] + p_c[3 * ts:4 * ts]
        imp = _mm_lhs_exact(p_g, imp_m)
        score = jnp.where((jl <= blk) & (jl < n_sel), jnp.where(forced, FORCED, imp), INVALID)
        sc_scr[pl.ds(pl.multiple_of((b * N_KV + g) * ts, ts), ts), :] = score

    @pl.when(b == nb - 1)
    def _():
        st_scr[...] = sc_scr[...].T
        rank = _topk_mask_t(st_scr, n_sel, LANES)
        sel_t = jnp.where((rank < SEL_TOPK) & (st_scr[...] > 0.5 * INVALID), 1.0, 0.0)
        sel_ref[...] = sel_t.T


def _nsa_sample_select(qn_s, kc, vc, ts, past, n_sel):
    nb = kc.shape[0]
    n_sub = kc.shape[2]
    jw = -(-n_sel // LANES) * LANES
    assert nb * N_KV * ts <= LANES
    cmp_spec = pl.BlockSpec((None, N_KV, n_sub, NSA_HD), lambda b: (b, 0, 0, 0))
    return pl.pallas_call(
        functools.partial(_nsa_sample_select_kernel, n_sub=n_sub, n_sel=n_sel, ts=ts, past=past, jw=jw),
        grid=(nb,),
        in_specs=[pl.BlockSpec((None, ts, 1024), lambda b: (b, 0, 0)), cmp_spec, cmp_spec],
        out_specs=[pl.BlockSpec((None, N_QH * ts, NSA_HD), lambda b: (b, 0, 0)),
                   pl.BlockSpec((LANES, jw), lambda b: (0, 0))],
        out_shape=[jax.ShapeDtypeStruct((nb, N_QH * ts, NSA_HD), F32),
                   jax.ShapeDtypeStruct((LANES, jw), F32)],
        scratch_shapes=[pltpu.VMEM((LANES, jw), F32), pltpu.VMEM((jw, LANES), F32)],
        compiler_params=_cparams(("arbitrary",)),
    )(qn_s.reshape(nb, ts, 1024), kc, vc)


def _nsa_sample_sweep_kernel(pt_ref, *refs, pp, ts, past, n_sel, wbuf):
    pages = refs[:pp]
    (q_ref, gt_ref, sel_ref, oc_ref, rows_ref, wnew_ref, wst_ref, o_ref,
     m_scr, l_scr, acc_scr) = refs[pp:]
    step = pl.program_id(1)
    rows = QPG * ts
    heads = 4 * N_KV
    jw = sel_ref.shape[1]
    q = q_ref[...]
    qgs = [jnp.concatenate(
        [q[:, (g * QPG + h) * NSA_HD:(g * QPG + h + 1) * NSA_HD] for h in range(QPG)],
        axis=0).astype(BF16) for g in range(N_KV)]
    sel = sel_ref[...].astype(BF16)

    @pl.when(step == 0)
    def _():
        m_scr[...] = jnp.full(m_scr.shape, NEG, F32)
        l_scr[...] = jnp.zeros_like(l_scr)
        acc_scr[...] = jnp.zeros_like(acc_scr)

    def update(g, s, ok, vv):
        r0 = slice(g * rows, (g + 1) * rows)
        s = jnp.where(ok, s, NEG)
        m_i = m_scr[r0]
        m_new = jnp.maximum(m_i, jnp.max(s, axis=-1, keepdims=True))
        pr = jnp.where(ok, jnp.exp(s - m_new), 0.0)
        alpha = jnp.exp(m_i - m_new)
        l_scr[r0] = alpha * l_scr[r0] + jnp.sum(pr, axis=-1, keepdims=True)
        acc_scr[r0] = alpha * acc_scr[r0] + _mm(pr.astype(BF16), vv)
        m_scr[r0] = m_new

    nk = pp * PAGE
    ej = lax.broadcasted_iota(jnp.int32, (jw, nk), 0)
    kl = lax.broadcasted_iota(jnp.int32, (1, nk), 1)
    expand = jnp.where(ej == (step * nk + kl) // SEL_BLOCK, 1.0, 0.0).astype(BF16)
    mask2 = _mm(sel, expand)
    for g in range(N_KV):
        kk = jnp.concatenate([pg[pl.ds(2 * N_KV + g, PAGE, stride=heads), :] for pg in pages],
                             axis=0).astype(BF16)
        vv = jnp.concatenate([pg[pl.ds(3 * N_KV + g, PAGE, stride=heads), :] for pg in pages],
                             axis=0).astype(BF16)
        mg = mask2[g * ts:(g + 1) * ts]
        ok = jnp.concatenate([mg] * QPG, axis=0) > 0.5
        update(g, _mm(qgs[g], kk, NT), ok, vv)

    @pl.when(step == pl.num_programs(1) - 1)
    def _():
        tpos = lax.broadcasted_iota(jnp.int32, (rows, 1), 0) % ts
        pad = jnp.zeros((PAGE - ts, NSA_HD), F32)
        il = lax.broadcasted_iota(jnp.int32, (1, PAGE), 1)
        last_sel = sel_ref[:, n_sel - 1:n_sel]
        gt = gt_ref[...]
        wlen = wbuf + PAGE
        wl = lax.broadcasted_iota(jnp.int32, (1, wlen), 1)
        dist = (past + tpos) - (past - wbuf + wl)
        ok_w = (dist >= 0) & (dist < WINDOW) & (wl < wbuf + ts)
        outs = []
        for g in range(N_KV):
            kn = jnp.concatenate([rows_ref[pl.ds(2 * N_KV + g, ts, stride=heads), :], pad], axis=0)
            vn = jnp.concatenate([rows_ref[pl.ds(3 * N_KV + g, ts, stride=heads), :], pad], axis=0)
            lsel = jnp.concatenate([last_sel[g * ts:(g + 1) * ts]] * QPG, axis=0) > 0.5
            ok = (il <= tpos) & (il < ts) & lsel
            update(g, _mm(qgs[g], kn.astype(BF16), NT), ok, vn.astype(BF16))
            r0 = slice(g * rows, (g + 1) * rows)
            o_s = acc_scr[r0] / l_scr[r0]
            kw = jnp.concatenate([wst_ref[pl.ds(g, wbuf, stride=2 * N_KV), :],
                                  wnew_ref[pl.ds(g, ts, stride=2 * N_KV), :], pad], axis=0).astype(BF16)
            vw = jnp.concatenate([wst_ref[pl.ds(N_KV + g, wbuf, stride=2 * N_KV), :],
                                  wnew_ref[pl.ds(N_KV + g, ts, stride=2 * N_KV), :], pad],
                                 axis=0).astype(BF16)
            p_w = _softmax_rows(_mm(qgs[g], kw, NT), ok_w)
            o_w = _mm(p_w.astype(BF16), vw)
            o_c = oc_ref[r0, :]
            for h in range(QPG):
                c0 = (g * QPG + h) * 3
                rh = slice(h * ts, (h + 1) * ts)
                outs.append(gt[:, c0:c0 + 1] * o_c[rh] + gt[:, c0 + 1:c0 + 2] * o_s[rh]
                            + gt[:, c0 + 2:c0 + 3] * o_w[rh])
        o_ref[...] = jnp.concatenate(outs, axis=1)


def _nsa_sample_sweep(pool, page_idx, qn_s, gates_s, sel, o_cmp, rows_s, win_s, win_state,
                      ts, past, n_sel):
    nb, n_pages = page_idx.shape
    pp = PAGES_PER_STEP
    wbuf = win_state.shape[1] // (2 * N_KV)
    jw = sel.shape[1]
    page = lambda k: pl.BlockSpec((PAGE * 4 * N_KV, NSA_HD), lambda b, s, pt: (pt[b, pp * s + k], 0))
    per_b = lambda r, w: pl.BlockSpec((None, r, w), lambda b, s, pt: (b, 0, 0))
    return pl.pallas_call(
        functools.partial(_nsa_sample_sweep_kernel, pp=pp, ts=ts, past=past, n_sel=n_sel, wbuf=wbuf),
        grid_spec=pltpu.PrefetchScalarGridSpec(
            num_scalar_prefetch=1,
            grid=(nb, n_pages // pp),
            in_specs=[page(k) for k in range(pp)] + [
                per_b(ts, 1024), per_b(ts, LANES),
                pl.BlockSpec((N_KV * ts, jw), lambda b, s, pt: (b, 0)),
                per_b(N_QH * ts, NSA_HD), per_b(ts * 4 * N_KV, NSA_HD), per_b(ts * 2 * N_KV, NSA_HD),
                per_b(wbuf * 2 * N_KV, NSA_HD)],
            out_specs=per_b(ts, 1024),
            scratch_shapes=[pltpu.VMEM((N_QH * ts, 1), F32), pltpu.VMEM((N_QH * ts, 1), F32),
                            pltpu.VMEM((N_QH * ts, NSA_HD), F32)]),
        out_shape=jax.ShapeDtypeStruct((nb, ts, 1024), F32),
        compiler_params=_cparams(("arbitrary", "arbitrary")),
    )(page_idx, *([pool] * pp), qn_s.reshape(nb, ts, 1024), gates_s.reshape(nb, ts, LANES), sel,
      o_cmp, rows_s.reshape(nb, ts * 4 * N_KV, NSA_HD), win_s.reshape(nb, ts * 2 * N_KV, NSA_HD),
      win_state)


def _softplus(z):
    return jnp.maximum(z, 0.0) + jnp.log1p(jnp.exp(-jnp.abs(z)))


def _rwkv_kernel(rkv_ref, aux_ref, sh_rkv_ref, sh_aux_ref, s0_ref, mu_rkv_ref, mu_aux_ref,
                 vec_ref, ww_ref, wa_ref, wg_ref, o_ref, sfin_ref, s_scr, c_rkv, c_aux, *, cs, n_valid):
    ci = pl.program_id(1)
    n_pairs = s_scr.shape[0]
    dr = n_pairs * LANES

    @pl.when(ci == 0)
    def _():
        s_scr[...] = s0_ref[...]
        c_rkv[...] = sh_rkv_ref[...]
        c_aux[...] = sh_aux_ref[...]

    def shift_mix(x, carry, mu):
        first = lax.broadcasted_iota(jnp.int32, x.shape, 0) == 0
        prev = jnp.where(first, carry, pltpu.roll(x, 1, 0))
        return x + (prev - x) * mu

    rkv = rkv_ref[...]
    aux = aux_ref[...]
    xm = shift_mix(rkv, c_rkv[...], mu_rkv_ref[...])
    xa = shift_mix(aux, c_aux[...], mu_aux_ref[...])
    c_rkv[...] = rkv[cs - 1:cs, :]
    c_aux[...] = aux[cs - 1:cs, :]

    w0, a0, k_k, k_a = vec_ref[0:1, :], vec_ref[1:2, :], vec_ref[2:3, :], vec_ref[3:4, :]
    r_k, ln_w, ln_b = vec_ref[4:5, :], vec_ref[5:6, :], vec_ref[6:7, :]
    r = xm[:, :dr]
    k = xm[:, dr:2 * dr]
    v = xm[:, 2 * dr:]
    u = w0 + _mm(jnp.tanh(xa).astype(BF16), ww_ref[...])
    lw = -jnp.exp(-_softplus(-u) - 0.5)
    a = jax.nn.sigmoid(a0 + _mm(xa.astype(BF16), wa_ref[...]))
    gate = _mm(jax.nn.sigmoid(xa).astype(BF16), wg_ref[...])

    lane = lax.broadcasted_iota(jnp.int32, (1, LANES), 1)
    head0 = lane < RWKV_HD
    rr = lax.broadcasted_iota(jnp.int32, (LANES, LANES), 0)
    cc = lax.broadcasted_iota(jnp.int32, (LANES, LANES), 1)
    seg = jnp.where((rr // RWKV_HD) == (cc // RWKV_HD), 1.0, 0.0).astype(BF16)
    eye = jnp.where(rr == cc, 1.0, 0.0)

    def seg_sum(x):
        return jnp.concatenate(
            [_mm_lhs_exact(x[:, p * LANES:(p + 1) * LANES], seg) for p in range(n_pairs)], axis=1)

    kk = k * k_k
    kkn = kk * lax.rsqrt(seg_sum(kk * kk) + 1e-12)
    kh = k * (1.0 + (a - 1.0) * k_a)
    bb = kkn * a
    bonus = seg_sum(r * kh * r_k) * v
    if n_valid < cs:
        live = lax.broadcasted_iota(jnp.int32, (cs, 1), 0) < n_valid
        lw = jnp.where(live, lw, 0.0)
        kh = jnp.where(live, kh, 0.0)
        kkn = jnp.where(live, kkn, 0.0)
        bb = jnp.where(live, bb, 0.0)
        v = jnp.where(live, v, 0.0)

    t_r = lax.broadcasted_iota(jnp.int32, (cs, cs), 0)
    t_c = lax.broadcasted_iota(jnp.int32, (cs, cs), 1)
    tri = jnp.where(t_r >= t_c, 1.0, 0.0).astype(BF16)
    cum = _mm_rhs_exact(tri, lw)
    tot = cum[cs - 1:cs, :]
    e_inc = jnp.exp(cum)
    e_inv = jnp.exp(-cum)
    e_rem = jnp.exp(tot - cum)
    q_t = r * e_inc
    a_t = -kkn * jnp.exp(cum - lw)
    k_t = kh * e_inv
    b_t = bb * e_inv
    k_hat = kh * e_rem
    b_hat = bb * e_rem
    w_tot = jnp.exp(tot)

    s2 = 2 * cs
    sr = lax.broadcasted_iota(jnp.int32, (s2, s2), 0) % cs
    sc = lax.broadcasted_iota(jnp.int32, (s2, s2), 1) % cs
    strict = sr > sc
    incl = sr >= sc
    eye2 = jnp.where(lax.broadcasted_iota(jnp.int32, (s2, s2), 0)
                     == lax.broadcasted_iota(jnp.int32, (s2, s2), 1), 1.0, 0.0)

    def stack(x):
        return jnp.concatenate([jnp.where(head0, x, 0.0), jnp.where(head0, 0.0, x)], axis=0)

    n_dbl = int(math.log2(cs)) - 1
    zeros_s = jnp.zeros((s2, LANES), F32)
    prs = range(n_pairs)
    lanes_of = lambda x: [stack(x[:, p * LANES:(p + 1) * LANES]) for p in prs]
    a_s, q_s, k_s, b_s = lanes_of(a_t), lanes_of(q_t), lanes_of(k_t), lanes_of(b_t)
    v_s, kh_s, bh_s = lanes_of(v), lanes_of(k_hat), lanes_of(b_hat)
    gram = [_mmp(jnp.concatenate([a_s[p], q_s[p]], axis=0), jnp.concatenate([b_s[p], k_s[p]], axis=0),
                 NT, RW_PASSES["gram"]) for p in prs]
    a_ab = [jnp.where(strict, gram[p][:s2, :s2], 0.0) for p in prs]
    a_ak = [jnp.where(strict, gram[p][:s2, s2:], 0.0) for p in prs]
    b_rbk = [jnp.concatenate([jnp.where(incl, gram[p][s2:, :s2], 0.0),
                              jnp.where(incl, gram[p][s2:, s2:], 0.0)], axis=1) for p in prs]
    akv = [_mmp(a_ak[p], v_s[p], passes=RW_PASSES["mix"]) for p in prs]
    tinv = [eye2 + a_ab[p] for p in prs]
    apow = a_ab
    for _ in range(n_dbl):
        apow = [_mmp(apow[p], apow[p], passes=RW_PASSES["inv"]) for p in prs]
        tinv = [tinv[p] + _mmp(apow[p], tinv[p], passes=RW_PASSES["inv"]) for p in prs]
    au = [_mmp(tinv[p], jnp.concatenate([a_s[p], akv[p]], axis=1), passes=RW_PASSES["mix"])
          for p in prs]
    ry = [_mmp(b_rbk[p], jnp.concatenate([au[p], jnp.concatenate([zeros_s, v_s[p]], axis=1)], axis=0),
               passes=RW_PASSES["mix"]) for p in prs]
    m_p = [eye * w_tot[:, p * LANES:(p + 1) * LANES]
           + _mmp(au[p][:, :LANES], bh_s[p], TN, RW_PASSES["trans"]) for p in prs]
    n_p = [_mmp(jnp.concatenate([au[p][:, LANES:], v_s[p]], axis=0),
                jnp.concatenate([bh_s[p], kh_s[p]], axis=0), TN, RW_PASSES["trans"]) for p in prs]
    s_old = [s_scr[p] for p in prs]
    y_s = [_mmp(q_s[p] + ry[p][:, :LANES], s_old[p], NT, RW_PASSES["out"]) + ry[p][:, LANES:]
           for p in prs]
    for p in prs:
        s_scr[p] = _mmp(s_old[p], m_p[p], passes=RW_PASSES["state"]) + n_p[p]
    y = jnp.concatenate([y_s[p][:cs] + y_s[p][cs:] for p in prs], axis=1)

    mean = seg_sum(y) * (1.0 / RWKV_HD)
    dy = y - mean
    var = seg_sum(dy * dy) * (1.0 / RWKV_HD)
    yn = dy * lax.rsqrt(var + GN_EPS) * ln_w + ln_b
    o_ref[...] = ((yn + bonus) * gate).astype(o_ref.dtype)

    @pl.when(ci == pl.num_programs(1) - 1)
    def _():
        sfin_ref[...] = s_scr[...]


def _rwkv(p, shift_rkv, shift_aux, s0_pairs, mu_rkv, mu_aux, vecs, ww, wa, wg, n_seq, t_len, cs, n_valid, out_dtype):
    n_chunks = t_len // cs
    n_pairs = s0_pairs.shape[1]
    dr = n_pairs * LANES
    full = lambda shape: pl.BlockSpec(shape, lambda b, c: (0,) * len(shape))
    return pl.pallas_call(
        functools.partial(_rwkv_kernel, cs=cs, n_valid=n_valid),
        grid=(n_seq, n_chunks),
        in_specs=[pl.BlockSpec((cs, 3 * dr), lambda b, c: (b * n_chunks + c, P_RKV // (3 * dr))),
                  pl.BlockSpec((cs, P_AUX_W), lambda b, c: (b * n_chunks + c, P_AUX // P_AUX_W)),
                  pl.BlockSpec((None, 1, 3 * dr), lambda b, c: (b, 0, 0)),
                  pl.BlockSpec((None, 1, P_AUX_W), lambda b, c: (b, 0, 0)),
                  pl.BlockSpec((None, n_pairs, LANES, LANES), lambda b, c: (b, 0, 0, 0)),
                  full(mu_rkv.shape), full(mu_aux.shape), full(vecs.shape),
                  full(ww.shape), full(wa.shape), full(wg.shape)],
        out_specs=[pl.BlockSpec((cs, dr), lambda b, c: (b * n_chunks + c, 0)),
                   pl.BlockSpec((None, n_pairs, LANES, LANES), lambda b, c: (b, 0, 0, 0))],
        out_shape=[jax.ShapeDtypeStruct((n_seq * t_len, dr), out_dtype),
                   jax.ShapeDtypeStruct((n_seq, n_pairs, LANES, LANES), F32)],
        scratch_shapes=[pltpu.VMEM((n_pairs, LANES, LANES), F32),
                        pltpu.VMEM((1, 3 * dr), F32), pltpu.VMEM((1, P_AUX_W), F32)],
        compiler_params=_cparams(("arbitrary", "arbitrary")),
    )(p, p, shift_rkv, shift_aux, s0_pairs, mu_rkv, mu_aux, vecs, ww, wa, wg)


def _out_kernel(x_ref, m_ref, on_ref, or_ref, wn_ref, wr_ref, o_ref):
    mix = _mm(on_ref[...], wn_ref[...]) + _mm(or_ref[...], wr_ref[...])
    o_ref[...] = x_ref[...] + m_ref[5] * mix


def _out_proj(x, mod, o_nsa, o_rw, w_top, w_bot, tm, tiles_per_seq):
    m_rows, d = x.shape
    dn = o_nsa.shape[1]
    mr = mod.shape[2]
    return pl.pallas_call(
        _out_kernel,
        grid=(m_rows // tm,),
        in_specs=[pl.BlockSpec((tm, d), lambda i: (i, 0)),
                  pl.BlockSpec((None, N_MOD, mr, d), lambda i: (i // tiles_per_seq, 0, 0, 0)),
                  pl.BlockSpec((tm, dn), lambda i: (i, 0)),
                  pl.BlockSpec((tm, dn), lambda i: (i, 0)),
                  pl.BlockSpec((dn, d), lambda i: (0, 0)),
                  pl.BlockSpec((dn, d), lambda i: (0, 0))],
        out_specs=pl.BlockSpec((tm, d), lambda i: (i, 0)),
        out_shape=jax.ShapeDtypeStruct((m_rows, d), F32),
        compiler_params=_cparams(("arbitrary",)),
    )(x, mod, o_nsa, o_rw, w_top, w_bot)


def _pad_cols(x, n):
    return jnp.pad(x, [(0, 0)] * (x.ndim - 1) + [(0, n)])


def _reorder_cols(w, d_rwkv, n_gate):
    nsa_main = P_AUX
    rw0 = nsa_main + n_gate
    lora0 = rw0 + 3 * d_rwkv
    n_lora = w.shape[-1] - lora0
    aux = jnp.concatenate([_pad_cols(w[..., nsa_main:rw0], AUX_WD - n_gate),
                           _pad_cols(w[..., lora0:], P_AUX_W - AUX_WD - n_lora)], axis=-1)
    return jnp.concatenate([w[..., :nsa_main], aux, w[..., rw0:lora0]], axis=-1)


def _pairs_from_heads(s):
    n, h = s.shape[:2]
    s = s.reshape(n, h // 2, 2, RWKV_HD, RWKV_HD)
    z = jnp.zeros_like(s[:, :, 0])
    top = jnp.concatenate([s[:, :, 0], z], axis=-1)
    bot = jnp.concatenate([z, s[:, :, 1]], axis=-1)
    return jnp.concatenate([top, bot], axis=-2)


def _heads_from_pairs(s):
    n, hp = s.shape[:2]
    a = s[:, :, :RWKV_HD, :RWKV_HD]
    b = s[:, :, RWKV_HD:, RWKV_HD:]
    return jnp.stack([a, b], axis=2).reshape(n, 2 * hp, RWKV_HD, RWKV_HD)


def kernel(x_prompt, x_sample, cache_nsa_kv, state_win_kv, state_wkv, state_shift, page_table,
           c_prompt, c_sample, w_ada, b_ada, norm_g, ffn_wi, ffn_wo, w_in, w_out,
           q_norm_g, k_norm_g, cmp_pe, cmp_w1, cmp_b1, cmp_w2,
           rwkv_mu, rwkv_w0, rwkv_w_w2, rwkv_a0, rwkv_w_a2, rwkv_w_g2,
           rwkv_k_k, rwkv_k_a, rwkv_r_k, rwkv_ln_w, rwkv_ln_b):
    bp, tp, d = x_prompt.shape
    bs, ts, _ = x_sample.shape
    depth = w_ada.shape[0]
    n_pool = cache_nsa_kv.shape[1]
    n_pages = page_table.shape[1]
    past = n_pages * PAGE
    wbuf = state_win_kv.shape[2]
    d_rwkv = rwkv_w0.shape[1]
    n_heads = d_rwkv // RWKV_HD
    d_nsa = N_QH * NSA_HD
    n_gate = 3 * N_QH
    n_dlora = rwkv_w_w2.shape[1]
    n_alora = rwkv_w_a2.shape[1]
    n_glora = rwkv_w_g2.shape[1]
    mp_rows, ms_rows = bp * tp, bs * ts
    tm = 512
    tiles_per_seq = tp // tm
    cs = 64
    ts_pad = -(-ts // cs) * cs
    win_keep = min(WINDOW, tp)

    c_all = jnp.concatenate([c_prompt, c_sample], axis=0)
    c_rows = -(-c_all.shape[0] // 8) * 8
    c_all = jnp.pad(c_all, ((0, c_rows - c_all.shape[0]), (0, 0)))
    mod = _modulation(c_all, w_ada, b_ada).reshape(depth, c_rows, N_MOD, d)
    mod_p = mod[:, :bp].reshape(depth, bp, N_MOD, 1, d)
    mod_s = jnp.repeat(mod[:, bp:bp + bs].transpose(0, 2, 1, 3), ts, axis=2).reshape(depth, 1, N_MOD, ms_rows, d)

    xp = x_prompt.reshape(mp_rows, d)
    xs = x_sample.reshape(ms_rows, d)
    pool = cache_nsa_kv.reshape(depth * n_pool * PAGE * 4 * N_KV, NSA_HD)
    prompt_pages = jnp.arange(bp * (tp // PAGE), dtype=jnp.int32).reshape(bp, tp // PAGE)
    n_sel_s = -(-(past + ts) // SEL_BLOCK)

    kv_p, kv_s, win_p, win_s, wkv_p, wkv_s, sh_p, sh_s = [], [], [], [], [], [], [], []
    for l in range(depth):
        wi = [ffn_wi[l, i].astype(BF16) for i in range(2)]
        wo = [ffn_wo[l, i].astype(BF16) for i in range(2)]
        w_in_l = _reorder_cols(w_in[l], d_rwkv, n_gate).astype(BF16)
        w_top = w_out[l, :d_nsa].astype(BF16)
        w_bot = w_out[l, d_nsa:].astype(BF16)
        g_rows = [norm_g[l, i].reshape(1, d) for i in range(3)]
        q_g = q_norm_g[l].reshape(1, NSA_HD)
        k_g = k_norm_g[l]
        w1 = cmp_w1[l].reshape(2, 2, CMP_STRIDE * NSA_HD, CMP_HID)
        w1cat = jnp.concatenate([w1[:, 0], w1[:, 1]], axis=-1).astype(BF16)
        pe2 = cmp_pe[l].reshape(2, 2, 1, CMP_STRIDE * NSA_HD)
        b1 = cmp_b1[l].reshape(2, 1, CMP_HID)
        w2 = cmp_w2[l].astype(BF16)
        def shift_parts(sh):
            aux = jnp.pad(sh[:, 3 * d_rwkv:], ((0, 0), (AUX_WD, P_AUX_W - AUX_WD - (sh.shape[1] - 3 * d_rwkv))))
            return sh[:, None, :3 * d_rwkv], aux[:, None, :]

        mu_rkv, mu_aux = (m[:, 0] for m in shift_parts(rwkv_mu[l].reshape(1, -1)))
        vecs = jnp.stack([rwkv_w0[l], rwkv_a0[l], rwkv_k_k[l], rwkv_k_a[l], rwkv_r_k[l].reshape(-1),
                          rwkv_ln_w[l], rwkv_ln_b[l], jnp.zeros_like(rwkv_w0[l])])
        lora_rows = lambda w, off: jnp.pad(w, ((off, P_AUX_W - off - w.shape[0]), (0, 0))).astype(BF16)
        ww = lora_rows(rwkv_w_w2[l], AUX_WD)
        wa = lora_rows(rwkv_w_a2[l], AUX_WD + n_dlora)
        wg = lora_rows(rwkv_w_g2[l], AUX_WD + n_dlora + n_alora)

        def shift_out(p_last):
            return jnp.concatenate([p_last[:, P_RKV:P_RKV + 3 * d_rwkv],
                                    p_last[:, P_AUX + AUX_WD:P_AUX + AUX_WD + n_dlora + n_alora + n_glora]],
                                   axis=1)

        xp = _ffn(xp, mod_p[l], g_rows[0], wi[0], wo[0], 0, tm, tiles_per_seq)
        xs = _ffn(xs, mod_s[l], g_rows[0], wi[0], wo[0], 0, ms_rows, 1)

        pp = _proj(xp, mod_p[l], g_rows[1], w_in_l, tm, tiles_per_seq)
        qn, rows, win, selbf, winbf, gates = _nsa_prep(pp, q_g, k_g, tm, BF16)
        kc, vc = _compress(rows, prompt_pages, w1cat, pe2, b1, w2, k_g)
        o_nsa = _nsa_prompt(qn, gates, selbf, winbf, kc, vc, bp, tp)
        zero_rkv, zero_aux = shift_parts(jnp.zeros((bp, state_shift.shape[2]), F32))
        o_rw, s_fin = _rwkv(pp, zero_rkv, zero_aux, jnp.zeros((bp, n_heads // 2, LANES, LANES), F32),
                            mu_rkv, mu_aux, vecs, ww, wa, wg, bp, tp, cs, cs, BF16)
        xp = _out_proj(xp, mod_p[l], o_nsa, o_rw, w_top, w_bot, tm, tiles_per_seq)
        kv_p.append(rows.reshape(bp, tp // PAGE, PAGE, 4, N_KV, NSA_HD))
        win_p.append(win.reshape(bp, tp, 2, N_KV, NSA_HD)[:, tp - win_keep:])
        wkv_p.append(_heads_from_pairs(s_fin))
        sh_p.append(shift_out(pp.reshape(bp, tp, P_COLS)[:, -1]))

        ps = _proj(xs, mod_s[l], g_rows[1], w_in_l, ms_rows, 1)
        qn_s, rows_s, win_new, _, _, gates_s = _nsa_prep(ps, q_g, k_g, ms_rows, F32)
        page_idx = page_table + l * n_pool
        kc_s, vc_s = _compress(pool, page_idx, w1cat, pe2, b1, w2, k_g)
        o_cmp, sel = _nsa_sample_select(qn_s, kc_s, vc_s, ts, past, n_sel_s)
        o_nsa_s = _nsa_sample_sweep(pool, page_idx, qn_s, gates_s, sel, o_cmp, rows_s, win_new,
                                    state_win_kv[l].reshape(bs, wbuf * 2 * N_KV, NSA_HD), ts, past, n_sel_s)
        sh_rkv, sh_aux = shift_parts(state_shift[l])
        ps_pad = jnp.pad(ps.reshape(bs, ts, P_COLS), ((0, 0), (0, ts_pad - ts), (0, 0)))
        o_rw_s, s_fin_s = _rwkv(ps_pad.reshape(bs * ts_pad, P_COLS), sh_rkv, sh_aux,
                                _pairs_from_heads(state_wkv[l]), mu_rkv, mu_aux, vecs, ww, wa, wg,
                                bs, ts_pad, ts_pad, ts, F32)
        o_rw_s = o_rw_s.reshape(bs, ts_pad, d_rwkv)[:, :ts].reshape(ms_rows, d_rwkv)
        xs = _out_proj(xs, mod_s[l], o_nsa_s.reshape(ms_rows, d_nsa).astype(BF16), o_rw_s.astype(BF16),
                       w_top, w_bot, ms_rows, 1)
        kv_s.append(rows_s.reshape(bs, ts, 4, N_KV, NSA_HD))
        win_s.append(jnp.concatenate([state_win_kv[l][:, ts:],
                                      win_new.reshape(bs, ts, 2, N_KV, NSA_HD)], axis=1))
        wkv_s.append(_heads_from_pairs(s_fin_s))
        sh_s.append(shift_out(ps.reshape(bs, ts, P_COLS)[:, -1]))

        xp = _ffn(xp, mod_p[l], g_rows[2], wi[1], wo[1], 2, tm, tiles_per_seq)
        xs = _ffn(xs, mod_s[l], g_rows[2], wi[1], wo[1], 2, ms_rows, 1)

    return (xp.reshape(bp, tp, d), xs.reshape(bs, ts, d),
            jnp.stack(kv_p), jnp.stack(kv_s), jnp.stack(win_p), jnp.stack(win_s),
            jnp.stack(wkv_p), jnp.stack(wkv_s), jnp.stack(sh_p), jnp.stack(sh_s))
```

```python
import functools
import math

import jax
import jax.numpy as jnp
from jax import lax
from jax.experimental import pallas as pl
from jax.experimental.pallas import tpu as pltpu

F32 = jnp.float32
BF16 = jnp.bfloat16

NSA_HD = 128
N_KV = 2
QPG = 4
N_QH = N_KV * QPG
CMP_LEN = 32
CMP_STRIDE = 16
CMP_HID = 2 * NSA_HD
SEL_BLOCK = 64
SEL_TOPK = 16
N_LOCAL = 2
WINDOW = 512
PAGE = 128
RWKV_HD = 64
N_MOD = 9
NORM_EPS = 1e-6
GN_EPS = 64e-5
NEG = -1e30
FORCED = 1e6
INVALID = -1e6

LANES = 128
VMEM_LIMIT = 56 * 1024 * 1024
PAGES_PER_STEP = 8

P_Q = 0
P_KV = 1024
P_AUX = 2560
P_AUX_W = 512
P_RKV = 3072
P_COLS = 6144
AUX_WD = 128
AUX_AD = 192
AUX_GD = 256


def _cparams(sem):
    return pltpu.CompilerParams(dimension_semantics=sem, vmem_limit_bytes=VMEM_LIMIT)


def _mm(a, b, dims=((1,), (0,))):
    return lax.dot_general(a, b, (dims, ((), ())), preferred_element_type=F32)


NT = ((1,), (1,))
TN = ((0,), (0,))


def _split2(x):
    hi = x.astype(BF16)
    lo = (x - hi.astype(F32)).astype(BF16)
    return hi, lo


def _split3(x):
    hi = x.astype(BF16)
    r1 = x - hi.astype(F32)
    mid = r1.astype(BF16)
    lo = (r1 - mid.astype(F32)).astype(BF16)
    return hi, mid, lo


def _mm3(a, b, dims=((1,), (0,))):
    ah, al = _split2(a)
    bh, bl = _split2(b)
    return _mm(ah, bh, dims) + (_mm(ah, bl, dims) + _mm(al, bh, dims))


def _mmp(a, b, dims=((1,), (0,)), passes=1):
    if passes == 3:
        return _mm3(a, b, dims)
    return _mm(a.astype(BF16), b.astype(BF16), dims)


RW_PASSES = {"gram": 1, "inv": 1, "mix": 1, "trans": 1, "out": 1, "state": 3}


def _mm_lhs_exact(a, b_bf16, dims=((1,), (0,))):
    a1, a2, a3 = _split3(a)
    return _mm(a1, b_bf16, dims) + (_mm(a2, b_bf16, dims) + _mm(a3, b_bf16, dims))


def _mm_rhs_exact(a_bf16, b, dims=((1,), (0,))):
    b1, b2, b3 = _split3(b)
    return _mm(a_bf16, b1, dims) + (_mm(a_bf16, b2, dims) + _mm(a_bf16, b3, dims))


def _silu(x):
    return x * jax.nn.sigmoid(x)


def _rms(x, g):
    return x * lax.rsqrt(jnp.mean(x * x, axis=-1, keepdims=True) + NORM_EPS) * g


def _ada_norm(x, m_ref, slot, g):
    return _rms(x, g) * (1.0 + m_ref[3 * slot + 1]) + m_ref[3 * slot]


def _mod_kernel(c_ref, w_ref, b_ref, o_ref):
    s = _silu(c_ref[...]).astype(BF16)
    o_ref[0] = _mm(s, w_ref[0].astype(BF16)) + b_ref[0]


def _modulation(c_all, w_ada, b_ada):
    depth, d, n = w_ada.shape
    rows = c_all.shape[0]
    tn = 1024
    return pl.pallas_call(
        _mod_kernel,
        grid=(depth, n // tn),
        in_specs=[pl.BlockSpec((rows, d), lambda l, j: (0, 0)),
                  pl.BlockSpec((1, d, tn), lambda l, j: (l, 0, j)),
                  pl.BlockSpec((1, 1, tn), lambda l, j: (l, 0, j))],
        out_specs=pl.BlockSpec((1, rows, tn), lambda l, j: (l, 0, j)),
        out_shape=jax.ShapeDtypeStruct((depth, rows, n), F32),
        compiler_params=_cparams(("arbitrary", "arbitrary")),
    )(c_all, w_ada, b_ada.reshape(depth, 1, n))


def _ffn_kernel(x_ref, m_ref, g_ref, wg_ref, wu_ref, wo_ref, o_ref, h_scr, acc_scr, *, slot):
    f = pl.program_id(1)

    @pl.when(f == 0)
    def _():
        h_scr[...] = _ada_norm(x_ref[...], m_ref, slot, g_ref[...]).astype(BF16)
        acc_scr[...] = jnp.zeros_like(acc_scr)

    h = h_scr[...]
    gate = _mm(h, wg_ref[...])
    up = _mm(h, wu_ref[...])
    act = (_silu(gate) * up).astype(BF16)
    acc_scr[...] += _mm(act, wo_ref[...])

    @pl.when(f == pl.num_programs(1) - 1)
    def _():
        o_ref[...] = x_ref[...] + 0.5 * m_ref[3 * slot + 2] * acc_scr[...]


def _ffn(x, mod, g, wi, wo, slot, tm, tiles_per_seq):
    m_rows, d = x.shape
    d_ff = wo.shape[0]
    tf = 512
    nf = d_ff // tf
    mr = mod.shape[2]
    return pl.pallas_call(
        functools.partial(_ffn_kernel, slot=slot),
        grid=(m_rows // tm, nf),
        in_specs=[pl.BlockSpec((tm, d), lambda i, f: (i, 0)),
                  pl.BlockSpec((None, N_MOD, mr, d), lambda i, f: (i // tiles_per_seq, 0, 0, 0)),
                  pl.BlockSpec((1, d), lambda i, f: (0, 0)),
                  pl.BlockSpec((d, tf), lambda i, f: (0, f)),
                  pl.BlockSpec((d, tf), lambda i, f: (0, nf + f)),
                  pl.BlockSpec((tf, d), lambda i, f: (f, 0))],
        out_specs=pl.BlockSpec((tm, d), lambda i, f: (i, 0)),
        out_shape=jax.ShapeDtypeStruct((m_rows, d), F32),
        scratch_shapes=[pltpu.VMEM((tm, d), BF16), pltpu.VMEM((tm, d), F32)],
        compiler_params=_cparams(("arbitrary", "arbitrary")),
    )(x, mod, g, wi, wi, wo)


def _proj_kernel(x_ref, m_ref, g_ref, w_ref, o_ref, h_scr):
    @pl.when(pl.program_id(1) == 0)
    def _():
        h_scr[...] = _ada_norm(x_ref[...], m_ref, 1, g_ref[...]).astype(BF16)

    o_ref[...] = _mm(h_scr[...], w_ref[...])


def _proj(x, mod, g, w, tm, tiles_per_seq):
    m_rows, d = x.shape
    n = w.shape[1]
    tn = 1536
    mr = mod.shape[2]
    return pl.pallas_call(
        _proj_kernel,
        grid=(m_rows // tm, n // tn),
        in_specs=[pl.BlockSpec((tm, d), lambda i, j: (i, 0)),
                  pl.BlockSpec((None, N_MOD, mr, d), lambda i, j: (i // tiles_per_seq, 0, 0, 0)),
                  pl.BlockSpec((1, d), lambda i, j: (0, 0)),
                  pl.BlockSpec((d, tn), lambda i, j: (0, j))],
        out_specs=pl.BlockSpec((tm, tn), lambda i, j: (i, j)),
        out_shape=jax.ShapeDtypeStruct((m_rows, n), F32),
        scratch_shapes=[pltpu.VMEM((tm, d), BF16)],
        compiler_params=_cparams(("arbitrary", "arbitrary")),
    )(x, mod, g, w)


def _rms_heads(x, g):
    outs = []
    for h in range(x.shape[1] // NSA_HD):
        outs.append(_rms(x[:, h * NSA_HD:(h + 1) * NSA_HD], g))
    return jnp.concatenate(outs, axis=1)


def _nsa_prep_kernel(q_ref, kva_ref, kvb_ref, kvc_ref, gt_ref, qg_ref, kg_ref,
                     qn_ref, rows_ref, win_ref, selbf_ref, winbf_ref, gates_ref):
    qn_ref[...] = (_rms_heads(q_ref[...], qg_ref[...]) * (NSA_HD ** -0.5)).astype(qn_ref.dtype)
    kvb = kvb_ref[...]
    ksel = _rms_heads(kvb[:, :2 * NSA_HD], kg_ref[1:2, :])
    selrows = jnp.concatenate([ksel, kvb[:, 2 * NSA_HD:]], axis=1)
    rows = jnp.concatenate([kva_ref[...], selrows], axis=1)
    tm = rows.shape[0]
    for c in range(4 * N_KV):
        rows_ref[pl.ds(c, tm, stride=4 * N_KV), :] = rows[:, c * NSA_HD:(c + 1) * NSA_HD]
    selbf_ref[...] = selrows.astype(selbf_ref.dtype)
    kvc = kvc_ref[...]
    kwin = _rms_heads(kvc[:, :2 * NSA_HD], kg_ref[2:3, :])
    winrows = jnp.concatenate([kwin, kvc[:, 2 * NSA_HD:]], axis=1)
    for c in range(2 * N_KV):
        win_ref[pl.ds(c, tm, stride=2 * N_KV), :] = winrows[:, c * NSA_HD:(c + 1) * NSA_HD]
    winbf_ref[...] = winrows.astype(winbf_ref.dtype)
    gates_ref[...] = jax.nn.sigmoid(gt_ref[...])


def _nsa_prep(p, q_g, k_g, tm, act_dtype):
    m_rows = p.shape[0]
    row = lambda w, j: pl.BlockSpec((tm, w), lambda i: (i, j))
    return pl.pallas_call(
        _nsa_prep_kernel,
        grid=(m_rows // tm,),
        in_specs=[row(1024, 0), row(512, 2), row(512, 3), row(512, 4), row(LANES, P_AUX // LANES),
                  pl.BlockSpec((1, NSA_HD), lambda i: (0, 0)),
                  pl.BlockSpec((3, NSA_HD), lambda i: (0, 0))],
        out_specs=[row(1024, 0), pl.BlockSpec((tm * 8, NSA_HD), lambda i: (i, 0)),
                   pl.BlockSpec((tm * 4, NSA_HD), lambda i: (i, 0)),
                   row(512, 0), row(512, 0), row(LANES, 0)],
        out_shape=[jax.ShapeDtypeStruct((m_rows, 1024), act_dtype),
                   jax.ShapeDtypeStruct((m_rows * 8, NSA_HD), F32),
                   jax.ShapeDtypeStruct((m_rows * 4, NSA_HD), F32),
                   jax.ShapeDtypeStruct((m_rows, 512), act_dtype),
                   jax.ShapeDtypeStruct((m_rows, 512), act_dtype),
                   jax.ShapeDtypeStruct((m_rows, LANES), F32)],
        compiler_params=_cparams(("arbitrary",)),
    )(p, p, p, p, p, q_g, k_g)


def _cmp_kernel(pt_ref, *refs, n_sub, pp):
    pages = refs[:pp]
    w1_ref, pe_ref, b1_ref, w2_ref, kg_ref, kc_ref, vc_ref, x_scr = refs[pp:]
    p = pl.program_id(1)
    spp = PAGE // CMP_STRIDE
    base = pl.multiple_of(p * (pp * spp), pp * spp)
    heads = 4 * N_KV
    for c in range(2 * N_KV):
        for l in range(CMP_STRIDE):
            pieces = [pg[pl.ds(l * heads + c, spp, stride=CMP_STRIDE * heads), :] for pg in pages]
            x_scr[c, pl.ds(base, pp * spp), l * NSA_HD:(l + 1) * NSA_HD] = (
                jnp.concatenate(pieces, axis=0).astype(BF16))

    @pl.when(p == pl.num_programs(1) - 1)
    def _():
        row = lax.broadcasted_iota(jnp.int32, (n_sub, NSA_HD), 0)
        for kv in range(2):
            w1 = w1_ref[kv]
            const = b1_ref[kv]
            for j in range(2):
                pe = jnp.broadcast_to(pe_ref[kv, j], (8, CMP_STRIDE * NSA_HD)).astype(BF16)
                const = const + _mm(pe, w1[:, j * CMP_HID:(j + 1) * CMP_HID])[0:1]
            for g in range(N_KV):
                ab = _mm(x_scr[kv * 2 + g], w1)
                nxt = pltpu.roll(ab[:, CMP_HID:], n_sub - 1, 0)
                acc = ab[:, :CMP_HID] + nxt + const
                o = _mm(jax.nn.gelu(acc).astype(BF16), w2_ref[kv])
                if kv == 0:
                    o = _rms(o, kg_ref[0:1, :])
                o = jnp.where(row < n_sub - 1, o, 0.0)
                if kv == 0:
                    kc_ref[g] = o
                else:
                    vc_ref[g] = o


def _compress(pool, page_idx, w1cat, pe2, b1, w2, k_g):
    nb, n_pages = page_idx.shape
    n_sub = n_pages * (PAGE // CMP_STRIDE)
    pp = PAGES_PER_STEP
    page = lambda k: pl.BlockSpec((PAGE * 4 * N_KV, NSA_HD), lambda b, p, pt: (pt[b, pp * p + k], 0))
    page_specs = [page(k) for k in range(pp)]
    full = lambda shape: pl.BlockSpec(shape, lambda b, p, pt: (0,) * len(shape))
    out = pl.BlockSpec((None, N_KV, n_sub, NSA_HD), lambda b, p, pt: (b, 0, 0, 0))
    return pl.pallas_call(
        functools.partial(_cmp_kernel, n_sub=n_sub, pp=pp),
        grid_spec=pltpu.PrefetchScalarGridSpec(
            num_scalar_prefetch=1,
            grid=(nb, n_pages // pp),
            in_specs=page_specs + [full(w1cat.shape), full(pe2.shape), full(b1.shape),
                      full(w2.shape), full(k_g.shape)],
            out_specs=[out, out],
            scratch_shapes=[pltpu.VMEM((4, n_sub, CMP_STRIDE * NSA_HD), BF16)]),
        out_shape=[jax.ShapeDtypeStruct((nb, N_KV, n_sub, NSA_HD), F32)] * 2,
        compiler_params=_cparams(("arbitrary", "arbitrary")),
    )(page_idx, *([pool] * pp), w1cat, pe2, b1, w2, k_g)


def _imp_matrix(n_sub, n_cmp, width):
    n = lax.broadcasted_iota(jnp.int32, (n_sub, width), 0)
    j = lax.broadcasted_iota(jnp.int32, (n_sub, width), 1)
    spb = SEL_BLOCK // CMP_STRIDE
    m = jnp.where(n // spb == j, 1.0, 0.0) + jnp.where((n + 1) // spb == j, 1.0, 0.0)
    return jnp.where(n < n_cmp, m, 0.0).astype(BF16)


def _topk_mask_t(st_scr, n_iter, width):
    jj = st_scr.shape[0]
    st = st_scr[...]
    jrow = lax.broadcasted_iota(jnp.int32, (jj, width), 0)

    def body(jp, rank):
        r = st_scr[pl.ds(jp, 1), :]
        beats = (r > st) | ((r == st) & (jp < jrow))
        return rank + jnp.where(beats, 1.0, 0.0)

    return lax.fori_loop(0, n_iter, body, jnp.zeros((jj, width), F32))


def _softmax_rows(s, ok):
    sm = jnp.where(ok, s, NEG)
    mx = jnp.max(sm, axis=-1, keepdims=True)
    e = jnp.where(ok, jnp.exp(sm - mx), 0.0)
    den = jnp.sum(e, axis=-1, keepdims=True)
    return e / jnp.where(den > 0.0, den, 1.0)


def _nsa_prompt_kernel(q_ref, gt_ref, sel_ref, win_ref, kc_ref, vc_ref, oh_ref, o_ref, st_scr,
                       *, t_len, n_sub, ck):
    tq = SEL_BLOCK
    qi = pl.program_id(1)
    s0 = qi * tq
    n_cmp = n_sub - 1
    rows = QPG * tq
    q = q_ref[...]
    rowpos = s0 + lax.broadcasted_iota(jnp.int32, (rows, 1), 0) % tq

    def q_group(g):
        return jnp.concatenate(
            [q[:, (g * QPG + h) * NSA_HD:(g * QPG + h + 1) * NSA_HD] for h in range(QPG)], axis=0)

    imp_m = _imp_matrix(n_sub, n_cmp, LANES)
    n_idx = lax.broadcasted_iota(jnp.int32, (rows, n_sub), 1)
    ok_c = (n_idx * CMP_STRIDE + (CMP_LEN - 1) <= rowpos) & (n_idx < n_cmp)
    o_cmp, scores = [], []
    jl = lax.broadcasted_iota(jnp.int32, (tq, LANES), 1)
    forced = (jl == 0) | ((jl <= qi) & (jl > qi - N_LOCAL))
    for g in range(N_KV):
        qg = q_group(g)
        s_c = _mm(qg, kc_ref[g].astype(BF16), NT)
        p_c = _softmax_rows(s_c, ok_c)
        o_cmp.append(_mm(p_c.astype(BF16), vc_ref[g].astype(BF16)))
        p_g = p_c[0:tq] + p_c[tq:2 * tq] + p_c[2 * tq:3 * tq] + p_c[3 * tq:4 * tq]
        imp = _mm_lhs_exact(p_g, imp_m)
        scores.append(jnp.where(jl <= qi, jnp.where(forced, FORCED, imp), INVALID))
    jj = st_scr.shape[0]
    st_scr[...] = jnp.concatenate(scores, axis=0).T[:jj]
    rank = _topk_mask_t(st_scr, qi + 1, N_KV * tq)
    jrow = lax.broadcasted_iota(jnp.int32, (jj, N_KV * tq), 0)
    sel_t = jnp.where((rank < SEL_TOPK) & (jrow <= qi), 1.0, 0.0)
    if jj < LANES:
        sel_t = jnp.concatenate([sel_t, jnp.zeros((LANES - jj, N_KV * tq), F32)], axis=0)
    sel = sel_t.T

    qgs = [q_group(g) for g in range(N_KV)]
    q_aug = []
    for g in range(N_KV):
        sg = jnp.concatenate([sel[g * tq:(g + 1) * tq]] * QPG, axis=0)
        q_aug.append(jnp.concatenate([qgs[g], ((sg - 1.0) * (-NEG)).astype(BF16)], axis=1))
    tokpos = s0 + lax.broadcasted_iota(jnp.int32, (1, rows), 1) % tq

    def chunks(cs, causal):
        scores_t, vals = [], []
        for c, cz in zip(cs, causal):
            k0 = pl.multiple_of(c * ck, ck)
            keypos = k0 + lax.broadcasted_iota(jnp.int32, (ck, 1), 0)
            onehot = oh_ref[pl.ds(k0, ck), :]
            for g in range(N_KV):
                kk = sel_ref[pl.ds(k0, ck), g * NSA_HD:(g + 1) * NSA_HD]
                s = _mm(jnp.concatenate([kk, onehot], axis=1), q_aug[g], NT)
                scores_t.append(jnp.where(keypos <= tokpos, s, NEG) if cz else s)
                vals.append(sel_ref[pl.ds(k0, ck), (N_KV + g) * NSA_HD:(N_KV + g + 1) * NSA_HD])
        maxes = [jnp.max(s, axis=0, keepdims=True) for s in scores_t]
        probs = [jnp.exp(s - m) for s, m in zip(scores_t, maxes)]
        sums = [jnp.sum(p, axis=0, keepdims=True) for p in probs]
        accs = [_mm(v, p.astype(BF16), TN) for v, p in zip(vals, probs)]
        parts = list(zip(maxes, sums, accs))
        return [tuple(parts[i * N_KV:(i + 1) * N_KV]) for i in range(len(cs))]

    def merge(carry, *chunks):
        new = []
        for g in range(N_KV):
            m_i, l_i, acc = carry[g]
            m_new = m_i
            for ch in chunks:
                m_new = jnp.maximum(m_new, ch[g][0])
            w_i = jnp.exp(m_i - m_new)
            l_new, acc_new = w_i * l_i, w_i * acc
            for ch in chunks:
                w_c = jnp.exp(ch[g][0] - m_new)
                l_new = l_new + w_c * ch[g][1]
                acc_new = acc_new + w_c * ch[g][2]
            new.append((m_new, l_new, acc_new))
        return tuple(new)

    neutral = tuple((jnp.full((1, rows), NEG, F32), jnp.zeros((1, rows), F32),
                     jnp.zeros((NSA_HD, rows), F32)) for _ in range(N_KV))
    c_diag = s0 // ck
    n_pairs = c_diag // 2
    mid = lax.fori_loop(
        0, n_pairs, lambda i, carry: merge(carry, *chunks([2 * i, 2 * i + 1], [False, False])), neutral)
    fin = lax.cond(c_diag % 2 == 1,
                   lambda: merge(mid, *chunks([c_diag - 1, c_diag], [False, True])),
                   lambda: merge(mid, *chunks([c_diag], [True])))

    wlen = WINDOW + 2 * tq
    w0 = pl.multiple_of(jnp.clip(s0 - WINDOW, 0, t_len - wlen), tq)
    wpos = w0 + lax.broadcasted_iota(jnp.int32, (wlen, 1), 0)
    dist = tokpos - wpos
    bias_w = jnp.where((dist >= 0) & (dist < WINDOW), 0.0, NEG)
    s_ws = [_mm(win_ref[pl.ds(w0, wlen), g * NSA_HD:(g + 1) * NSA_HD], qgs[g], NT) + bias_w
            for g in range(N_KV)]
    e_ws = [jnp.exp(s - jnp.max(s, axis=0, keepdims=True)) for s in s_ws]
    o_ws = [(_mm(win_ref[pl.ds(w0, wlen), (N_KV + g) * NSA_HD:(N_KV + g + 1) * NSA_HD],
                 e_ws[g].astype(BF16), TN) / jnp.sum(e_ws[g], axis=0, keepdims=True)).T
            for g in range(N_KV)]
    gt = gt_ref[...]
    outs = []
    for g in range(N_KV):
        o_w = o_ws[g]
        _, l_i, acc = fin[g]
        o_s = (acc / l_i).T
        for h in range(QPG):
            c0 = (g * QPG + h) * 3
            r0 = slice(h * tq, (h + 1) * tq)
            outs.append(gt[:, c0:c0 + 1] * o_cmp[g][r0] + gt[:, c0 + 1:c0 + 2] * o_s[r0]
                        + gt[:, c0 + 2:c0 + 3] * o_w[r0])
    o_ref[...] = jnp.concatenate(outs, axis=1).astype(o_ref.dtype)


def _nsa_prompt(qn, gates, selbf, winbf, kc, vc, nb, t_len):
    n_sub = kc.shape[2]
    tq = SEL_BLOCK
    nq = t_len // tq
    ck = 512
    n_blocks = -(-(t_len // SEL_BLOCK) // 8) * 8
    assert n_blocks <= LANES
    block_onehot = (jnp.arange(t_len)[:, None] // SEL_BLOCK == jnp.arange(LANES)[None, :]).astype(BF16)
    per_b = lambda w: pl.BlockSpec((None, t_len, w), lambda b, i: (b, 0, 0))
    cmp_spec = pl.BlockSpec((None, N_KV, n_sub, NSA_HD), lambda b, i: (b, 0, 0, 0))
    return pl.pallas_call(
        functools.partial(_nsa_prompt_kernel, t_len=t_len, n_sub=n_sub, ck=ck),
        grid=(nb, nq),
        in_specs=[pl.BlockSpec((tq, 1024), lambda b, i: (b * nq + i, 0)),
                  pl.BlockSpec((tq, LANES), lambda b, i: (b * nq + i, 0)),
                  per_b(512), per_b(512), cmp_spec, cmp_spec,
                  pl.BlockSpec((t_len, LANES), lambda b, i: (0, 0))],
        out_specs=pl.BlockSpec((tq, 1024), lambda b, i: (b * nq + i, 0)),
        out_shape=jax.ShapeDtypeStruct((nb * t_len, 1024), BF16),
        scratch_shapes=[pltpu.VMEM((n_blocks, N_KV * tq), F32)],
        compiler_params=_cparams(("arbitrary", "arbitrary")),
    )(qn, gates, selbf.reshape(nb, t_len, 512), winbf.reshape(nb, t_len, 512), kc, vc, block_onehot)


def _nsa_sample_select_kernel(q_ref, kc_ref, vc_ref, oc_ref, sel_ref, sc_scr, st_scr,
                              *, n_sub, n_sel, ts, past, jw):
    b = pl.program_id(0)
    nb = pl.num_programs(0)
    n_cmp = n_sub - 1
    q = q_ref[...]
    imp_m = _imp_matrix(n_sub, n_cmp, jw)
    rows = QPG * ts
    rowpos = past + lax.broadcasted_iota(jnp.int32, (rows, 1), 0) % ts
    n_idx = lax.broadcasted_iota(jnp.int32, (rows, n_sub), 1)
    ok_c = (n_idx * CMP_STRIDE + (CMP_LEN - 1) <= rowpos) & (n_idx < n_cmp)
    jl = lax.broadcasted_iota(jnp.int32, (ts, jw), 1)
    blk = (past + lax.broadcasted_iota(jnp.int32, (ts, 1), 0)) // SEL_BLOCK
    forced = (jl == 0) | ((jl <= blk) & (jl > blk - N_LOCAL))

    @pl.when(b == 0)
    def _():
        sc_scr[...] = jnp.full(sc_scr.shape, INVALID, F32)

    for g in range(N_KV):
        qg = jnp.concatenate(
            [q[:, (g * QPG + h) * NSA_HD:(g * QPG + h + 1) * NSA_HD] for h in range(QPG)],
            axis=0).astype(BF16)
        s_c = _mm(qg, kc_ref[g].astype(BF16), NT)
        p_c = _softmax_rows(s_c, ok_c)
        oc_ref[g * rows:(g + 1) * rows, :] = _mm(p_c.astype(BF16), vc_ref[g].astype(BF16))
        p_g = p_c[0:ts] + p_c[ts:2 * ts] + p_c[2 * ts:3 * ts] + p_c[3 * ts:4 * ts]
        imp = _mm_lhs_exact(p_g, imp_m)
        score = jnp.where((jl <= blk) & (jl < n_sel), jnp.where(forced, FORCED, imp), INVALID)
        sc_scr[pl.ds(pl.multiple_of((b * N_KV + g) * ts, ts), ts), :] = score

    @pl.when(b == nb - 1)
    def _():
        st_scr[...] = sc_scr[...].T
        rank = _topk_mask_t(st_scr, n_sel, LANES)
        sel_t = jnp.where((rank < SEL_TOPK) & (st_scr[...] > 0.5 * INVALID), 1.0, 0.0)
        sel_ref[...] = sel_t.T


def _nsa_sample_select(qn_s, kc, vc, ts, past, n_sel):
    nb = kc.shape[0]
    n_sub = kc.shape[2]
    jw = -(-n_sel // LANES) * LANES
    assert nb * N_KV * ts <= LANES
    cmp_spec = pl.BlockSpec((None, N_KV, n_sub, NSA_HD), lambda b: (b, 0, 0, 0))
    return pl.pallas_call(
        functools.partial(_nsa_sample_select_kernel, n_sub=n_sub, n_sel=n_sel, ts=ts, past=past, jw=jw),
        grid=(nb,),
        in_specs=[pl.BlockSpec((None, ts, 1024), lambda b: (b, 0, 0)), cmp_spec, cmp_spec],
        out_specs=[pl.BlockSpec((None, N_QH * ts, NSA_HD), lambda b: (b, 0, 0)),
                   pl.BlockSpec((LANES, jw), lambda b: (0, 0))],
        out_shape=[jax.ShapeDtypeStruct((nb, N_QH * ts, NSA_HD), F32),
                   jax.ShapeDtypeStruct((LANES, jw), F32)],
        scratch_shapes=[pltpu.VMEM((LANES, jw), F32), pltpu.VMEM((jw, LANES), F32)],
        compiler_params=_cparams(("arbitrary",)),
    )(qn_s.reshape(nb, ts, 1024), kc, vc)


def _nsa_sample_sweep_kernel(pt_ref, *refs, pp, ts, past, n_sel, wbuf):
    pages = refs[:pp]
    (q_ref, gt_ref, sel_ref, oc_ref, rows_ref, wnew_ref, wst_ref, o_ref,
     m_scr, l_scr, acc_scr) = refs[pp:]
    step = pl.program_id(1)
    rows = QPG * ts
    heads = 4 * N_KV
    jw = sel_ref.shape[1]
    q = q_ref[...]
    qgs = [jnp.concatenate(
        [q[:, (g * QPG + h) * NSA_HD:(g * QPG + h + 1) * NSA_HD] for h in range(QPG)],
        axis=0).astype(BF16) for g in range(N_KV)]
    sel = sel_ref[...].astype(BF16)

    @pl.when(step == 0)
    def _():
        m_scr[...] = jnp.full(m_scr.shape, NEG, F32)
        l_scr[...] = jnp.zeros_like(l_scr)
        acc_scr[...] = jnp.zeros_like(acc_scr)

    def update(g, s, ok, vv):
        r0 = slice(g * rows, (g + 1) * rows)
        s = jnp.where(ok, s, NEG)
        m_i = m_scr[r0]
        m_new = jnp.maximum(m_i, jnp.max(s, axis=-1, keepdims=True))
        pr = jnp.where(ok, jnp.exp(s - m_new), 0.0)
        alpha = jnp.exp(m_i - m_new)
        l_scr[r0] = alpha * l_scr[r0] + jnp.sum(pr, axis=-1, keepdims=True)
        acc_scr[r0] = alpha * acc_scr[r0] + _mm(pr.astype(BF16), vv)
        m_scr[r0] = m_new

    nk = pp * PAGE
    ej = lax.broadcasted_iota(jnp.int32, (jw, nk), 0)
    kl = lax.broadcasted_iota(jnp.int32, (1, nk), 1)
    expand = jnp.where(ej == (step * nk + kl) // SEL_BLOCK, 1.0, 0.0).astype(BF16)
    mask2 = _mm(sel, expand)
    for g in range(N_KV):
        kk = jnp.concatenate([pg[pl.ds(2 * N_KV + g, PAGE, stride=heads), :] for pg in pages],
                             axis=0).astype(BF16)
        vv = jnp.concatenate([pg[pl.ds(3 * N_KV + g, PAGE, stride=heads), :] for pg in pages],
                             axis=0).astype(BF16)
        mg = mask2[g * ts:(g + 1) * ts]
        ok = jnp.concatenate([mg] * QPG, axis=0) > 0.5
        update(g, _mm(qgs[g], kk, NT), ok, vv)

    @pl.when(step == pl.num_programs(1) - 1)
    def _():
        tpos = lax.broadcasted_iota(jnp.int32, (rows, 1), 0) % ts
        pad = jnp.zeros((PAGE - ts, NSA_HD), F32)
        il = lax.broadcasted_iota(jnp.int32, (1, PAGE), 1)
        last_sel = sel_ref[:, n_sel - 1:n_sel]
        gt = gt_ref[...]
        wlen = wbuf + PAGE
        wl = lax.broadcasted_iota(jnp.int32, (1, wlen), 1)
        dist = (past + tpos) - (past - wbuf + wl)
        ok_w = (dist >= 0) & (dist < WINDOW) & (wl < wbuf + ts)
        outs = []
        for g in range(N_KV):
            kn = jnp.concatenate([rows_ref[pl.ds(2 * N_KV + g, ts, stride=heads), :], pad], axis=0)
            vn = jnp.concatenate([rows_ref[pl.ds(3 * N_KV + g, ts, stride=heads), :], pad], axis=0)
            lsel = jnp.concatenate([last_sel[g * ts:(g + 1) * ts]] * QPG, axis=0) > 0.5
            ok = (il <= tpos) & (il < ts) & lsel
            update(g, _mm(qgs[g], kn.astype(BF16), NT), ok, vn.astype(BF16))
            r0 = slice(g * rows, (g + 1) * rows)
            o_s = acc_scr[r0] / l_scr[r0]
            kw = jnp.concatenate([wst_ref[pl.ds(g, wbuf, stride=2 * N_KV), :],
                                  wnew_ref[pl.ds(g, ts, stride=2 * N_KV), :], pad], axis=0).astype(BF16)
            vw = jnp.concatenate([wst_ref[pl.ds(N_KV + g, wbuf, stride=2 * N_KV), :],
                                  wnew_ref[pl.ds(N_KV + g, ts, stride=2 * N_KV), :], pad],
                                 axis=0).astype(BF16)
            p_w = _softmax_rows(_mm(qgs[g], kw, NT), ok_w)
            o_w = _mm(p_w.astype(BF16), vw)
            o_c = oc_ref[r0, :]
            for h in range(QPG):
                c0 = (g * QPG + h) * 3
                rh = slice(h * ts, (h + 1) * ts)
                outs.append(gt[:, c0:c0 + 1] * o_c[rh] + gt[:, c0 + 1:c0 + 2] * o_s[rh]
                            + gt[:, c0 + 2:c0 + 3] * o_w[rh])
        o_ref[...] = jnp.concatenate(outs, axis=1)


def _nsa_sample_sweep(pool, page_idx, qn_s, gates_s, sel, o_cmp, rows_s, win_s, win_state,
                      ts, past, n_sel):
    nb, n_pages = page_idx.shape
    pp = PAGES_PER_STEP
    wbuf = win_state.shape[1] // (2 * N_KV)
    jw = sel.shape[1]
    page = lambda k: pl.BlockSpec((PAGE * 4 * N_KV, NSA_HD), lambda b, s, pt: (pt[b, pp * s + k], 0))
    per_b = lambda r, w: pl.BlockSpec((None, r, w), lambda b, s, pt: (b, 0, 0))
    return pl.pallas_call(
        functools.partial(_nsa_sample_sweep_kernel, pp=pp, ts=ts, past=past, n_sel=n_sel, wbuf=wbuf),
        grid_spec=pltpu.PrefetchScalarGridSpec(
            num_scalar_prefetch=1,
            grid=(nb, n_pages // pp),
            in_specs=[page(k) for k in range(pp)] + [
                per_b(ts, 1024), per_b(ts, LANES),
                pl.BlockSpec((N_KV * ts, jw), lambda b, s, pt: (b, 0)),
                per_b(N_QH * ts, NSA_HD), per_b(ts * 4 * N_KV, NSA_HD), per_b(ts * 2 * N_KV, NSA_HD),
                per_b(wbuf * 2 * N_KV, NSA_HD)],
            out_specs=per_b(ts, 1024),
            scratch_shapes=[pltpu.VMEM((N_QH * ts, 1), F32), pltpu.VMEM((N_QH * ts, 1), F32),
                            pltpu.VMEM((N_QH * ts, NSA_HD), F32)]),
        out_shape=jax.ShapeDtypeStruct((nb, ts, 1024), F32),
        compiler_params=_cparams(("arbitrary", "arbitrary")),
    )(page_idx, *([pool] * pp), qn_s.reshape(nb, ts, 1024), gates_s.reshape(nb, ts, LANES), sel,
      o_cmp, rows_s.reshape(nb, ts * 4 * N_KV, NSA_HD), win_s.reshape(nb, ts * 2 * N_KV, NSA_HD),
      win_state)


def _softplus(z):
    return jnp.maximum(z, 0.0) + jnp.log1p(jnp.exp(-jnp.abs(z)))


def _rwkv_kernel(rkv_ref, aux_ref, sh_rkv_ref, sh_aux_ref, s0_ref, mu_rkv_ref, mu_aux_ref,
                 vec_ref, ww_ref, wa_ref, wg_ref, o_ref, sfin_ref, s_scr, c_rkv, c_aux, *, cs, n_valid):
    ci = pl.program_id(1)
    n_pairs = s_scr.shape[0]
    dr = n_pairs * LANES

    @pl.when(ci == 0)
    def _():
        s_scr[...] = s0_ref[...]
        c_rkv[...] = sh_rkv_ref[...]
        c_aux[...] = sh_aux_ref[...]

    def shift_mix(x, carry, mu):
        first = lax.broadcasted_iota(jnp.int32, x.shape, 0) == 0
        prev = jnp.where(first, carry, pltpu.roll(x, 1, 0))
        return x + (prev - x) * mu

    rkv = rkv_ref[...]
    aux = aux_ref[...]
    xm = shift_mix(rkv, c_rkv[...], mu_rkv_ref[...])
    xa = shift_mix(aux, c_aux[...], mu_aux_ref[...])
    c_rkv[...] = rkv[cs - 1:cs, :]
    c_aux[...] = aux[cs - 1:cs, :]

    w0, a0, k_k, k_a = vec_ref[0:1, :], vec_ref[1:2, :], vec_ref[2:3, :], vec_ref[3:4, :]
    r_k, ln_w, ln_b = vec_ref[4:5, :], vec_ref[5:6, :], vec_ref[6:7, :]
    r = xm[:, :dr]
    k = xm[:, dr:2 * dr]
    v = xm[:, 2 * dr:]
    u = w0 + _mm(jnp.tanh(xa).astype(BF16), ww_ref[...])
    lw = -jnp.exp(-_softplus(-u) - 0.5)
    a = jax.nn.sigmoid(a0 + _mm(xa.astype(BF16), wa_ref[...]))
    gate = _mm(jax.nn.sigmoid(xa).astype(BF16), wg_ref[...])

    lane = lax.broadcasted_iota(jnp.int32, (1, LANES), 1)
    head0 = lane < RWKV_HD
    rr = lax.broadcasted_iota(jnp.int32, (LANES, LANES), 0)
    cc = lax.broadcasted_iota(jnp.int32, (LANES, LANES), 1)
    seg = jnp.where((rr // RWKV_HD) == (cc // RWKV_HD), 1.0, 0.0).astype(BF16)
    eye = jnp.where(rr == cc, 1.0, 0.0)

    def seg_sum(x):
        return jnp.concatenate(
            [_mm_lhs_exact(x[:, p * LANES:(p + 1) * LANES], seg) for p in range(n_pairs)], axis=1)

    kk = k * k_k
    kkn = kk * lax.rsqrt(seg_sum(kk * kk) + 1e-12)
    kh = k * (1.0 + (a - 1.0) * k_a)
    bb = kkn * a
    bonus = seg_sum(r * kh * r_k) * v
    if n_valid < cs:
        live = lax.broadcasted_iota(jnp.int32, (cs, 1), 0) < n_valid
        lw = jnp.where(live, lw, 0.0)
        kh = jnp.where(live, kh, 0.0)
        kkn = jnp.where(live, kkn, 0.0)
        bb = jnp.where(live, bb, 0.0)
        v = jnp.where(live, v, 0.0)

    t_r = lax.broadcasted_iota(jnp.int32, (cs, cs), 0)
    t_c = lax.broadcasted_iota(jnp.int32, (cs, cs), 1)
    tri = jnp.where(t_r >= t_c, 1.0, 0.0).astype(BF16)
    cum = _mm_rhs_exact(tri, lw)
    tot = cum[cs - 1:cs, :]
    e_inc = jnp.exp(cum)
    e_inv = jnp.exp(-cum)
    e_rem = jnp.exp(tot - cum)
    q_t = r * e_inc
    a_t = -kkn * jnp.exp(cum - lw)
    k_t = kh * e_inv
    b_t = bb * e_inv
    k_hat = kh * e_rem
    b_hat = bb * e_rem
    w_tot = jnp.exp(tot)

    s2 = 2 * cs
    sr = lax.broadcasted_iota(jnp.int32, (s2, s2), 0) % cs
    sc = lax.broadcasted_iota(jnp.int32, (s2, s2), 1) % cs
    strict = sr > sc
    incl = sr >= sc
    eye2 = jnp.where(lax.broadcasted_iota(jnp.int32, (s2, s2), 0)
                     == lax.broadcasted_iota(jnp.int32, (s2, s2), 1), 1.0, 0.0)

    def stack(x):
        return jnp.concatenate([jnp.where(head0, x, 0.0), jnp.where(head0, 0.0, x)], axis=0)

    n_dbl = int(math.log2(cs)) - 1
    zeros_s = jnp.zeros((s2, LANES), F32)
    prs = range(n_pairs)
    lanes_of = lambda x: [stack(x[:, p * LANES:(p + 1) * LANES]) for p in prs]
    a_s, q_s, k_s, b_s = lanes_of(a_t), lanes_of(q_t), lanes_of(k_t), lanes_of(b_t)
    v_s, kh_s, bh_s = lanes_of(v), lanes_of(k_hat), lanes_of(b_hat)
    gram = [_mmp(jnp.concatenate([a_s[p], q_s[p]], axis=0), jnp.concatenate([b_s[p], k_s[p]], axis=0),
                 NT, RW_PASSES["gram"]) for p in prs]
    a_ab = [jnp.where(strict, gram[p][:s2, :s2], 0.0) for p in prs]
    a_ak = [jnp.where(strict, gram[p][:s2, s2:], 0.0) for p in prs]
    b_rbk = [jnp.concatenate([jnp.where(incl, gram[p][s2:, :s2], 0.0),
                              jnp.where(incl, gram[p][s2:, s2:], 0.0)], axis=1) for p in prs]
    akv = [_mmp(a_ak[p], v_s[p], passes=RW_PASSES["mix"]) for p in prs]
    tinv = [eye2 + a_ab[p] for p in prs]
    apow = a_ab
    for _ in range(n_dbl):
        apow = [_mmp(apow[p], apow[p], passes=RW_PASSES["inv"]) for p in prs]
        tinv = [tinv[p] + _mmp(apow[p], tinv[p], passes=RW_PASSES["inv"]) for p in prs]
    au = [_mmp(tinv[p], jnp.concatenate([a_s[p], akv[p]], axis=1), passes=RW_PASSES["mix"])
          for p in prs]
    ry = [_mmp(b_rbk[p], jnp.concatenate([au[p], jnp.concatenate([zeros_s, v_s[p]], axis=1)], axis=0),
               passes=RW_PASSES["mix"]) for p in prs]
    m_p = [eye * w_tot[:, p * LANES:(p + 1) * LANES]
           + _mmp(au[p][:, :LANES], bh_s[p], TN, RW_PASSES["trans"]) for p in prs]
    n_p = [_mmp(jnp.concatenate([au[p][:, LANES:], v_s[p]], axis=0),
                jnp.concatenate([bh_s[p], kh_s[p]], axis=0), TN, RW_PASSES["trans"]) for p in prs]
    s_old = [s_scr[p] for p in prs]
    y_s = [_mmp(q_s[p] + ry[p][:, :LANES], s_old[p], NT, RW_PASSES["out"]) + ry[p][:, LANES:]
           for p in prs]
    for p in prs:
        s_scr[p] = _mmp(s_old[p], m_p[p], passes=RW_PASSES["state"]) + n_p[p]
    y = jnp.concatenate([y_s[p][:cs] + y_s[p][cs:] for p in prs], axis=1)

    mean = seg_sum(y) * (1.0 / RWKV_HD)
    dy = y - mean
    var = seg_sum(dy * dy) * (1.0 / RWKV_HD)
    yn = dy * lax.rsqrt(var + GN_EPS) * ln_w + ln_b
    o_ref[...] = ((yn + bonus) * gate).astype(o_ref.dtype)

    @pl.when(ci == pl.num_programs(1) - 1)
    def _():
        sfin_ref[...] = s_scr[...]


def _rwkv(p, shift_rkv, shift_aux, s0_pairs, mu_rkv, mu_aux, vecs, ww, wa, wg, n_seq, t_len, cs, n_valid, out_dtype):
    n_chunks = t_len // cs
    n_pairs = s0_pairs.shape[1]
    dr = n_pairs * LANES
    full = lambda shape: pl.BlockSpec(shape, lambda b, c: (0,) * len(shape))
    return pl.pallas_call(
        functools.partial(_rwkv_kernel, cs=cs, n_valid=n_valid),
        grid=(n_seq, n_chunks),
        in_specs=[pl.BlockSpec((cs, 3 * dr), lambda b, c: (b * n_chunks + c, P_RKV // (3 * dr))),
                  pl.BlockSpec((cs, P_AUX_W), lambda b, c: (b * n_chunks + c, P_AUX // P_AUX_W)),
                  pl.BlockSpec((None, 1, 3 * dr), lambda b, c: (b, 0, 0)),
                  pl.BlockSpec((None, 1, P_AUX_W), lambda b, c: (b, 0, 0)),
                  pl.BlockSpec((None, n_pairs, LANES, LANES), lambda b, c: (b, 0, 0, 0)),
                  full(mu_rkv.shape), full(mu_aux.shape), full(vecs.shape),
                  full(ww.shape), full(wa.shape), full(wg.shape)],
        out_specs=[pl.BlockSpec((cs, dr), lambda b, c: (b * n_chunks + c, 0)),
                   pl.BlockSpec((None, n_pairs, LANES, LANES), lambda b, c: (b, 0, 0, 0))],
        out_shape=[jax.ShapeDtypeStruct((n_seq * t_len, dr), out_dtype),
                   jax.ShapeDtypeStruct((n_seq, n_pairs, LANES, LANES), F32)],
        scratch_shapes=[pltpu.VMEM((n_pairs, LANES, LANES), F32),
                        pltpu.VMEM((1, 3 * dr), F32), pltpu.VMEM((1, P_AUX_W), F32)],
        compiler_params=_cparams(("arbitrary", "arbitrary")),
    )(p, p, shift_rkv, shift_aux, s0_pairs, mu_rkv, mu_aux, vecs, ww, wa, wg)


def _out_kernel(x_ref, m_ref, on_ref, or_ref, wn_ref, wr_ref, o_ref):
    mix = _mm(on_ref[...], wn_ref[...]) + _mm(or_ref[...], wr_ref[...])
    o_ref[...] = x_ref[...] + m_ref[5] * mix


def _out_proj(x, mod, o_nsa, o_rw, w_top, w_bot, tm, tiles_per_seq):
    m_rows, d = x.shape
    dn = o_nsa.shape[1]
    mr = mod.shape[2]
    return pl.pallas_call(
        _out_kernel,
        grid=(m_rows // tm,),
        in_specs=[pl.BlockSpec((tm, d), lambda i: (i, 0)),
                  pl.BlockSpec((None, N_MOD, mr, d), lambda i: (i // tiles_per_seq, 0, 0, 0)),
                  pl.BlockSpec((tm, dn), lambda i: (i, 0)),
                  pl.BlockSpec((tm, dn), lambda i: (i, 0)),
                  pl.BlockSpec((dn, d), lambda i: (0, 0)),
                  pl.BlockSpec((dn, d), lambda i: (0, 0))],
        out_specs=pl.BlockSpec((tm, d), lambda i: (i, 0)),
        out_shape=jax.ShapeDtypeStruct((m_rows, d), F32),
        compiler_params=_cparams(("arbitrary",)),
    )(x, mod, o_nsa, o_rw, w_top, w_bot)


def _pad_cols(x, n):
    return jnp.pad(x, [(0, 0)] * (x.ndim - 1) + [(0, n)])


def _reorder_cols(w, d_rwkv, n_gate):
    nsa_main = P_AUX
    rw0 = nsa_main + n_gate
    lora0 = rw0 + 3 * d_rwkv
    n_lora = w.shape[-1] - lora0
    aux = jnp.concatenate([_pad_cols(w[..., nsa_main:rw0], AUX_WD - n_gate),
                           _pad_cols(w[..., lora0:], P_AUX_W - AUX_WD - n_lora)], axis=-1)
    return jnp.concatenate([w[..., :nsa_main], aux, w[..., rw0:lora0]], axis=-1)


def _pairs_from_heads(s):
    n, h = s.shape[:2]
    s = s.reshape(n, h // 2, 2, RWKV_HD, RWKV_HD)
    z = jnp.zeros_like(s[:, :, 0])
    top = jnp.concatenate([s[:, :, 0], z], axis=-1)
    bot = jnp.concatenate([z, s[:, :, 1]], axis=-1)
    return jnp.concatenate([top, bot], axis=-2)


def _heads_from_pairs(s):
    n, hp = s.shape[:2]
    a = s[:, :, :RWKV_HD, :RWKV_HD]
    b = s[:, :, RWKV_HD:, RWKV_HD:]
    return jnp.stack([a, b], axis=2).reshape(n, 2 * hp, RWKV_HD, RWKV_HD)


def kernel(x_prompt, x_sample, cache_nsa_kv, state_win_kv, state_wkv, state_shift, page_table,
           c_prompt, c_sample, w_ada, b_ada, norm_g, ffn_wi, ffn_wo, w_in, w_out,
           q_norm_g, k_norm_g, cmp_pe, cmp_w1, cmp_b1, cmp_w2,
           rwkv_mu, rwkv_w0, rwkv_w_w2, rwkv_a0, rwkv_w_a2, rwkv_w_g2,
           rwkv_k_k, rwkv_k_a, rwkv_r_k, rwkv_ln_w, rwkv_ln_b):
    bp, tp, d = x_prompt.shape
    bs, ts, _ = x_sample.shape
    depth = w_ada.shape[0]
    n_pool = cache_nsa_kv.shape[1]
    n_pages = page_table.shape[1]
    past = n_pages * PAGE
    wbuf = state_win_kv.shape[2]
    d_rwkv = rwkv_w0.shape[1]
    n_heads = d_rwkv // RWKV_HD
    d_nsa = N_QH * NSA_HD
    n_gate = 3 * N_QH
    n_dlora = rwkv_w_w2.shape[1]
    n_alora = rwkv_w_a2.shape[1]
    n_glora = rwkv_w_g2.shape[1]
    mp_rows, ms_rows = bp * tp, bs * ts
    tm = 512
    tiles_per_seq = tp // tm
    cs = 64
    ts_pad = -(-ts // cs) * cs
    win_keep = min(WINDOW, tp)

    c_all = jnp.concatenate([c_prompt, c_sample], axis=0)
    c_rows = -(-c_all.shape[0] // 8) * 8
    c_all = jnp.pad(c_all, ((0, c_rows - c_all.shape[0]), (0, 0)))
    mod = _modulation(c_all, w_ada, b_ada).reshape(depth, c_rows, N_MOD, d)
    mod_p = mod[:, :bp].reshape(depth, bp, N_MOD, 1, d)
    mod_s = jnp.repeat(mod[:, bp:bp + bs].transpose(0, 2, 1, 3), ts, axis=2).reshape(depth, 1, N_MOD, ms_rows, d)

    xp = x_prompt.reshape(mp_rows, d)
    xs = x_sample.reshape(ms_rows, d)
    pool = cache_nsa_kv.reshape(depth * n_pool * PAGE * 4 * N_KV, NSA_HD)
    prompt_pages = jnp.arange(bp * (tp // PAGE), dtype=jnp.int32).reshape(bp, tp // PAGE)
    n_sel_s = -(-(past + ts) // SEL_BLOCK)

    kv_p, kv_s, win_p, win_s, wkv_p, wkv_s, sh_p, sh_s = [], [], [], [], [], [], [], []
    for l in range(depth):
        wi = [ffn_wi[l, i].astype(BF16) for i in range(2)]
        wo = [ffn_wo[l, i].astype(BF16) for i in range(2)]
        w_in_l = _reorder_cols(w_in[l], d_rwkv, n_gate).astype(BF16)
        w_top = w_out[l, :d_nsa].astype(BF16)
        w_bot = w_out[l, d_nsa:].astype(BF16)
        g_rows = [norm_g[l, i].reshape(1, d) for i in range(3)]
        q_g = q_norm_g[l].reshape(1, NSA_HD)
        k_g = k_norm_g[l]
        w1 = cmp_w1[l].reshape(2, 2, CMP_STRIDE * NSA_HD, CMP_HID)
        w1cat = jnp.concatenate([w1[:, 0], w1[:, 1]], axis=-1).astype(BF16)
        pe2 = cmp_pe[l].reshape(2, 2, 1, CMP_STRIDE * NSA_HD)
        b1 = cmp_b1[l].reshape(2, 1, CMP_HID)
        w2 = cmp_w2[l].astype(BF16)
        def shift_parts(sh):
            aux = jnp.pad(sh[:, 3 * d_rwkv:], ((0, 0), (AUX_WD, P_AUX_W - AUX_WD - (sh.shape[1] - 3 * d_rwkv))))
            return sh[:, None, :3 * d_rwkv], aux[:, None, :]

        mu_rkv, mu_aux = (m[:, 0] for m in shift_parts(rwkv_mu[l].reshape(1, -1)))
        vecs = jnp.stack([rwkv_w0[l], rwkv_a0[l], rwkv_k_k[l], rwkv_k_a[l], rwkv_r_k[l].reshape(-1),
                          rwkv_ln_w[l], rwkv_ln_b[l], jnp.zeros_like(rwkv_w0[l])])
        lora_rows = lambda w, off: jnp.pad(w, ((off, P_AUX_W - off - w.shape[0]), (0, 0))).astype(BF16)
        ww = lora_rows(rwkv_w_w2[l], AUX_WD)
        wa = lora_rows(rwkv_w_a2[l], AUX_WD + n_dlora)
        wg = lora_rows(rwkv_w_g2[l], AUX_WD + n_dlora + n_alora)

        def shift_out(p_last):
            return jnp.concatenate([p_last[:, P_RKV:P_RKV + 3 * d_rwkv],
                                    p_last[:, P_AUX + AUX_WD:P_AUX + AUX_WD + n_dlora + n_alora + n_glora]],
                                   axis=1)

        xp = _ffn(xp, mod_p[l], g_rows[0], wi[0], wo[0], 0, tm, tiles_per_seq)
        xs = _ffn(xs, mod_s[l], g_rows[0], wi[0], wo[0], 0, ms_rows, 1)

        pp = _proj(xp, mod_p[l], g_rows[1], w_in_l, tm, tiles_per_seq)
        qn, rows, win, selbf, winbf, gates = _nsa_prep(pp, q_g, k_g, tm, BF16)
        kc, vc = _compress(rows, prompt_pages, w1cat, pe2, b1, w2, k_g)
        o_nsa = _nsa_prompt(qn, gates, selbf, winbf, kc, vc, bp, tp)
        zero_rkv, zero_aux = shift_parts(jnp.zeros((bp, state_shift.shape[2]), F32))
        o_rw, s_fin = _rwkv(pp, zero_rkv, zero_aux, jnp.zeros((bp, n_heads // 2, LANES, LANES), F32),
                            mu_rkv, mu_aux, vecs, ww, wa, wg, bp, tp, cs, cs, BF16)
        xp = _out_proj(xp, mod_p[l], o_nsa, o_rw, w_top, w_bot, tm, tiles_per_seq)
        kv_p.append(rows.reshape(bp, tp // PAGE, PAGE, 4, N_KV, NSA_HD))
        win_p.append(win.reshape(bp, tp, 2, N_KV, NSA_HD)[:, tp - win_keep:])
        wkv_p.append(_heads_from_pairs(s_fin))
        sh_p.append(shift_out(pp.reshape(bp, tp, P_COLS)[:, -1]))

        ps = _proj(xs, mod_s[l], g_rows[1], w_in_l, ms_rows, 1)
        qn_s, rows_s, win_new, _, _, gates_s = _nsa_prep(ps, q_g, k_g, ms_rows, F32)
        page_idx = page_table + l * n_pool
        kc_s, vc_s = _compress(pool, page_idx, w1cat, pe2, b1, w2, k_g)
        o_cmp, sel = _nsa_sample_select(qn_s, kc_s, vc_s, ts, past, n_sel_s)
        o_nsa_s = _nsa_sample_sweep(pool, page_idx, qn_s, gates_s, sel, o_cmp, rows_s, win_new,
                                    state_win_kv[l].reshape(bs, wbuf * 2 * N_KV, NSA_HD), ts, past, n_sel_s)
        sh_rkv, sh_aux = shift_parts(state_shift[l])
        ps_pad = jnp.pad(ps.reshape(bs, ts, P_COLS), ((0, 0), (0, ts_pad - ts), (0, 0)))
        o_rw_s, s_fin_s = _rwkv(ps_pad.reshape(bs * ts_pad, P_COLS), sh_rkv, sh_aux,
                                _pairs_from_heads(state_wkv[l]), mu_rkv, mu_aux, vecs, ww, wa, wg,
                                bs, ts_pad, ts_pad, ts, F32)
        o_rw_s = o_rw_s.reshape(bs, ts_pad, d_rwkv)[:, :ts].reshape(ms_rows, d_rwkv)
        xs = _out_proj(xs, mod_s[l], o_nsa_s.reshape(ms_rows, d_nsa).astype(BF16), o_rw_s.astype(BF16),
                       w_top, w_bot, ms_rows, 1)
        kv_s.append(rows_s.reshape(bs, ts, 4, N_KV, NSA_HD))
        win_s.append(jnp.concatenate([state_win_kv[l][:, ts:],
                                      win_new.reshape(bs, ts, 2, N_KV, NSA_HD)], axis=1))
        wkv_s.append(_heads_from_pairs(s_fin_s))
        sh_s.append(shift_out(ps.reshape(bs, ts, P_COLS)[:, -1]))

        xp = _ffn(xp, mod_p[l], g_rows[2], wi[1], wo[1], 2, tm, tiles_per_seq)
        xs = _ffn(xs, mod_s[l], g_rows[2], wi[1], wo[1], 2, ms_rows, 1)

    return (xp.reshape(bp, tp, d), xs.reshape(bs, ts, d),
            jnp.stack(kv_p), jnp.stack(kv_s), jnp.stack(win_p), jnp.stack(win_s),
            jnp.stack(wkv_p), jnp.stack(wkv_s), jnp.stack(sh_p), jnp.stack(sh_s))
```

```python
import functools
import math

import jax
import jax.numpy as jnp
from jax import lax
from jax.experimental import pallas as pl
from jax.experimental.pallas import tpu as pltpu

F32 = jnp.float32
BF16 = jnp.bfloat16

NSA_HD = 128
N_KV = 2
QPG = 4
N_QH = N_KV * QPG
CMP_LEN = 32
CMP_STRIDE = 16
CMP_HID = 2 * NSA_HD
SEL_BLOCK = 64
SEL_TOPK = 16
N_LOCAL = 2
WINDOW = 512
PAGE = 128
RWKV_HD = 64
N_MOD = 9
NORM_EPS = 1e-6
GN_EPS = 64e-5
NEG = -1e30
FORCED = 1e6
INVALID = -1e6

LANES = 128
VMEM_LIMIT = 56 * 1024 * 1024
PAGES_PER_STEP = 8

P_Q = 0
P_KV = 1024
P_AUX = 2560
P_AUX_W = 512
P_RKV = 3072
P_COLS = 6144
AUX_WD = 128
AUX_AD = 192
AUX_GD = 256


def _cparams(sem):
    return pltpu.CompilerParams(dimension_semantics=sem, vmem_limit_bytes=VMEM_LIMIT)


def _mm(a, b, dims=((1,), (0,))):
    return lax.dot_general(a, b, (dims, ((), ())), preferred_element_type=F32)


NT = ((1,), (1,))
TN = ((0,), (0,))


def _split2(x):
    hi = x.astype(BF16)
    lo = (x - hi.astype(F32)).astype(BF16)
    return hi, lo


def _split3(x):
    hi = x.astype(BF16)
    r1 = x - hi.astype(F32)
    mid = r1.astype(BF16)
    lo = (r1 - mid.astype(F32)).astype(BF16)
    return hi, mid, lo


def _mm3(a, b, dims=((1,), (0,))):
    ah, al = _split2(a)
    bh, bl = _split2(b)
    return _mm(ah, bh, dims) + (_mm(ah, bl, dims) + _mm(al, bh, dims))


def _mmp(a, b, dims=((1,), (0,)), passes=1):
    if passes == 3:
        return _mm3(a, b, dims)
    return _mm(a.astype(BF16), b.astype(BF16), dims)


RW_PASSES = {"gram": 1, "inv": 1, "mix": 1, "trans": 1, "out": 1, "state": 1}


def _mm_lhs_exact(a, b_bf16, dims=((1,), (0,))):
    a1, a2, a3 = _split3(a)
    return _mm(a1, b_bf16, dims) + (_mm(a2, b_bf16, dims) + _mm(a3, b_bf16, dims))


def _mm_rhs_exact(a_bf16, b, dims=((1,), (0,))):
    b1, b2, b3 = _split3(b)
    return _mm(a_bf16, b1, dims) + (_mm(a_bf16, b2, dims) + _mm(a_bf16, b3, dims))


def _silu(x):
    return x * jax.nn.sigmoid(x)


def _rms(x, g):
    return x * lax.rsqrt(jnp.mean(x * x, axis=-1, keepdims=True) + NORM_EPS) * g


def _ada_norm(x, m_ref, slot, g):
    return _rms(x, g) * (1.0 + m_ref[3 * slot + 1]) + m_ref[3 * slot]


def _mod_kernel(c_ref, w_ref, b_ref, o_ref):
    s = _silu(c_ref[...]).astype(BF16)
    o_ref[0] = _mm(s, w_ref[0].astype(BF16)) + b_ref[0]


def _modulation(c_all, w_ada, b_ada):
    depth, d, n = w_ada.shape
    rows = c_all.shape[0]
    tn = 1024
    return pl.pallas_call(
        _mod_kernel,
        grid=(depth, n // tn),
        in_specs=[pl.BlockSpec((rows, d), lambda l, j: (0, 0)),
                  pl.BlockSpec((1, d, tn), lambda l, j: (l, 0, j)),
                  pl.BlockSpec((1, 1, tn), lambda l, j: (l, 0, j))],
        out_specs=pl.BlockSpec((1, rows, tn), lambda l, j: (l, 0, j)),
        out_shape=jax.ShapeDtypeStruct((depth, rows, n), F32),
        compiler_params=_cparams(("arbitrary", "arbitrary")),
    )(c_all, w_ada, b_ada.reshape(depth, 1, n))


def _ffn_kernel(x_ref, m_ref, g_ref, wg_ref, wu_ref, wo_ref, o_ref, h_scr, acc_scr, *, slot):
    f = pl.program_id(1)

    @pl.when(f == 0)
    def _():
        h_scr[...] = _ada_norm(x_ref[...], m_ref, slot, g_ref[...]).astype(BF16)
        acc_scr[...] = jnp.zeros_like(acc_scr)

    h = h_scr[...]
    gate = _mm(h, wg_ref[...])
    up = _mm(h, wu_ref[...])
    act = (_silu(gate) * up).astype(BF16)
    acc_scr[...] += _mm(act, wo_ref[...])

    @pl.when(f == pl.num_programs(1) - 1)
    def _():
        o_ref[...] = x_ref[...] + 0.5 * m_ref[3 * slot + 2] * acc_scr[...]


def _ffn(x, mod, g, wi, wo, layer, half, slot, tm, tiles_per_seq):
    m_rows, d = x.shape
    d_ff = wo.shape[2]
    tf = 512
    nf = d_ff // tf
    mr = mod.shape[2]
    return pl.pallas_call(
        functools.partial(_ffn_kernel, slot=slot),
        grid=(m_rows // tm, nf),
        in_specs=[pl.BlockSpec((tm, d), lambda i, f: (i, 0)),
                  pl.BlockSpec((None, N_MOD, mr, d), lambda i, f: (i // tiles_per_seq, 0, 0, 0)),
                  pl.BlockSpec((1, d), lambda i, f: (0, 0)),
                  pl.BlockSpec((None, None, d, tf), lambda i, f: (layer, half, 0, f)),
                  pl.BlockSpec((None, None, d, tf), lambda i, f: (layer, half, 0, nf + f)),
                  pl.BlockSpec((None, None, tf, d), lambda i, f: (layer, half, f, 0))],
        out_specs=pl.BlockSpec((tm, d), lambda i, f: (i, 0)),
        out_shape=jax.ShapeDtypeStruct((m_rows, d), F32),
        scratch_shapes=[pltpu.VMEM((tm, d), BF16), pltpu.VMEM((tm, d), F32)],
        compiler_params=_cparams(("arbitrary", "arbitrary")),
    )(x, mod, g, wi, wi, wo)


def _proj_kernel(x_ref, m_ref, g_ref, w_ref, o_ref, h_scr):
    @pl.when(pl.program_id(1) == 0)
    def _():
        h_scr[...] = _ada_norm(x_ref[...], m_ref, 1, g_ref[...]).astype(BF16)

    o_ref[...] = _mm(h_scr[...], w_ref[...])


def _proj(x, mod, g, w, layer, tm, tiles_per_seq):
    m_rows, d = x.shape
    n = w.shape[2]
    tn = 1536
    mr = mod.shape[2]
    return pl.pallas_call(
        _proj_kernel,
        grid=(m_rows // tm, n // tn),
        in_specs=[pl.BlockSpec((tm, d), lambda i, j: (i, 0)),
                  pl.BlockSpec((None, N_MOD, mr, d), lambda i, j: (i // tiles_per_seq, 0, 0, 0)),
                  pl.BlockSpec((1, d), lambda i, j: (0, 0)),
                  pl.BlockSpec((None, d, tn), lambda i, j: (layer, 0, j))],
        out_specs=pl.BlockSpec((tm, tn), lambda i, j: (i, j)),
        out_shape=jax.ShapeDtypeStruct((m_rows, n), F32),
        scratch_shapes=[pltpu.VMEM((tm, d), BF16)],
        compiler_params=_cparams(("arbitrary", "arbitrary")),
    )(x, mod, g, w)


def _rms_heads(x, g):
    outs = []
    for h in range(x.shape[1] // NSA_HD):
        outs.append(_rms(x[:, h * NSA_HD:(h + 1) * NSA_HD], g))
    return jnp.concatenate(outs, axis=1)


def _nsa_prep_kernel(q_ref, kva_ref, kvb_ref, kvc_ref, gt_ref, qg_ref, kg_ref,
                     qn_ref, rows_ref, win_ref, selbf_ref, winbf_ref, gates_ref):
    qn_ref[...] = (_rms_heads(q_ref[...], qg_ref[...]) * (NSA_HD ** -0.5)).astype(qn_ref.dtype)
    kvb = kvb_ref[...]
    ksel = _rms_heads(kvb[:, :2 * NSA_HD], kg_ref[1:2, :])
    selrows = jnp.concatenate([ksel, kvb[:, 2 * NSA_HD:]], axis=1)
    rows = jnp.concatenate([kva_ref[...], selrows], axis=1)
    tm = rows.shape[0]
    for c in range(4 * N_KV):
        rows_ref[pl.ds(c, tm, stride=4 * N_KV), :] = rows[:, c * NSA_HD:(c + 1) * NSA_HD]
    selbf_ref[...] = selrows.astype(selbf_ref.dtype)
    kvc = kvc_ref[...]
    kwin = _rms_heads(kvc[:, :2 * NSA_HD], kg_ref[2:3, :])
    winrows = jnp.concatenate([kwin, kvc[:, 2 * NSA_HD:]], axis=1)
    for c in range(2 * N_KV):
        win_ref[pl.ds(c, tm, stride=2 * N_KV), :] = winrows[:, c * NSA_HD:(c + 1) * NSA_HD]
    winbf_ref[...] = winrows.astype(winbf_ref.dtype)
    gates_ref[...] = jax.nn.sigmoid(gt_ref[...])


def _nsa_prep(p, q_g, k_g, tm, act_dtype):
    m_rows = p.shape[0]
    row = lambda w, j: pl.BlockSpec((tm, w), lambda i: (i, j))
    return pl.pallas_call(
        _nsa_prep_kernel,
        grid=(m_rows // tm,),
        in_specs=[row(1024, 0), row(512, 2), row(512, 3), row(512, 4), row(LANES, P_AUX // LANES),
                  pl.BlockSpec((1, NSA_HD), lambda i: (0, 0)),
                  pl.BlockSpec((3, NSA_HD), lambda i: (0, 0))],
        out_specs=[row(1024, 0), pl.BlockSpec((tm * 8, NSA_HD), lambda i: (i, 0)),
                   pl.BlockSpec((tm * 4, NSA_HD), lambda i: (i, 0)),
                   row(512, 0), row(512, 0), row(LANES, 0)],
        out_shape=[jax.ShapeDtypeStruct((m_rows, 1024), act_dtype),
                   jax.ShapeDtypeStruct((m_rows * 8, NSA_HD), F32),
                   jax.ShapeDtypeStruct((m_rows * 4, NSA_HD), F32),
                   jax.ShapeDtypeStruct((m_rows, 512), act_dtype),
                   jax.ShapeDtypeStruct((m_rows, 512), act_dtype),
                   jax.ShapeDtypeStruct((m_rows, LANES), F32)],
        compiler_params=_cparams(("arbitrary",)),
    )(p, p, p, p, p, q_g, k_g)


def _cmp_kernel(pt_ref, *refs, n_sub, pp, emit_sel):
    pages = refs[:pp]
    if emit_sel:
        w1_ref, pe_ref, b1_ref, w2_ref, kg_ref, kc_ref, vc_ref, sel_ref, x_scr = refs[pp:]
    else:
        w1_ref, pe_ref, b1_ref, w2_ref, kg_ref, kc_ref, vc_ref, x_scr = refs[pp:]
    p = pl.program_id(1)
    spp = PAGE // CMP_STRIDE
    base = pl.multiple_of(p * (pp * spp), pp * spp)
    heads = 4 * N_KV
    if emit_sel:
        for c in range(2 * N_KV):
            rows_c = [pg[pl.ds(2 * N_KV + c, PAGE, stride=heads), :] for pg in pages]
            sel_ref[:, c * NSA_HD:(c + 1) * NSA_HD] = jnp.concatenate(rows_c, axis=0).astype(BF16)
    for c in range(2 * N_KV):
        for l in range(CMP_STRIDE):
            pieces = [pg[pl.ds(l * heads + c, spp, stride=CMP_STRIDE * heads), :] for pg in pages]
            x_scr[c, pl.ds(base, pp * spp), l * NSA_HD:(l + 1) * NSA_HD] = (
                jnp.concatenate(pieces, axis=0).astype(BF16))

    @pl.when(p == pl.num_programs(1) - 1)
    def _():
        row = lax.broadcasted_iota(jnp.int32, (n_sub, NSA_HD), 0)
        for kv in range(2):
            w1 = w1_ref[kv]
            const = b1_ref[kv]
            for j in range(2):
                pe = jnp.broadcast_to(pe_ref[kv, j], (8, CMP_STRIDE * NSA_HD)).astype(BF16)
                const = const + _mm(pe, w1[:, j * CMP_HID:(j + 1) * CMP_HID])[0:1]
            for g in range(N_KV):
                ab = _mm(x_scr[kv * 2 + g], w1)
                nxt = pltpu.roll(ab[:, CMP_HID:], n_sub - 1, 0)
                acc = ab[:, :CMP_HID] + nxt + const
                o = _mm(jax.nn.gelu(acc).astype(BF16), w2_ref[kv])
                if kv == 0:
                    o = _rms(o, kg_ref[0:1, :])
                o = jnp.where(row < n_sub - 1, o, 0.0)
                if kv == 0:
                    kc_ref[g] = o
                else:
                    vc_ref[g] = o


def _compress(pool, page_idx, w1cat, pe2, b1, w2, k_g, emit_sel):
    nb, n_pages = page_idx.shape
    n_sub = n_pages * (PAGE // CMP_STRIDE)
    pp = PAGES_PER_STEP
    page = lambda k: pl.BlockSpec((PAGE * 4 * N_KV, NSA_HD), lambda b, p, pt: (pt[b, pp * p + k], 0))
    page_specs = [page(k) for k in range(pp)]
    full = lambda shape: pl.BlockSpec(shape, lambda b, p, pt: (0,) * len(shape))
    out = pl.BlockSpec((None, N_KV, n_sub, NSA_HD), lambda b, p, pt: (b, 0, 0, 0))
    out_specs = [out, out]
    out_shape = [jax.ShapeDtypeStruct((nb, N_KV, n_sub, NSA_HD), F32)] * 2
    if emit_sel:
        out_specs.append(pl.BlockSpec((None, pp * PAGE, 512), lambda b, p, pt: (b, p, 0)))
        out_shape.append(jax.ShapeDtypeStruct((nb, n_pages * PAGE, 512), BF16))
    return pl.pallas_call(
        functools.partial(_cmp_kernel, n_sub=n_sub, pp=pp, emit_sel=emit_sel),
        grid_spec=pltpu.PrefetchScalarGridSpec(
            num_scalar_prefetch=1,
            grid=(nb, n_pages // pp),
            in_specs=page_specs + [full(w1cat.shape), full(pe2.shape), full(b1.shape),
                      full(w2.shape), full(k_g.shape)],
            out_specs=out_specs,
            scratch_shapes=[pltpu.VMEM((4, n_sub, CMP_STRIDE * NSA_HD), BF16)]),
        out_shape=out_shape,
        compiler_params=_cparams(("arbitrary", "arbitrary")),
    )(page_idx, *([pool] * pp), w1cat, pe2, b1, w2, k_g)


def _imp_matrix(n_sub, n_cmp, width):
    n = lax.broadcasted_iota(jnp.int32, (n_sub, width), 0)
    j = lax.broadcasted_iota(jnp.int32, (n_sub, width), 1)
    spb = SEL_BLOCK // CMP_STRIDE
    m = jnp.where(n // spb == j, 1.0, 0.0) + jnp.where((n + 1) // spb == j, 1.0, 0.0)
    return jnp.where(n < n_cmp, m, 0.0).astype(BF16)


def _topk_mask_t(st_scr, n_iter, width):
    jj = st_scr.shape[0]
    st = st_scr[...]
    jrow = lax.broadcasted_iota(jnp.int32, (jj, width), 0)

    def body(jp, rank):
        r = st_scr[pl.ds(jp, 1), :]
        beats = (r > st) | ((r == st) & (jp < jrow))
        return rank + jnp.where(beats, 1.0, 0.0)

    return lax.fori_loop(0, n_iter, body, jnp.zeros((jj, width), F32))


def _softmax_rows(s, ok):
    sm = jnp.where(ok, s, NEG)
    mx = jnp.max(sm, axis=-1, keepdims=True)
    e = jnp.where(ok, jnp.exp(sm - mx), 0.0)
    den = jnp.sum(e, axis=-1, keepdims=True)
    return e / jnp.where(den > 0.0, den, 1.0)


def _nsa_prompt_kernel(q_ref, gt_ref, sel_ref, win_ref, kc_ref, vc_ref, oh_ref, o_ref, st_scr,
                       *, t_len, n_sub, ck):
    tq = SEL_BLOCK
    qi = pl.program_id(1)
    s0 = qi * tq
    n_cmp = n_sub - 1
    rows = QPG * tq
    q = q_ref[...]
    rowpos = s0 + lax.broadcasted_iota(jnp.int32, (rows, 1), 0) % tq

    def q_group(g):
        return jnp.concatenate(
            [q[:, (g * QPG + h) * NSA_HD:(g * QPG + h + 1) * NSA_HD] for h in range(QPG)], axis=0)

    imp_m = _imp_matrix(n_sub, n_cmp, LANES)
    n_idx = lax.broadcasted_iota(jnp.int32, (rows, n_sub), 1)
    ok_c = (n_idx * CMP_STRIDE + (CMP_LEN - 1) <= rowpos) & (n_idx < n_cmp)
    o_cmp, scores = [], []
    jl = lax.broadcasted_iota(jnp.int32, (tq, LANES), 1)
    forced = (jl == 0) | ((jl <= qi) & (jl > qi - N_LOCAL))
    for g in range(N_KV):
        qg = q_group(g)
        s_c = _mm(qg, kc_ref[g].astype(BF16), NT)
        p_c = _softmax_rows(s_c, ok_c)
        o_cmp.append(_mm(p_c.astype(BF16), vc_ref[g].astype(BF16)))
        p_g = p_c[0:tq] + p_c[tq:2 * tq] + p_c[2 * tq:3 * tq] + p_c[3 * tq:4 * tq]
        imp = _mm_lhs_exact(p_g, imp_m)
        scores.append(jnp.where(jl <= qi, jnp.where(forced, FORCED, imp), INVALID))
    jj = st_scr.shape[0]
    st_scr[...] = jnp.concatenate(scores, axis=0).T[:jj]
    rank = _topk_mask_t(st_scr, qi + 1, N_KV * tq)
    jrow = lax.broadcasted_iota(jnp.int32, (jj, N_KV * tq), 0)
    sel_t = jnp.where((rank < SEL_TOPK) & (jrow <= qi), 1.0, 0.0)
    if jj < LANES:
        sel_t = jnp.concatenate([sel_t, jnp.zeros((LANES - jj, N_KV * tq), F32)], axis=0)
    sel = sel_t.T

    qgs = [q_group(g) for g in range(N_KV)]
    q_aug = []
    for g in range(N_KV):
        sg = jnp.concatenate([sel[g * tq:(g + 1) * tq]] * QPG, axis=0)
        q_aug.append(jnp.concatenate([qgs[g], ((sg - 1.0) * (-NEG)).astype(BF16)], axis=1))
    tokpos = s0 + lax.broadcasted_iota(jnp.int32, (1, rows), 1) % tq

    def chunks(cs, causal):
        scores_t, vals = [], []
        for c, cz in zip(cs, causal):
            k0 = pl.multiple_of(c * ck, ck)
            keypos = k0 + lax.broadcasted_iota(jnp.int32, (ck, 1), 0)
            onehot = oh_ref[pl.ds(k0, ck), :]
            for g in range(N_KV):
                kk = sel_ref[pl.ds(k0, ck), g * NSA_HD:(g + 1) * NSA_HD]
                s = _mm(jnp.concatenate([kk, onehot], axis=1), q_aug[g], NT)
                scores_t.append(jnp.where(keypos <= tokpos, s, NEG) if cz else s)
                vals.append(sel_ref[pl.ds(k0, ck), (N_KV + g) * NSA_HD:(N_KV + g + 1) * NSA_HD])
        maxes = [jnp.max(s, axis=0, keepdims=True) for s in scores_t]
        probs = [jnp.exp(s - m) for s, m in zip(scores_t, maxes)]
        sums = [jnp.sum(p, axis=0, keepdims=True) for p in probs]
        accs = [_mm(v, p.astype(BF16), TN) for v, p in zip(vals, probs)]
        parts = list(zip(maxes, sums, accs))
        return [tuple(parts[i * N_KV:(i + 1) * N_KV]) for i in range(len(cs))]

    def merge(carry, *chunks):
        new = []
        for g in range(N_KV):
            m_i, l_i, acc = carry[g]
            m_new = m_i
            for ch in chunks:
                m_new = jnp.maximum(m_new, ch[g][0])
            w_i = jnp.exp(m_i - m_new)
            l_new, acc_new = w_i * l_i, w_i * acc
            for ch in chunks:
                w_c = jnp.exp(ch[g][0] - m_new)
                l_new = l_new + w_c * ch[g][1]
                acc_new = acc_new + w_c * ch[g][2]
            new.append((m_new, l_new, acc_new))
        return tuple(new)

    neutral = tuple((jnp.full((1, rows), NEG, F32), jnp.zeros((1, rows), F32),
                     jnp.zeros((NSA_HD, rows), F32)) for _ in range(N_KV))
    c_diag = s0 // ck
    n_pairs = c_diag // 2
    mid = lax.fori_loop(
        0, n_pairs, lambda i, carry: merge(carry, *chunks([2 * i, 2 * i + 1], [False, False])), neutral)
    fin = lax.cond(c_diag % 2 == 1,
                   lambda: merge(mid, *chunks([c_diag - 1, c_diag], [False, True])),
                   lambda: merge(mid, *chunks([c_diag], [True])))

    wlen = WINDOW + 2 * tq
    w0 = pl.multiple_of(jnp.clip(s0 - WINDOW, 0, t_len - wlen), tq)
    wpos = w0 + lax.broadcasted_iota(jnp.int32, (wlen, 1), 0)
    dist = tokpos - wpos
    bias_w = jnp.where((dist >= 0) & (dist < WINDOW), 0.0, NEG)
    s_ws = [_mm(win_ref[pl.ds(w0, wlen), g * NSA_HD:(g + 1) * NSA_HD], qgs[g], NT) + bias_w
            for g in range(N_KV)]
    e_ws = [jnp.exp(s - jnp.max(s, axis=0, keepdims=True)) for s in s_ws]
    o_ws = [(_mm(win_ref[pl.ds(w0, wlen), (N_KV + g) * NSA_HD:(N_KV + g + 1) * NSA_HD],
                 e_ws[g].astype(BF16), TN) / jnp.sum(e_ws[g], axis=0, keepdims=True)).T
            for g in range(N_KV)]
    gt = gt_ref[...]
    outs = []
    for g in range(N_KV):
        o_w = o_ws[g]
        _, l_i, acc = fin[g]
        o_s = (acc / l_i).T
        for h in range(QPG):
            c0 = (g * QPG + h) * 3
            r0 = slice(h * tq, (h + 1) * tq)
            outs.append(gt[:, c0:c0 + 1] * o_cmp[g][r0] + gt[:, c0 + 1:c0 + 2] * o_s[r0]
                        + gt[:, c0 + 2:c0 + 3] * o_w[r0])
    o_ref[...] = jnp.concatenate(outs, axis=1).astype(o_ref.dtype)


def _nsa_prompt(qn, gates, selbf, winbf, kc, vc, nb, t_len):
    n_sub = kc.shape[2]
    tq = SEL_BLOCK
    nq = t_len // tq
    ck = 512
    n_blocks = -(-(t_len // SEL_BLOCK) // 8) * 8
    assert n_blocks <= LANES
    block_onehot = (jnp.arange(t_len)[:, None] // SEL_BLOCK == jnp.arange(LANES)[None, :]).astype(BF16)
    per_b = lambda w: pl.BlockSpec((None, t_len, w), lambda b, i: (b, 0, 0))
    cmp_spec = pl.BlockSpec((None, N_KV, n_sub, NSA_HD), lambda b, i: (b, 0, 0, 0))
    return pl.pallas_call(
        functools.partial(_nsa_prompt_kernel, t_len=t_len, n_sub=n_sub, ck=ck),
        grid=(nb, nq),
        in_specs=[pl.BlockSpec((tq, 1024), lambda b, i: (b * nq + i, 0)),
                  pl.BlockSpec((tq, LANES), lambda b, i: (b * nq + i, 0)),
                  per_b(512), per_b(512), cmp_spec, cmp_spec,
                  pl.BlockSpec((t_len, LANES), lambda b, i: (0, 0))],
        out_specs=pl.BlockSpec((tq, 1024), lambda b, i: (b * nq + i, 0)),
        out_shape=jax.ShapeDtypeStruct((nb * t_len, 1024), BF16),
        scratch_shapes=[pltpu.VMEM((n_blocks, N_KV * tq), F32)],
        compiler_params=_cparams(("arbitrary", "arbitrary")),
    )(qn, gates, selbf.reshape(nb, t_len, 512), winbf.reshape(nb, t_len, 512), kc, vc, block_onehot)


def _nsa_sample_select_kernel(q_ref, kc_ref, vc_ref, oc_ref, sel_ref, sc_scr, st_scr,
                              *, n_sub, n_sel, ts, past, jw):
    b = pl.program_id(0)
    nb = pl.num_programs(0)
    n_cmp = n_sub - 1
    q = q_ref[...]
    imp_m = _imp_matrix(n_sub, n_cmp, jw)
    rows = QPG * ts
    rowpos = past + lax.broadcasted_iota(jnp.int32, (rows, 1), 0) % ts
    n_idx = lax.broadcasted_iota(jnp.int32, (rows, n_sub), 1)
    ok_c = (n_idx * CMP_STRIDE + (CMP_LEN - 1) <= rowpos) & (n_idx < n_cmp)
    jl = lax.broadcasted_iota(jnp.int32, (ts, jw), 1)
    blk = (past + lax.broadcasted_iota(jnp.int32, (ts, 1), 0)) // SEL_BLOCK
    forced = (jl == 0) | ((jl <= blk) & (jl > blk - N_LOCAL))

    @pl.when(b == 0)
    def _():
        sc_scr[...] = jnp.full(sc_scr.shape, INVALID, F32)

    for g in range(N_KV):
        qg = jnp.concatenate(
            [q[:, (g * QPG + h) * NSA_HD:(g * QPG + h + 1) * NSA_HD] for h in range(QPG)],
            axis=0).astype(BF16)
        s_c = _mm(qg, kc_ref[g].astype(BF16), NT)
        p_c = _softmax_rows(s_c, ok_c)
        oc_ref[g * rows:(g + 1) * rows, :] = _mm(p_c.astype(BF16), vc_ref[g].astype(BF16))
        p_g = p_c[0:ts] + p_c[ts:2 * ts] + p_c[2 * ts:3 * ts] + p_c[3 * ts:4 * ts]
        imp = _mm_lhs_exact(p_g, imp_m)
        score = jnp.where((jl <= blk) & (jl < n_sel), jnp.where(forced, FORCED, imp), INVALID)
        sc_scr[pl.ds(pl.multiple_of((b * N_KV + g) * ts, ts), ts), :] = score

    @pl.when(b == nb - 1)
    def _():
        st_scr[...] = sc_scr[...].T
        rank = _topk_mask_t(st_scr, n_sel, LANES)
        sel_t = jnp.where((rank < SEL_TOPK) & (st_scr[...] > 0.5 * INVALID), 1.0, 0.0)
        sel_ref[...] = sel_t.T


def _nsa_sample_select(qn_s, kc, vc, ts, past, n_sel):
    nb = kc.shape[0]
    n_sub = kc.shape[2]
    jw = -(-n_sel // LANES) * LANES
    assert nb * N_KV * ts <= LANES
    cmp_spec = pl.BlockSpec((None, N_KV, n_sub, NSA_HD), lambda b: (b, 0, 0, 0))
    return pl.pallas_call(
        functools.partial(_nsa_sample_select_kernel, n_sub=n_sub, n_sel=n_sel, ts=ts, past=past, jw=jw),
        grid=(nb,),
        in_specs=[pl.BlockSpec((None, ts, 1024), lambda b: (b, 0, 0)), cmp_spec, cmp_spec],
        out_specs=[pl.BlockSpec((None, N_QH * ts, NSA_HD), lambda b: (b, 0, 0)),
                   pl.BlockSpec((LANES, jw), lambda b: (0, 0))],
        out_shape=[jax.ShapeDtypeStruct((nb, N_QH * ts, NSA_HD), F32),
                   jax.ShapeDtypeStruct((LANES, jw), F32)],
        scratch_shapes=[pltpu.VMEM((LANES, jw), F32), pltpu.VMEM((jw, LANES), F32)],
        compiler_params=_cparams(("arbitrary",)),
    )(qn_s.reshape(nb, ts, 1024), kc, vc)


def _nsa_sample_sweep_kernel(past_ref, q_ref, gt_ref, sel_ref, oc_ref, rows_ref, wnew_ref, wst_ref,
                             o_ref, m_scr, l_scr, acc_scr, *, nk, ts, past, n_sel, wbuf):
    step = pl.program_id(1)
    rows = QPG * ts
    heads = 4 * N_KV
    jw = sel_ref.shape[1]
    q = q_ref[...]
    qgs = [jnp.concatenate(
        [q[:, (g * QPG + h) * NSA_HD:(g * QPG + h + 1) * NSA_HD] for h in range(QPG)],
        axis=0).astype(BF16) for g in range(N_KV)]
    sel = sel_ref[...].astype(BF16)

    @pl.when(step == 0)
    def _():
        m_scr[...] = jnp.full(m_scr.shape, NEG, F32)
        l_scr[...] = jnp.zeros_like(l_scr)
        acc_scr[...] = jnp.zeros_like(acc_scr)

    def update(g, s, ok, vv):
        r0 = slice(g * rows, (g + 1) * rows)
        s = jnp.where(ok, s, NEG)
        m_i = m_scr[r0]
        m_new = jnp.maximum(m_i, jnp.max(s, axis=-1, keepdims=True))
        pr = jnp.where(ok, jnp.exp(s - m_new), 0.0)
        alpha = jnp.exp(m_i - m_new)
        l_scr[r0] = alpha * l_scr[r0] + jnp.sum(pr, axis=-1, keepdims=True)
        acc_scr[r0] = alpha * acc_scr[r0] + _mm(pr.astype(BF16), vv)
        m_scr[r0] = m_new

    ej = lax.broadcasted_iota(jnp.int32, (jw, nk), 0)
    kl = lax.broadcasted_iota(jnp.int32, (1, nk), 1)
    expand = jnp.where(ej == (step * nk + kl) // SEL_BLOCK, 1.0, 0.0).astype(BF16)
    mask2 = _mm(sel, expand)
    for g in range(N_KV):
        kk = past_ref[:, g * NSA_HD:(g + 1) * NSA_HD]
        vv = past_ref[:, (N_KV + g) * NSA_HD:(N_KV + g + 1) * NSA_HD]
        mg = mask2[g * ts:(g + 1) * ts]
        ok = jnp.concatenate([mg] * QPG, axis=0) > 0.5
        update(g, _mm(qgs[g], kk, NT), ok, vv)

    @pl.when(step == pl.num_programs(1) - 1)
    def _():
        tpos = lax.broadcasted_iota(jnp.int32, (rows, 1), 0) % ts
        pad = jnp.zeros((PAGE - ts, NSA_HD), F32)
        il = lax.broadcasted_iota(jnp.int32, (1, PAGE), 1)
        last_sel = sel_ref[:, n_sel - 1:n_sel]
        gt = gt_ref[...]
        wlen = wbuf + PAGE
        wl = lax.broadcasted_iota(jnp.int32, (1, wlen), 1)
        dist = (past + tpos) - (past - wbuf + wl)
        ok_w = (dist >= 0) & (dist < WINDOW) & (wl < wbuf + ts)
        outs = []
        for g in range(N_KV):
            kn = jnp.concatenate([rows_ref[pl.ds(2 * N_KV + g, ts, stride=heads), :], pad], axis=0)
            vn = jnp.concatenate([rows_ref[pl.ds(3 * N_KV + g, ts, stride=heads), :], pad], axis=0)
            lsel = jnp.concatenate([last_sel[g * ts:(g + 1) * ts]] * QPG, axis=0) > 0.5
            ok = (il <= tpos) & (il < ts) & lsel
            update(g, _mm(qgs[g], kn.astype(BF16), NT), ok, vn.astype(BF16))
            r0 = slice(g * rows, (g + 1) * rows)
            o_s = acc_scr[r0] / l_scr[r0]
            kw = jnp.concatenate([wst_ref[pl.ds(g, wbuf, stride=2 * N_KV), :],
                                  wnew_ref[pl.ds(g, ts, stride=2 * N_KV), :], pad], axis=0).astype(BF16)
            vw = jnp.concatenate([wst_ref[pl.ds(N_KV + g, wbuf, stride=2 * N_KV), :],
                                  wnew_ref[pl.ds(N_KV + g, ts, stride=2 * N_KV), :], pad],
                                 axis=0).astype(BF16)
            p_w = _softmax_rows(_mm(qgs[g], kw, NT), ok_w)
            o_w = _mm(p_w.astype(BF16), vw)
            o_c = oc_ref[r0, :]
            for h in range(QPG):
                c0 = (g * QPG + h) * 3
                rh = slice(h * ts, (h + 1) * ts)
                outs.append(gt[:, c0:c0 + 1] * o_c[rh] + gt[:, c0 + 1:c0 + 2] * o_s[rh]
                            + gt[:, c0 + 2:c0 + 3] * o_w[rh])
        o_ref[...] = jnp.concatenate(outs, axis=1)


def _nsa_sample_sweep(past_sel, qn_s, gates_s, sel, o_cmp, rows_s, win_s, win_state, ts, n_sel):
    nb, past, _ = past_sel.shape
    nk = min(past, 2048)
    wbuf = win_state.shape[1] // (2 * N_KV)
    jw = sel.shape[1]
    per_b = lambda r, w: pl.BlockSpec((None, r, w), lambda b, s: (b, 0, 0))
    return pl.pallas_call(
        functools.partial(_nsa_sample_sweep_kernel, nk=nk, ts=ts, past=past, n_sel=n_sel, wbuf=wbuf),
        grid=(nb, past // nk),
        in_specs=[pl.BlockSpec((None, nk, 512), lambda b, s: (b, s, 0)),
                  per_b(ts, 1024), per_b(ts, LANES),
                  pl.BlockSpec((N_KV * ts, jw), lambda b, s: (b, 0)),
                  per_b(N_QH * ts, NSA_HD), per_b(ts * 4 * N_KV, NSA_HD), per_b(ts * 2 * N_KV, NSA_HD),
                  per_b(wbuf * 2 * N_KV, NSA_HD)],
        out_specs=per_b(ts, 1024),
        out_shape=jax.ShapeDtypeStruct((nb, ts, 1024), F32),
        scratch_shapes=[pltpu.VMEM((N_QH * ts, 1), F32), pltpu.VMEM((N_QH * ts, 1), F32),
                        pltpu.VMEM((N_QH * ts, NSA_HD), F32)],
        compiler_params=_cparams(("arbitrary", "arbitrary")),
    )(past_sel, qn_s.reshape(nb, ts, 1024), gates_s.reshape(nb, ts, LANES), sel,
      o_cmp, rows_s.reshape(nb, ts * 4 * N_KV, NSA_HD), win_s.reshape(nb, ts * 2 * N_KV, NSA_HD),
      win_state)


def _softplus(z):
    return jnp.maximum(z, 0.0) + jnp.log1p(jnp.exp(-jnp.abs(z)))


def _rwkv_kernel(rkv_ref, aux_ref, sh_rkv_ref, sh_aux_ref, s0_ref, mu_rkv_ref, mu_aux_ref,
                 vec_ref, ww_ref, wa_ref, wg_ref, o_ref, sfin_ref, s_scr, c_rkv, c_aux, *, cs, n_valid):
    ci = pl.program_id(1)
    n_pairs = s_scr.shape[0]
    dr = n_pairs * LANES

    @pl.when(ci == 0)
    def _():
        s_scr[...] = s0_ref[...]
        c_rkv[...] = sh_rkv_ref[...]
        c_aux[...] = sh_aux_ref[...]

    def shift_mix(x, carry, mu):
        first = lax.broadcasted_iota(jnp.int32, x.shape, 0) == 0
        prev = jnp.where(first, carry, pltpu.roll(x, 1, 0))
        return x + (prev - x) * mu

    rkv = rkv_ref[...]
    aux = aux_ref[...]
    xm = shift_mix(rkv, c_rkv[...], mu_rkv_ref[...])
    xa = shift_mix(aux, c_aux[...], mu_aux_ref[...])
    c_rkv[...] = rkv[cs - 1:cs, :]
    c_aux[...] = aux[cs - 1:cs, :]

    w0, a0, k_k, k_a = vec_ref[0:1, :], vec_ref[1:2, :], vec_ref[2:3, :], vec_ref[3:4, :]
    r_k, ln_w, ln_b = vec_ref[4:5, :], vec_ref[5:6, :], vec_ref[6:7, :]
    r = xm[:, :dr]
    k = xm[:, dr:2 * dr]
    v = xm[:, 2 * dr:]
    u = w0 + _mm(jnp.tanh(xa).astype(BF16), ww_ref[...])
    lw = -jnp.exp(-_softplus(-u) - 0.5)
    a = jax.nn.sigmoid(a0 + _mm(xa.astype(BF16), wa_ref[...]))
    gate = _mm(jax.nn.sigmoid(xa).astype(BF16), wg_ref[...])

    lane = lax.broadcasted_iota(jnp.int32, (1, LANES), 1)
    head0 = lane < RWKV_HD
    rr = lax.broadcasted_iota(jnp.int32, (LANES, LANES), 0)
    cc = lax.broadcasted_iota(jnp.int32, (LANES, LANES), 1)
    seg = jnp.where((rr // RWKV_HD) == (cc // RWKV_HD), 1.0, 0.0).astype(BF16)
    eye = jnp.where(rr == cc, 1.0, 0.0)

    def seg_sum(x):
        return jnp.concatenate(
            [_mm_lhs_exact(x[:, p * LANES:(p + 1) * LANES], seg) for p in range(n_pairs)], axis=1)

    kk = k * k_k
    kkn = kk * lax.rsqrt(seg_sum(kk * kk) + 1e-12)
    kh = k * (1.0 + (a - 1.0) * k_a)
    bb = kkn * a
    bonus = seg_sum(r * kh * r_k) * v
    if n_valid < cs:
        live = lax.broadcasted_iota(jnp.int32, (cs, 1), 0) < n_valid
        lw = jnp.where(live, lw, 0.0)
        kh = jnp.where(live, kh, 0.0)
        kkn = jnp.where(live, kkn, 0.0)
        bb = jnp.where(live, bb, 0.0)
        v = jnp.where(live, v, 0.0)

    t_r = lax.broadcasted_iota(jnp.int32, (cs, cs), 0)
    t_c = lax.broadcasted_iota(jnp.int32, (cs, cs), 1)
    tri = jnp.where(t_r >= t_c, 1.0, 0.0).astype(BF16)
    cum = _mm_rhs_exact(tri, lw)
    tot = cum[cs - 1:cs, :]
    e_inc = jnp.exp(cum)
    e_inv = jnp.exp(-cum)
    e_rem = jnp.exp(tot - cum)
    q_t = r * e_inc
    a_t = -kkn * jnp.exp(cum - lw)
    k_t = kh * e_inv
    b_t = bb * e_inv
    k_hat = kh * e_rem
    b_hat = bb * e_rem
    w_tot = jnp.exp(tot)

    s2 = 2 * cs
    sr = lax.broadcasted_iota(jnp.int32, (s2, s2), 0) % cs
    sc = lax.broadcasted_iota(jnp.int32, (s2, s2), 1) % cs
    strict = sr > sc
    incl = sr >= sc
    eye2 = jnp.where(lax.broadcasted_iota(jnp.int32, (s2, s2), 0)
                     == lax.broadcasted_iota(jnp.int32, (s2, s2), 1), 1.0, 0.0)

    def stack(x):
        return jnp.concatenate([jnp.where(head0, x, 0.0), jnp.where(head0, 0.0, x)], axis=0)

    n_dbl = int(math.log2(cs)) - 1
    zeros_s = jnp.zeros((s2, LANES), F32)
    prs = range(n_pairs)
    lanes_of = lambda x: [stack(x[:, p * LANES:(p + 1) * LANES]) for p in prs]
    a_s, q_s, k_s, b_s = lanes_of(a_t), lanes_of(q_t), lanes_of(k_t), lanes_of(b_t)
    v_s, kh_s, bh_s = lanes_of(v), lanes_of(k_hat), lanes_of(b_hat)
    gram = [_mmp(jnp.concatenate([a_s[p], q_s[p]], axis=0), jnp.concatenate([b_s[p], k_s[p]], axis=0),
                 NT, RW_PASSES["gram"]) for p in prs]
    a_ab = [jnp.where(strict, gram[p][:s2, :s2], 0.0) for p in prs]
    a_ak = [jnp.where(strict, gram[p][:s2, s2:], 0.0) for p in prs]
    b_rbk = [jnp.concatenate([jnp.where(incl, gram[p][s2:, :s2], 0.0),
                              jnp.where(incl, gram[p][s2:, s2:], 0.0)], axis=1) for p in prs]
    akv = [_mmp(a_ak[p], v_s[p], passes=RW_PASSES["mix"]) for p in prs]
    tinv = [eye2 + a_ab[p] for p in prs]
    apow = a_ab
    for _ in range(n_dbl):
        apow = [_mmp(apow[p], apow[p], passes=RW_PASSES["inv"]) for p in prs]
        tinv = [tinv[p] + _mmp(apow[p], tinv[p], passes=RW_PASSES["inv"]) for p in prs]
    au = [_mmp(tinv[p], jnp.concatenate([a_s[p], akv[p]], axis=1), passes=RW_PASSES["mix"])
          for p in prs]
    ry = [_mmp(b_rbk[p], jnp.concatenate([au[p], jnp.concatenate([zeros_s, v_s[p]], axis=1)], axis=0),
               passes=RW_PASSES["mix"]) for p in prs]
    m_p = [eye * w_tot[:, p * LANES:(p + 1) * LANES]
           + _mmp(au[p][:, :LANES], bh_s[p], TN, RW_PASSES["trans"]) for p in prs]
    n_p = [_mmp(jnp.concatenate([au[p][:, LANES:], v_s[p]], axis=0),
                jnp.concatenate([bh_s[p], kh_s[p]], axis=0), TN, RW_PASSES["trans"]) for p in prs]
    s_old = [s_scr[p] for p in prs]
    y_s = [_mmp(q_s[p] + ry[p][:, :LANES], s_old[p], NT, RW_PASSES["out"]) + ry[p][:, LANES:]
           for p in prs]
    for p in prs:
        s_scr[p] = _mmp(s_old[p], m_p[p], passes=RW_PASSES["state"]) + n_p[p]
    y = jnp.concatenate([y_s[p][:cs] + y_s[p][cs:] for p in prs], axis=1)

    mean = seg_sum(y) * (1.0 / RWKV_HD)
    dy = y - mean
    var = seg_sum(dy * dy) * (1.0 / RWKV_HD)
    yn = dy * lax.rsqrt(var + GN_EPS) * ln_w + ln_b
    o_ref[...] = ((yn + bonus) * gate).astype(o_ref.dtype)

    @pl.when(ci == pl.num_programs(1) - 1)
    def _():
        sfin_ref[...] = s_scr[...]


def _rwkv(p, shift_rkv, shift_aux, s0_pairs, mu_rkv, mu_aux, vecs, ww, wa, wg, n_seq, t_len, cs, n_valid, out_dtype):
    n_chunks = t_len // cs
    n_pairs = s0_pairs.shape[1]
    dr = n_pairs * LANES
    full = lambda shape: pl.BlockSpec(shape, lambda b, c: (0,) * len(shape))
    return pl.pallas_call(
        functools.partial(_rwkv_kernel, cs=cs, n_valid=n_valid),
        grid=(n_seq, n_chunks),
        in_specs=[pl.BlockSpec((cs, 3 * dr), lambda b, c: (b * n_chunks + c, P_RKV // (3 * dr))),
                  pl.BlockSpec((cs, P_AUX_W), lambda b, c: (b * n_chunks + c, P_AUX // P_AUX_W)),
                  pl.BlockSpec((None, 1, 3 * dr), lambda b, c: (b, 0, 0)),
                  pl.BlockSpec((None, 1, P_AUX_W), lambda b, c: (b, 0, 0)),
                  pl.BlockSpec((None, n_pairs, LANES, LANES), lambda b, c: (b, 0, 0, 0)),
                  full(mu_rkv.shape), full(mu_aux.shape), full(vecs.shape),
                  full(ww.shape), full(wa.shape), full(wg.shape)],
        out_specs=[pl.BlockSpec((cs, dr), lambda b, c: (b * n_chunks + c, 0)),
                   pl.BlockSpec((None, n_pairs, LANES, LANES), lambda b, c: (b, 0, 0, 0))],
        out_shape=[jax.ShapeDtypeStruct((n_seq * t_len, dr), out_dtype),
                   jax.ShapeDtypeStruct((n_seq, n_pairs, LANES, LANES), F32)],
        scratch_shapes=[pltpu.VMEM((n_pairs, LANES, LANES), F32),
                        pltpu.VMEM((1, 3 * dr), F32), pltpu.VMEM((1, P_AUX_W), F32)],
        compiler_params=_cparams(("arbitrary", "arbitrary")),
    )(p, p, shift_rkv, shift_aux, s0_pairs, mu_rkv, mu_aux, vecs, ww, wa, wg)


def _out_kernel(x_ref, m_ref, on_ref, or_ref, wn_ref, wr_ref, o_ref):
    mix = _mm(on_ref[...], wn_ref[...]) + _mm(or_ref[...], wr_ref[...])
    o_ref[...] = x_ref[...] + m_ref[5] * mix


def _out_proj(x, mod, o_nsa, o_rw, w_out, layer, tm, tiles_per_seq):
    m_rows, d = x.shape
    dn = o_nsa.shape[1]
    mr = mod.shape[2]
    return pl.pallas_call(
        _out_kernel,
        grid=(m_rows // tm,),
        in_specs=[pl.BlockSpec((tm, d), lambda i: (i, 0)),
                  pl.BlockSpec((None, N_MOD, mr, d), lambda i: (i // tiles_per_seq, 0, 0, 0)),
                  pl.BlockSpec((tm, dn), lambda i: (i, 0)),
                  pl.BlockSpec((tm, dn), lambda i: (i, 0)),
                  pl.BlockSpec((None, dn, d), lambda i: (layer, 0, 0)),
                  pl.BlockSpec((None, dn, d), lambda i: (layer, 1, 0))],
        out_specs=pl.BlockSpec((tm, d), lambda i: (i, 0)),
        out_shape=jax.ShapeDtypeStruct((m_rows, d), F32),
        compiler_params=_cparams(("arbitrary",)),
    )(x, mod, o_nsa, o_rw, w_out, w_out)


def _pad_cols(x, n):
    return jnp.pad(x, [(0, 0)] * (x.ndim - 1) + [(0, n)])


def _reorder_cols(w, d_rwkv, n_gate):
    nsa_main = P_AUX
    rw0 = nsa_main + n_gate
    lora0 = rw0 + 3 * d_rwkv
    n_lora = w.shape[-1] - lora0
    aux = jnp.concatenate([_pad_cols(w[..., nsa_main:rw0], AUX_WD - n_gate),
                           _pad_cols(w[..., lora0:], P_AUX_W - AUX_WD - n_lora)], axis=-1)
    return jnp.concatenate([w[..., :nsa_main], aux, w[..., rw0:lora0]], axis=-1)


def _pairs_from_heads(s):
    n, h = s.shape[:2]
    s = s.reshape(n, h // 2, 2, RWKV_HD, RWKV_HD)
    z = jnp.zeros_like(s[:, :, 0])
    top = jnp.concatenate([s[:, :, 0], z], axis=-1)
    bot = jnp.concatenate([z, s[:, :, 1]], axis=-1)
    return jnp.concatenate([top, bot], axis=-2)


def _heads_from_pairs(s):
    n, hp = s.shape[:2]
    a = s[:, :, :RWKV_HD, :RWKV_HD]
    b = s[:, :, RWKV_HD:, RWKV_HD:]
    return jnp.stack([a, b], axis=2).reshape(n, 2 * hp, RWKV_HD, RWKV_HD)


def kernel(x_prompt, x_sample, cache_nsa_kv, state_win_kv, state_wkv, state_shift, page_table,
           c_prompt, c_sample, w_ada, b_ada, norm_g, ffn_wi, ffn_wo, w_in, w_out,
           q_norm_g, k_norm_g, cmp_pe, cmp_w1, cmp_b1, cmp_w2,
           rwkv_mu, rwkv_w0, rwkv_w_w2, rwkv_a0, rwkv_w_a2, rwkv_w_g2,
           rwkv_k_k, rwkv_k_a, rwkv_r_k, rwkv_ln_w, rwkv_ln_b):
    bp, tp, d = x_prompt.shape
    bs, ts, _ = x_sample.shape
    depth = w_ada.shape[0]
    n_pool = cache_nsa_kv.shape[1]
    n_pages = page_table.shape[1]
    past = n_pages * PAGE
    wbuf = state_win_kv.shape[2]
    d_rwkv = rwkv_w0.shape[1]
    n_heads = d_rwkv // RWKV_HD
    d_nsa = N_QH * NSA_HD
    n_gate = 3 * N_QH
    n_dlora = rwkv_w_w2.shape[1]
    n_alora = rwkv_w_a2.shape[1]
    n_glora = rwkv_w_g2.shape[1]
    mp_rows, ms_rows = bp * tp, bs * ts
    tm = 512
    tiles_per_seq = tp // tm
    cs = 64
    ts_pad = -(-ts // cs) * cs
    win_keep = min(WINDOW, tp)

    c_all = jnp.concatenate([c_prompt, c_sample], axis=0)
    c_rows = -(-c_all.shape[0] // 8) * 8
    c_all = jnp.pad(c_all, ((0, c_rows - c_all.shape[0]), (0, 0)))
    mod = _modulation(c_all, w_ada, b_ada).reshape(depth, c_rows, N_MOD, d)
    mod_p = mod[:, :bp].reshape(depth, bp, N_MOD, 1, d)
    mod_s = jnp.repeat(mod[:, bp:bp + bs].transpose(0, 2, 1, 3), ts, axis=2).reshape(depth, 1, N_MOD, ms_rows, d)

    xp = x_prompt.reshape(mp_rows, d)
    xs = x_sample.reshape(ms_rows, d)
    pool = cache_nsa_kv.reshape(depth * n_pool * PAGE * 4 * N_KV, NSA_HD)
    prompt_pages = jnp.arange(bp * (tp // PAGE), dtype=jnp.int32).reshape(bp, tp // PAGE)
    n_sel_s = -(-(past + ts) // SEL_BLOCK)

    kv_p, kv_s, win_p, win_s, wkv_p, wkv_s, sh_p, sh_s = [], [], [], [], [], [], [], []
    wi_all = ffn_wi.astype(BF16)
    wo_all = ffn_wo.astype(BF16)
    w_in_all = _reorder_cols(w_in, d_rwkv, n_gate).astype(BF16)
    w_out_all = w_out.astype(BF16)
    for l in range(depth):
        g_rows = [norm_g[l, i].reshape(1, d) for i in range(3)]
        q_g = q_norm_g[l].reshape(1, NSA_HD)
        k_g = k_norm_g[l]
        w1 = cmp_w1[l].reshape(2, 2, CMP_STRIDE * NSA_HD, CMP_HID)
        w1cat = jnp.concatenate([w1[:, 0], w1[:, 1]], axis=-1).astype(BF16)
        pe2 = cmp_pe[l].reshape(2, 2, 1, CMP_STRIDE * NSA_HD)
        b1 = cmp_b1[l].reshape(2, 1, CMP_HID)
        w2 = cmp_w2[l].astype(BF16)
        def shift_parts(sh):
            aux = jnp.pad(sh[:, 3 * d_rwkv:], ((0, 0), (AUX_WD, P_AUX_W - AUX_WD - (sh.shape[1] - 3 * d_rwkv))))
            return sh[:, None, :3 * d_rwkv], aux[:, None, :]

        mu_rkv, mu_aux = (m[:, 0] for m in shift_parts(rwkv_mu[l].reshape(1, -1)))
        vecs = jnp.stack([rwkv_w0[l], rwkv_a0[l], rwkv_k_k[l], rwkv_k_a[l], rwkv_r_k[l].reshape(-1),
                          rwkv_ln_w[l], rwkv_ln_b[l], jnp.zeros_like(rwkv_w0[l])])
        lora_rows = lambda w, off: jnp.pad(w, ((off, P_AUX_W - off - w.shape[0]), (0, 0))).astype(BF16)
        ww = lora_rows(rwkv_w_w2[l], AUX_WD)
        wa = lora_rows(rwkv_w_a2[l], AUX_WD + n_dlora)
        wg = lora_rows(rwkv_w_g2[l], AUX_WD + n_dlora + n_alora)

        def shift_out(p_last):
            return jnp.concatenate([p_last[:, P_RKV:P_RKV + 3 * d_rwkv],
                                    p_last[:, P_AUX + AUX_WD:P_AUX + AUX_WD + n_dlora + n_alora + n_glora]],
                                   axis=1)

        xp = _ffn(xp, mod_p[l], g_rows[0], wi_all, wo_all, l, 0, 0, tm, tiles_per_seq)
        xs = _ffn(xs, mod_s[l], g_rows[0], wi_all, wo_all, l, 0, 0, ms_rows, 1)

        pp = _proj(xp, mod_p[l], g_rows[1], w_in_all, l, tm, tiles_per_seq)
        qn, rows, win, selbf, winbf, gates = _nsa_prep(pp, q_g, k_g, tm, BF16)
        kc, vc = _compress(rows, prompt_pages, w1cat, pe2, b1, w2, k_g, False)
        o_nsa = _nsa_prompt(qn, gates, selbf, winbf, kc, vc, bp, tp)
        zero_rkv, zero_aux = shift_parts(jnp.zeros((bp, state_shift.shape[2]), F32))
        o_rw, s_fin = _rwkv(pp, zero_rkv, zero_aux, jnp.zeros((bp, n_heads // 2, LANES, LANES), F32),
                            mu_rkv, mu_aux, vecs, ww, wa, wg, bp, tp, cs, cs, BF16)
        xp = _out_proj(xp, mod_p[l], o_nsa, o_rw, w_out_all, l, tm, tiles_per_seq)
        kv_p.append(rows.reshape(bp, tp // PAGE, PAGE, 4, N_KV, NSA_HD))
        win_p.append(win.reshape(bp, tp, 2, N_KV, NSA_HD)[:, tp - win_keep:])
        wkv_p.append(_heads_from_pairs(s_fin))
        sh_p.append(shift_out(pp.reshape(bp, tp, P_COLS)[:, -1]))

        ps = _proj(xs, mod_s[l], g_rows[1], w_in_all, l, ms_rows, 1)
        qn_s, rows_s, win_new, _, _, gates_s = _nsa_prep(ps, q_g, k_g, ms_rows, F32)
        page_idx = page_table + l * n_pool
        kc_s, vc_s, past_sel = _compress(pool, page_idx, w1cat, pe2, b1, w2, k_g, True)
        o_cmp, sel = _nsa_sample_select(qn_s, kc_s, vc_s, ts, past, n_sel_s)
        o_nsa_s = _nsa_sample_sweep(past_sel, qn_s, gates_s, sel, o_cmp, rows_s, win_new,
                                    state_win_kv[l].reshape(bs, wbuf * 2 * N_KV, NSA_HD), ts, n_sel_s)
        sh_rkv, sh_aux = shift_parts(state_shift[l])
        ps_pad = jnp.pad(ps.reshape(bs, ts, P_COLS), ((0, 0), (0, ts_pad - ts), (0, 0)))
        o_rw_s, s_fin_s = _rwkv(ps_pad.reshape(bs * ts_pad, P_COLS), sh_rkv, sh_aux,
                                _pairs_from_heads(state_wkv[l]), mu_rkv, mu_aux, vecs, ww, wa, wg,
                                bs, ts_pad, ts_pad, ts, F32)
        o_rw_s = o_rw_s.reshape(bs, ts_pad, d_rwkv)[:, :ts].reshape(ms_rows, d_rwkv)
        xs = _out_proj(xs, mod_s[l], o_nsa_s.reshape(ms_rows, d_nsa).astype(BF16), o_rw_s.astype(BF16),
                       w_out_all, l, ms_rows, 1)
        kv_s.append(rows_s.reshape(bs, ts, 4, N_KV, NSA_HD))
        win_s.append(jnp.concatenate([state_win_kv[l][:, ts:],
                                      win_new.reshape(bs, ts, 2, N_KV, NSA_HD)], axis=1))
        wkv_s.append(_heads_from_pairs(s_fin_s))
        sh_s.append(shift_out(ps.reshape(bs, ts, P_COLS)[:, -1]))

        xp = _ffn(xp, mod_p[l], g_rows[2], wi_all, wo_all, l, 1, 2, tm, tiles_per_seq)
        xs = _ffn(xs, mod_s[l], g_rows[2], wi_all, wo_all, l, 1, 2, ms_rows, 1)

    return (xp.reshape(bp, tp, d), xs.reshape(bs, ts, d),
            jnp.stack(kv_p), jnp.stack(kv_s), jnp.stack(win_p), jnp.stack(win_s),
            jnp.stack(wkv_p), jnp.stack(wkv_s), jnp.stack(sh_p), jnp.stack(sh_s))
```

```python
import functools
import math

import jax
import jax.numpy as jnp
from jax import lax
from jax.experimental import pallas as pl
from jax.experimental.pallas import tpu as pltpu

F32 = jnp.float32
BF16 = jnp.bfloat16

NSA_HD = 128
N_KV = 2
QPG = 4
N_QH = N_KV * QPG
CMP_LEN = 32
CMP_STRIDE = 16
CMP_HID = 2 * NSA_HD
SEL_BLOCK = 64
SEL_TOPK = 16
N_LOCAL = 2
WINDOW = 512
PAGE = 128
RWKV_HD = 64
N_MOD = 9
NORM_EPS = 1e-6
GN_EPS = 64e-5
NEG = -1e30
FORCED = 1e6
INVALID = -1e6

LANES = 128
VMEM_LIMIT = 56 * 1024 * 1024
PAGES_PER_STEP = 8

P_Q = 0
P_KV = 1024
P_AUX = 2560
P_AUX_W = 512
P_RKV = 3072
P_COLS = 6144
AUX_WD = 128
AUX_AD = 192
AUX_GD = 256


def _cparams(sem):
    return pltpu.CompilerParams(dimension_semantics=sem, vmem_limit_bytes=VMEM_LIMIT)


def _mm(a, b, dims=((1,), (0,))):
    return lax.dot_general(a, b, (dims, ((), ())), preferred_element_type=F32)


NT = ((1,), (1,))
TN = ((0,), (0,))


def _split2(x):
    hi = x.astype(BF16)
    lo = (x - hi.astype(F32)).astype(BF16)
    return hi, lo


def _split3(x):
    hi = x.astype(BF16)
    r1 = x - hi.astype(F32)
    mid = r1.astype(BF16)
    lo = (r1 - mid.astype(F32)).astype(BF16)
    return hi, mid, lo


def _mm3(a, b, dims=((1,), (0,))):
    ah, al = _split2(a)
    bh, bl = _split2(b)
    return _mm(ah, bh, dims) + (_mm(ah, bl, dims) + _mm(al, bh, dims))


def _mmp(a, b, dims=((1,), (0,)), passes=1):
    if passes == 3:
        return _mm3(a, b, dims)
    return _mm(a.astype(BF16), b.astype(BF16), dims)


RW_PASSES = {"gram": 1, "inv": 1, "mix": 1, "trans": 1, "out": 1, "state": 1}


def _mm_lhs_exact(a, b_bf16, dims=((1,), (0,))):
    a1, a2, a3 = _split3(a)
    return _mm(a1, b_bf16, dims) + (_mm(a2, b_bf16, dims) + _mm(a3, b_bf16, dims))


def _mm_rhs_exact(a_bf16, b, dims=((1,), (0,))):
    b1, b2, b3 = _split3(b)
    return _mm(a_bf16, b1, dims) + (_mm(a_bf16, b2, dims) + _mm(a_bf16, b3, dims))


def _silu(x):
    return x * jax.nn.sigmoid(x)


def _rms(x, g):
    return x * lax.rsqrt(jnp.mean(x * x, axis=-1, keepdims=True) + NORM_EPS) * g


def _ada_norm(x, m_ref, slot, g):
    return _rms(x, g) * (1.0 + m_ref[3 * slot + 1]) + m_ref[3 * slot]


def _mod_kernel(c_ref, w_ref, b_ref, o_ref):
    s = _silu(c_ref[...]).astype(BF16)
    o_ref[0] = _mm(s, w_ref[0].astype(BF16)) + b_ref[0]


def _modulation(c_all, w_ada, b_ada):
    depth, d, n = w_ada.shape
    rows = c_all.shape[0]
    tn = 1024
    return pl.pallas_call(
        _mod_kernel,
        grid=(depth, n // tn),
        in_specs=[pl.BlockSpec((rows, d), lambda l, j: (0, 0)),
                  pl.BlockSpec((1, d, tn), lambda l, j: (l, 0, j)),
                  pl.BlockSpec((1, 1, tn), lambda l, j: (l, 0, j))],
        out_specs=pl.BlockSpec((1, rows, tn), lambda l, j: (l, 0, j)),
        out_shape=jax.ShapeDtypeStruct((depth, rows, n), F32),
        compiler_params=_cparams(("arbitrary", "arbitrary")),
    )(c_all, w_ada, b_ada.reshape(depth, 1, n))


def _ffn_kernel(x_ref, m_ref, g_ref, wg_ref, wu_ref, wo_ref, o_ref, h_scr, acc_scr, *, slot):
    f = pl.program_id(1)

    @pl.when(f == 0)
    def _():
        h_scr[...] = _ada_norm(x_ref[...], m_ref, slot, g_ref[...]).astype(BF16)
        acc_scr[...] = jnp.zeros_like(acc_scr)

    h = h_scr[...]
    gate = _mm(h, wg_ref[...].astype(BF16))
    up = _mm(h, wu_ref[...].astype(BF16))
    act = (_silu(gate) * up).astype(BF16)
    acc_scr[...] += _mm(act, wo_ref[...].astype(BF16))

    @pl.when(f == pl.num_programs(1) - 1)
    def _():
        o_ref[...] = x_ref[...] + 0.5 * m_ref[3 * slot + 2] * acc_scr[...]


def _ffn(x, mod, g, wi, wo, layer, half, slot, tm, tiles_per_seq):
    m_rows, d = x.shape
    d_ff = wo.shape[2]
    tf = 256
    nf = d_ff // tf
    mr = mod.shape[2]
    single = pl.Buffered(1)
    return pl.pallas_call(
        functools.partial(_ffn_kernel, slot=slot),
        grid=(m_rows // tm, nf),
        in_specs=[pl.BlockSpec((tm, d), lambda i, f: (i, 0), pipeline_mode=single),
                  pl.BlockSpec((None, N_MOD, mr, d), lambda i, f: (i // tiles_per_seq, 0, 0, 0)),
                  pl.BlockSpec((1, d), lambda i, f: (0, 0)),
                  pl.BlockSpec((None, None, d, tf), lambda i, f: (layer, half, 0, f)),
                  pl.BlockSpec((None, None, d, tf), lambda i, f: (layer, half, 0, nf + f)),
                  pl.BlockSpec((None, None, tf, d), lambda i, f: (layer, half, f, 0))],
        out_specs=pl.BlockSpec((tm, d), lambda i, f: (i, 0), pipeline_mode=single),
        out_shape=jax.ShapeDtypeStruct((m_rows, d), F32),
        scratch_shapes=[pltpu.VMEM((tm, d), BF16), pltpu.VMEM((tm, d), F32)],
        compiler_params=_cparams(("arbitrary", "arbitrary")),
    )(x, mod, g, wi, wi, wo)


def _proj_kernel(x_ref, m_ref, g_ref, w_ref, o_ref, h_scr):
    @pl.when(pl.program_id(1) == 0)
    def _():
        h_scr[...] = _ada_norm(x_ref[...], m_ref, 1, g_ref[...]).astype(BF16)

    o_ref[...] = _mm(h_scr[...], w_ref[...])


def _proj(x, mod, g, w, layer, tm, tiles_per_seq):
    m_rows, d = x.shape
    n = w.shape[2]
    tn = 1536
    mr = mod.shape[2]
    return pl.pallas_call(
        _proj_kernel,
        grid=(m_rows // tm, n // tn),
        in_specs=[pl.BlockSpec((tm, d), lambda i, j: (i, 0)),
                  pl.BlockSpec((None, N_MOD, mr, d), lambda i, j: (i // tiles_per_seq, 0, 0, 0)),
                  pl.BlockSpec((1, d), lambda i, j: (0, 0)),
                  pl.BlockSpec((None, d, tn), lambda i, j: (layer, 0, j))],
        out_specs=pl.BlockSpec((tm, tn), lambda i, j: (i, j)),
        out_shape=jax.ShapeDtypeStruct((m_rows, n), F32),
        scratch_shapes=[pltpu.VMEM((tm, d), BF16)],
        compiler_params=_cparams(("arbitrary", "arbitrary")),
    )(x, mod, g, w)


def _rms_heads(x, g):
    outs = []
    for h in range(x.shape[1] // NSA_HD):
        outs.append(_rms(x[:, h * NSA_HD:(h + 1) * NSA_HD], g))
    return jnp.concatenate(outs, axis=1)


def _nsa_prep_kernel(q_ref, kva_ref, kvb_ref, kvc_ref, gt_ref, qg_ref, kg_ref,
                     qn_ref, rows_ref, win_ref, selbf_ref, winbf_ref, gates_ref):
    qn_ref[...] = (_rms_heads(q_ref[...], qg_ref[...]) * (NSA_HD ** -0.5)).astype(qn_ref.dtype)
    kvb = kvb_ref[...]
    ksel = _rms_heads(kvb[:, :2 * NSA_HD], kg_ref[1:2, :])
    selrows = jnp.concatenate([ksel, kvb[:, 2 * NSA_HD:]], axis=1)
    rows = jnp.concatenate([kva_ref[...], selrows], axis=1)
    tm = rows.shape[0]
    for c in range(4 * N_KV):
        rows_ref[pl.ds(c, tm, stride=4 * N_KV), :] = rows[:, c * NSA_HD:(c + 1) * NSA_HD]
    selbf_ref[...] = selrows.astype(selbf_ref.dtype)
    kvc = kvc_ref[...]
    kwin = _rms_heads(kvc[:, :2 * NSA_HD], kg_ref[2:3, :])
    winrows = jnp.concatenate([kwin, kvc[:, 2 * NSA_HD:]], axis=1)
    for c in range(2 * N_KV):
        win_ref[pl.ds(c, tm, stride=2 * N_KV), :] = winrows[:, c * NSA_HD:(c + 1) * NSA_HD]
    winbf_ref[...] = winrows.astype(winbf_ref.dtype)
    gates_ref[...] = jax.nn.sigmoid(gt_ref[...])


def _nsa_prep(p, q_g, k_g, tm, act_dtype):
    m_rows = p.shape[0]
    row = lambda w, j: pl.BlockSpec((tm, w), lambda i: (i, j))
    return pl.pallas_call(
        _nsa_prep_kernel,
        grid=(m_rows // tm,),
        in_specs=[row(1024, 0), row(512, 2), row(512, 3), row(512, 4), row(LANES, P_AUX // LANES),
                  pl.BlockSpec((1, NSA_HD), lambda i: (0, 0)),
                  pl.BlockSpec((3, NSA_HD), lambda i: (0, 0))],
        out_specs=[row(1024, 0), pl.BlockSpec((tm * 8, NSA_HD), lambda i: (i, 0)),
                   pl.BlockSpec((tm * 4, NSA_HD), lambda i: (i, 0)),
                   row(512, 0), row(512, 0), row(LANES, 0)],
        out_shape=[jax.ShapeDtypeStruct((m_rows, 1024), act_dtype),
                   jax.ShapeDtypeStruct((m_rows * 8, NSA_HD), F32),
                   jax.ShapeDtypeStruct((m_rows * 4, NSA_HD), F32),
                   jax.ShapeDtypeStruct((m_rows, 512), act_dtype),
                   jax.ShapeDtypeStruct((m_rows, 512), act_dtype),
                   jax.ShapeDtypeStruct((m_rows, LANES), F32)],
        compiler_params=_cparams(("arbitrary",)),
    )(p, p, p, p, p, q_g, k_g)


def _cmp_kernel(pt_ref, *refs, n_sub, pp, emit_sel):
    pages = refs[:pp]
    if emit_sel:
        w1_ref, pe_ref, b1_ref, w2_ref, kg_ref, kc_ref, vc_ref, sel_ref, x_scr = refs[pp:]
    else:
        w1_ref, pe_ref, b1_ref, w2_ref, kg_ref, kc_ref, vc_ref, x_scr = refs[pp:]
    p = pl.program_id(1)
    spp = PAGE // CMP_STRIDE
    base = pl.multiple_of(p * (pp * spp), pp * spp)
    heads = 4 * N_KV
    by_head = [pltpu.einshape("(tc)d->ctd", pg[...], c=heads) for pg in pages]
    if emit_sel:
        for c in range(2 * N_KV):
            sel_ref[:, c * NSA_HD:(c + 1) * NSA_HD] = jnp.concatenate(
                [h[2 * N_KV + c] for h in by_head], axis=0).astype(BF16)
    for c in range(2 * N_KV):
        x_scr[c, pl.ds(base, pp * spp), :] = jnp.concatenate(
            [pltpu.einshape("(sl)d->s(ld)", h[c], l=CMP_STRIDE) for h in by_head], axis=0).astype(BF16)

    @pl.when(p == pl.num_programs(1) - 1)
    def _():
        row = lax.broadcasted_iota(jnp.int32, (n_sub, NSA_HD), 0)
        for kv in range(2):
            w1 = w1_ref[kv]
            const = b1_ref[kv]
            for j in range(2):
                pe = jnp.broadcast_to(pe_ref[kv, j], (8, CMP_STRIDE * NSA_HD)).astype(BF16)
                const = const + _mm(pe, w1[:, j * CMP_HID:(j + 1) * CMP_HID])[0:1]
            for g in range(N_KV):
                ab = _mm(x_scr[kv * 2 + g], w1)
                nxt = pltpu.roll(ab[:, CMP_HID:], n_sub - 1, 0)
                acc = ab[:, :CMP_HID] + nxt + const
                o = _mm(jax.nn.gelu(acc).astype(BF16), w2_ref[kv])
                if kv == 0:
                    o = _rms(o, kg_ref[0:1, :])
                o = jnp.where(row < n_sub - 1, o, 0.0)
                if kv == 0:
                    kc_ref[g] = o
                else:
                    vc_ref[g] = o


def _compress(pool, page_idx, w1cat, pe2, b1, w2, k_g, emit_sel):
    nb, n_pages = page_idx.shape
    n_sub = n_pages * (PAGE // CMP_STRIDE)
    pp = PAGES_PER_STEP
    page = lambda k: pl.BlockSpec((PAGE * 4 * N_KV, NSA_HD), lambda b, p, pt: (pt[b, pp * p + k], 0))
    page_specs = [page(k) for k in range(pp)]
    full = lambda shape: pl.BlockSpec(shape, lambda b, p, pt: (0,) * len(shape))
    out = pl.BlockSpec((None, N_KV, n_sub, NSA_HD), lambda b, p, pt: (b, 0, 0, 0))
    out_specs = [out, out]
    out_shape = [jax.ShapeDtypeStruct((nb, N_KV, n_sub, NSA_HD), F32)] * 2
    if emit_sel:
        out_specs.append(pl.BlockSpec((None, pp * PAGE, 512), lambda b, p, pt: (b, p, 0)))
        out_shape.append(jax.ShapeDtypeStruct((nb, n_pages * PAGE, 512), BF16))
    return pl.pallas_call(
        functools.partial(_cmp_kernel, n_sub=n_sub, pp=pp, emit_sel=emit_sel),
        grid_spec=pltpu.PrefetchScalarGridSpec(
            num_scalar_prefetch=1,
            grid=(nb, n_pages // pp),
            in_specs=page_specs + [full(w1cat.shape), full(pe2.shape), full(b1.shape),
                      full(w2.shape), full(k_g.shape)],
            out_specs=out_specs,
            scratch_shapes=[pltpu.VMEM((4, n_sub, CMP_STRIDE * NSA_HD), BF16)]),
        out_shape=out_shape,
        compiler_params=_cparams(("arbitrary", "arbitrary")),
    )(page_idx, *([pool] * pp), w1cat, pe2, b1, w2, k_g)


def _imp_matrix(n_sub, n_cmp, width):
    n = lax.broadcasted_iota(jnp.int32, (n_sub, width), 0)
    j = lax.broadcasted_iota(jnp.int32, (n_sub, width), 1)
    spb = SEL_BLOCK // CMP_STRIDE
    m = jnp.where(n // spb == j, 1.0, 0.0) + jnp.where((n + 1) // spb == j, 1.0, 0.0)
    return jnp.where(n < n_cmp, m, 0.0).astype(BF16)


def _topk_mask_t(st_scr, n_iter, width):
    jj = st_scr.shape[0]
    st = st_scr[...]
    jrow = lax.broadcasted_iota(jnp.int32, (jj, width), 0)

    def body(jp, rank):
        r = st_scr[pl.ds(jp, 1), :]
        beats = (r > st) | ((r == st) & (jp < jrow))
        return rank + jnp.where(beats, 1.0, 0.0)

    return lax.fori_loop(0, n_iter, body, jnp.zeros((jj, width), F32))


def _softmax_rows(s, ok):
    sm = jnp.where(ok, s, NEG)
    mx = jnp.max(sm, axis=-1, keepdims=True)
    e = jnp.where(ok, jnp.exp(sm - mx), 0.0)
    den = jnp.sum(e, axis=-1, keepdims=True)
    return e / jnp.where(den > 0.0, den, 1.0)


def _nsa_prompt_kernel(q_ref, gt_ref, sel_ref, win_ref, kc_ref, vc_ref, oh_ref, o_ref, st_scr,
                       *, t_len, n_sub, ck):
    tq = SEL_BLOCK
    qi = pl.program_id(1)
    s0 = qi * tq
    n_cmp = n_sub - 1
    rows = QPG * tq
    q = q_ref[...]
    rowpos = s0 + lax.broadcasted_iota(jnp.int32, (rows, 1), 0) % tq

    def q_group(g):
        return jnp.concatenate(
            [q[:, (g * QPG + h) * NSA_HD:(g * QPG + h + 1) * NSA_HD] for h in range(QPG)], axis=0)

    imp_m = _imp_matrix(n_sub, n_cmp, LANES)
    n_idx = lax.broadcasted_iota(jnp.int32, (rows, n_sub), 1)
    ok_c = (n_idx * CMP_STRIDE + (CMP_LEN - 1) <= rowpos) & (n_idx < n_cmp)
    o_cmp, scores = [], []
    jl = lax.broadcasted_iota(jnp.int32, (tq, LANES), 1)
    forced = (jl == 0) | ((jl <= qi) & (jl > qi - N_LOCAL))
    for g in range(N_KV):
        qg = q_group(g)
        s_c = _mm(qg, kc_ref[g].astype(BF16), NT)
        p_c = _softmax_rows(s_c, ok_c)
        o_cmp.append(_mm(p_c.astype(BF16), vc_ref[g].astype(BF16)))
        p_g = p_c[0:tq] + p_c[tq:2 * tq] + p_c[2 * tq:3 * tq] + p_c[3 * tq:4 * tq]
        imp = _mm_lhs_exact(p_g, imp_m)
        scores.append(jnp.where(jl <= qi, jnp.where(forced, FORCED, imp), INVALID))
    jj = st_scr.shape[0]
    st_scr[...] = jnp.concatenate(scores, axis=0).T[:jj]
    rank = _topk_mask_t(st_scr, qi + 1, N_KV * tq)
    jrow = lax.broadcasted_iota(jnp.int32, (jj, N_KV * tq), 0)
    sel_t = jnp.where((rank < SEL_TOPK) & (jrow <= qi), 1.0, 0.0)
    if jj < LANES:
        sel_t = jnp.concatenate([sel_t, jnp.zeros((LANES - jj, N_KV * tq), F32)], axis=0)
    sel = sel_t.T

    qgs = [q_group(g) for g in range(N_KV)]
    q_aug = []
    for g in range(N_KV):
        sg = jnp.concatenate([sel[g * tq:(g + 1) * tq]] * QPG, axis=0)
        q_aug.append(jnp.concatenate([qgs[g], ((sg - 1.0) * (-NEG)).astype(BF16)], axis=1))
    tokpos = s0 + lax.broadcasted_iota(jnp.int32, (1, rows), 1) % tq

    def chunks(cs, causal):
        scores_t, vals = [], []
        for c, cz in zip(cs, causal):
            k0 = pl.multiple_of(c * ck, ck)
            keypos = k0 + lax.broadcasted_iota(jnp.int32, (ck, 1), 0)
            onehot = oh_ref[pl.ds(k0, ck), :]
            for g in range(N_KV):
                kk = sel_ref[pl.ds(k0, ck), g * NSA_HD:(g + 1) * NSA_HD]
                s = _mm(jnp.concatenate([kk, onehot], axis=1), q_aug[g], NT)
                scores_t.append(jnp.where(keypos <= tokpos, s, NEG) if cz else s)
                vals.append(sel_ref[pl.ds(k0, ck), (N_KV + g) * NSA_HD:(N_KV + g + 1) * NSA_HD])
        maxes = [jnp.max(s, axis=0, keepdims=True) for s in scores_t]
        probs = [jnp.exp(s - m) for s, m in zip(scores_t, maxes)]
        sums = [jnp.sum(p, axis=0, keepdims=True) for p in probs]
        accs = [_mm(v, p.astype(BF16), TN) for v, p in zip(vals, probs)]
        parts = list(zip(maxes, sums, accs))
        return [tuple(parts[i * N_KV:(i + 1) * N_KV]) for i in range(len(cs))]

    def merge(carry, *chunks):
        new = []
        for g in range(N_KV):
            m_i, l_i, acc = carry[g]
            m_new = m_i
            for ch in chunks:
                m_new = jnp.maximum(m_new, ch[g][0])
            w_i = jnp.exp(m_i - m_new)
            l_new, acc_new = w_i * l_i, w_i * acc
            for ch in chunks:
                w_c = jnp.exp(ch[g][0] - m_new)
                l_new = l_new + w_c * ch[g][1]
                acc_new = acc_new + w_c * ch[g][2]
            new.append((m_new, l_new, acc_new))
        return tuple(new)

    neutral = tuple((jnp.full((1, rows), NEG, F32), jnp.zeros((1, rows), F32),
                     jnp.zeros((NSA_HD, rows), F32)) for _ in range(N_KV))
    c_diag = s0 // ck
    n_pairs = c_diag // 2
    mid = lax.fori_loop(
        0, n_pairs, lambda i, carry: merge(carry, *chunks([2 * i, 2 * i + 1], [False, False])), neutral)
    fin = lax.cond(c_diag % 2 == 1,
                   lambda: merge(mid, *chunks([c_diag - 1, c_diag], [False, True])),
                   lambda: merge(mid, *chunks([c_diag], [True])))

    wlen = WINDOW + 2 * tq
    w0 = pl.multiple_of(jnp.clip(s0 - WINDOW, 0, t_len - wlen), tq)
    wpos = w0 + lax.broadcasted_iota(jnp.int32, (wlen, 1), 0)
    dist = tokpos - wpos
    bias_w = jnp.where((dist >= 0) & (dist < WINDOW), 0.0, NEG)
    s_ws = [_mm(win_ref[pl.ds(w0, wlen), g * NSA_HD:(g + 1) * NSA_HD], qgs[g], NT) + bias_w
            for g in range(N_KV)]
    e_ws = [jnp.exp(s - jnp.max(s, axis=0, keepdims=True)) for s in s_ws]
    o_ws = [(_mm(win_ref[pl.ds(w0, wlen), (N_KV + g) * NSA_HD:(N_KV + g + 1) * NSA_HD],
                 e_ws[g].astype(BF16), TN) / jnp.sum(e_ws[g], axis=0, keepdims=True)).T
            for g in range(N_KV)]
    gt = gt_ref[...]
    outs = []
    for g in range(N_KV):
        o_w = o_ws[g]
        _, l_i, acc = fin[g]
        o_s = (acc / l_i).T
        for h in range(QPG):
            c0 = (g * QPG + h) * 3
            r0 = slice(h * tq, (h + 1) * tq)
            outs.append(gt[:, c0:c0 + 1] * o_cmp[g][r0] + gt[:, c0 + 1:c0 + 2] * o_s[r0]
                        + gt[:, c0 + 2:c0 + 3] * o_w[r0])
    o_ref[...] = jnp.concatenate(outs, axis=1).astype(o_ref.dtype)


def _nsa_prompt(qn, gates, selbf, winbf, kc, vc, nb, t_len):
    n_sub = kc.shape[2]
    tq = SEL_BLOCK
    nq = t_len // tq
    ck = 512
    n_blocks = -(-(t_len // SEL_BLOCK) // 8) * 8
    assert n_blocks <= LANES
    block_onehot = (jnp.arange(t_len)[:, None] // SEL_BLOCK == jnp.arange(LANES)[None, :]).astype(BF16)
    per_b = lambda w: pl.BlockSpec((None, t_len, w), lambda b, i: (b, 0, 0))
    cmp_spec = pl.BlockSpec((None, N_KV, n_sub, NSA_HD), lambda b, i: (b, 0, 0, 0))
    return pl.pallas_call(
        functools.partial(_nsa_prompt_kernel, t_len=t_len, n_sub=n_sub, ck=ck),
        grid=(nb, nq),
        in_specs=[pl.BlockSpec((tq, 1024), lambda b, i: (b * nq + i, 0)),
                  pl.BlockSpec((tq, LANES), lambda b, i: (b * nq + i, 0)),
                  per_b(512), per_b(512), cmp_spec, cmp_spec,
                  pl.BlockSpec((t_len, LANES), lambda b, i: (0, 0))],
        out_specs=pl.BlockSpec((tq, 1024), lambda b, i: (b * nq + i, 0)),
        out_shape=jax.ShapeDtypeStruct((nb * t_len, 1024), BF16),
        scratch_shapes=[pltpu.VMEM((n_blocks, N_KV * tq), F32)],
        compiler_params=_cparams(("arbitrary", "arbitrary")),
    )(qn, gates, selbf.reshape(nb, t_len, 512), winbf.reshape(nb, t_len, 512), kc, vc, block_onehot)


def _nsa_sample_select_kernel(q_ref, kc_ref, vc_ref, oc_ref, sel_ref, sc_scr, st_scr,
                              *, n_sub, n_sel, ts, past, jw):
    b = pl.program_id(0)
    nb = pl.num_programs(0)
    n_cmp = n_sub - 1
    q = q_ref[...]
    imp_m = _imp_matrix(n_sub, n_cmp, jw)
    rows = QPG * ts
    rowpos = past + lax.broadcasted_iota(jnp.int32, (rows, 1), 0) % ts
    n_idx = lax.broadcasted_iota(jnp.int32, (rows, n_sub), 1)
    ok_c = (n_idx * CMP_STRIDE + (CMP_LEN - 1) <= rowpos) & (n_idx < n_cmp)
    jl = lax.broadcasted_iota(jnp.int32, (ts, jw), 1)
    blk = (past + lax.broadcasted_iota(jnp.int32, (ts, 1), 0)) // SEL_BLOCK
    forced = (jl == 0) | ((jl <= blk) & (jl > blk - N_LOCAL))

    @pl.when(b == 0)
    def _():
        sc_scr[...] = jnp.full(sc_scr.shape, INVALID, F32)

    for g in range(N_KV):
        qg = jnp.concatenate(
            [q[:, (g * QPG + h) * NSA_HD:(g * QPG + h + 1) * NSA_HD] for h in range(QPG)],
            axis=0).astype(BF16)
        s_c = _mm(qg, kc_ref[g].astype(BF16), NT)
        p_c = _softmax_rows(s_c, ok_c)
        oc_ref[g * rows:(g + 1) * rows, :] = _mm(p_c.astype(BF16), vc_ref[g].astype(BF16))
        p_g = p_c[0:ts] + p_c[ts:2 * ts] + p_c[2 * ts:3 * ts] + p_c[3 * ts:4 * ts]
        imp = _mm_lhs_exact(p_g, imp_m)
        score = jnp.where((jl <= blk) & (jl < n_sel), jnp.where(forced, FORCED, imp), INVALID)
        sc_scr[pl.ds(pl.multiple_of((b * N_KV + g) * ts, ts), ts), :] = score

    @pl.when(b == nb - 1)
    def _():
        st_scr[...] = sc_scr[...].T
        rank = _topk_mask_t(st_scr, n_sel, LANES)
        sel_t = jnp.where((rank < SEL_TOPK) & (st_scr[...] > 0.5 * INVALID), 1.0, 0.0)
        sel_ref[...] = sel_t.T


def _nsa_sample_select(qn_s, kc, vc, ts, past, n_sel):
    nb = kc.shape[0]
    n_sub = kc.shape[2]
    jw = -(-n_sel // LANES) * LANES
    assert nb * N_KV * ts <= LANES
    cmp_spec = pl.BlockSpec((None, N_KV, n_sub, NSA_HD), lambda b: (b, 0, 0, 0))
    return pl.pallas_call(
        functools.partial(_nsa_sample_select_kernel, n_sub=n_sub, n_sel=n_sel, ts=ts, past=past, jw=jw),
        grid=(nb,),
        in_specs=[pl.BlockSpec((None, ts, 1024), lambda b: (b, 0, 0)), cmp_spec, cmp_spec],
        out_specs=[pl.BlockSpec((None, N_QH * ts, NSA_HD), lambda b: (b, 0, 0)),
                   pl.BlockSpec((LANES, jw), lambda b: (0, 0))],
        out_shape=[jax.ShapeDtypeStruct((nb, N_QH * ts, NSA_HD), F32),
                   jax.ShapeDtypeStruct((LANES, jw), F32)],
        scratch_shapes=[pltpu.VMEM((LANES, jw), F32), pltpu.VMEM((jw, LANES), F32)],
        compiler_params=_cparams(("arbitrary",)),
    )(qn_s.reshape(nb, ts, 1024), kc, vc)


def _nsa_sample_sweep_kernel(past_ref, q_ref, gt_ref, sel_ref, oc_ref, rows_ref, wnew_ref, wst_ref,
                             o_ref, m_scr, l_scr, acc_scr, *, nk, ts, past, n_sel, wbuf):
    step = pl.program_id(1)
    rows = QPG * ts
    heads = 4 * N_KV
    jw = sel_ref.shape[1]
    q = q_ref[...]
    qgs = [jnp.concatenate(
        [q[:, (g * QPG + h) * NSA_HD:(g * QPG + h + 1) * NSA_HD] for h in range(QPG)],
        axis=0).astype(BF16) for g in range(N_KV)]
    sel = sel_ref[...].astype(BF16)

    @pl.when(step == 0)
    def _():
        m_scr[...] = jnp.full(m_scr.shape, NEG, F32)
        l_scr[...] = jnp.zeros_like(l_scr)
        acc_scr[...] = jnp.zeros_like(acc_scr)

    def update(g, s, ok, vv):
        r0 = slice(g * rows, (g + 1) * rows)
        s = jnp.where(ok, s, NEG)
        m_i = m_scr[r0]
        m_new = jnp.maximum(m_i, jnp.max(s, axis=-1, keepdims=True))
        pr = jnp.where(ok, jnp.exp(s - m_new), 0.0)
        alpha = jnp.exp(m_i - m_new)
        l_scr[r0] = alpha * l_scr[r0] + jnp.sum(pr, axis=-1, keepdims=True)
        acc_scr[r0] = alpha * acc_scr[r0] + _mm(pr.astype(BF16), vv)
        m_scr[r0] = m_new

    ej = lax.broadcasted_iota(jnp.int32, (jw, nk), 0)
    kl = lax.broadcasted_iota(jnp.int32, (1, nk), 1)
    expand = jnp.where(ej == (step * nk + kl) // SEL_BLOCK, 1.0, 0.0).astype(BF16)
    mask2 = _mm(sel, expand)
    for g in range(N_KV):
        kk = past_ref[:, g * NSA_HD:(g + 1) * NSA_HD]
        vv = past_ref[:, (N_KV + g) * NSA_HD:(N_KV + g + 1) * NSA_HD]
        mg = mask2[g * ts:(g + 1) * ts]
        ok = jnp.concatenate([mg] * QPG, axis=0) > 0.5
        update(g, _mm(qgs[g], kk, NT), ok, vv)

    @pl.when(step == pl.num_programs(1) - 1)
    def _():
        tpos = lax.broadcasted_iota(jnp.int32, (rows, 1), 0) % ts
        pad = jnp.zeros((PAGE - ts, NSA_HD), F32)
        il = lax.broadcasted_iota(jnp.int32, (1, PAGE), 1)
        last_sel = sel_ref[:, n_sel - 1:n_sel]
        gt = gt_ref[...]
        wlen = wbuf + PAGE
        wl = lax.broadcasted_iota(jnp.int32, (1, wlen), 1)
        dist = (past + tpos) - (past - wbuf + wl)
        ok_w = (dist >= 0) & (dist < WINDOW) & (wl < wbuf + ts)
        outs = []
        for g in range(N_KV):
            kn = jnp.concatenate([rows_ref[pl.ds(2 * N_KV + g, ts, stride=heads), :], pad], axis=0)
            vn = jnp.concatenate([rows_ref[pl.ds(3 * N_KV + g, ts, stride=heads), :], pad], axis=0)
            lsel = jnp.concatenate([last_sel[g * ts:(g + 1) * ts]] * QPG, axis=0) > 0.5
            ok = (il <= tpos) & (il < ts) & lsel
            update(g, _mm(qgs[g], kn.astype(BF16), NT), ok, vn.astype(BF16))
            r0 = slice(g * rows, (g + 1) * rows)
            o_s = acc_scr[r0] / l_scr[r0]
            kw = jnp.concatenate([wst_ref[pl.ds(g, wbuf, stride=2 * N_KV), :],
                                  wnew_ref[pl.ds(g, ts, stride=2 * N_KV), :], pad], axis=0).astype(BF16)
            vw = jnp.concatenate([wst_ref[pl.ds(N_KV + g, wbuf, stride=2 * N_KV), :],
                                  wnew_ref[pl.ds(N_KV + g, ts, stride=2 * N_KV), :], pad],
                                 axis=0).astype(BF16)
            p_w = _softmax_rows(_mm(qgs[g], kw, NT), ok_w)
            o_w = _mm(p_w.astype(BF16), vw)
            o_c = oc_ref[r0, :]
            for h in range(QPG):
                c0 = (g * QPG + h) * 3
                rh = slice(h * ts, (h + 1) * ts)
                outs.append(gt[:, c0:c0 + 1] * o_c[rh] + gt[:, c0 + 1:c0 + 2] * o_s[rh]
                            + gt[:, c0 + 2:c0 + 3] * o_w[rh])
        o_ref[...] = jnp.concatenate(outs, axis=1)


def _nsa_sample_sweep(past_sel, qn_s, gates_s, sel, o_cmp, rows_s, win_s, win_state, ts, n_sel):
    nb, past, _ = past_sel.shape
    nk = min(past, 2048)
    wbuf = win_state.shape[1] // (2 * N_KV)
    jw = sel.shape[1]
    per_b = lambda r, w: pl.BlockSpec((None, r, w), lambda b, s: (b, 0, 0))
    return pl.pallas_call(
        functools.partial(_nsa_sample_sweep_kernel, nk=nk, ts=ts, past=past, n_sel=n_sel, wbuf=wbuf),
        grid=(nb, past // nk),
        in_specs=[pl.BlockSpec((None, nk, 512), lambda b, s: (b, s, 0)),
                  per_b(ts, 1024), per_b(ts, LANES),
                  pl.BlockSpec((N_KV * ts, jw), lambda b, s: (b, 0)),
                  per_b(N_QH * ts, NSA_HD), per_b(ts * 4 * N_KV, NSA_HD), per_b(ts * 2 * N_KV, NSA_HD),
                  per_b(wbuf * 2 * N_KV, NSA_HD)],
        out_specs=per_b(ts, 1024),
        out_shape=jax.ShapeDtypeStruct((nb, ts, 1024), F32),
        scratch_shapes=[pltpu.VMEM((N_QH * ts, 1), F32), pltpu.VMEM((N_QH * ts, 1), F32),
                        pltpu.VMEM((N_QH * ts, NSA_HD), F32)],
        compiler_params=_cparams(("arbitrary", "arbitrary")),
    )(past_sel, qn_s.reshape(nb, ts, 1024), gates_s.reshape(nb, ts, LANES), sel,
      o_cmp, rows_s.reshape(nb, ts * 4 * N_KV, NSA_HD), win_s.reshape(nb, ts * 2 * N_KV, NSA_HD),
      win_state)


def _softplus(z):
    return jnp.maximum(z, 0.0) + jnp.log1p(jnp.exp(-jnp.abs(z)))


def _rwkv_kernel(rkv_ref, aux_ref, sh_rkv_ref, sh_aux_ref, s0_ref, mu_rkv_ref, mu_aux_ref,
                 vec_ref, ww_ref, wa_ref, wg_ref, o_ref, sfin_ref, s_scr, c_rkv, c_aux, *, cs, n_valid):
    ci = pl.program_id(1)
    n_pairs = s_scr.shape[0]
    dr = n_pairs * LANES

    @pl.when(ci == 0)
    def _():
        s_scr[...] = s0_ref[...]
        c_rkv[...] = sh_rkv_ref[...]
        c_aux[...] = sh_aux_ref[...]

    def shift_mix(x, carry, mu):
        first = lax.broadcasted_iota(jnp.int32, x.shape, 0) == 0
        prev = jnp.where(first, carry, pltpu.roll(x, 1, 0))
        return x + (prev - x) * mu

    rkv = rkv_ref[...]
    aux = aux_ref[...]
    xm = shift_mix(rkv, c_rkv[...], mu_rkv_ref[...])
    xa = shift_mix(aux, c_aux[...], mu_aux_ref[...])
    c_rkv[...] = rkv[cs - 1:cs, :]
    c_aux[...] = aux[cs - 1:cs, :]

    w0, a0, k_k, k_a = vec_ref[0:1, :], vec_ref[1:2, :], vec_ref[2:3, :], vec_ref[3:4, :]
    r_k, ln_w, ln_b = vec_ref[4:5, :], vec_ref[5:6, :], vec_ref[6:7, :]
    r = xm[:, :dr]
    k = xm[:, dr:2 * dr]
    v = xm[:, 2 * dr:]
    u = w0 + _mm(jnp.tanh(xa).astype(BF16), ww_ref[...])
    lw = -jnp.exp(-_softplus(-u) - 0.5)
    a = jax.nn.sigmoid(a0 + _mm(xa.astype(BF16), wa_ref[...]))
    gate = _mm(jax.nn.sigmoid(xa).astype(BF16), wg_ref[...])

    lane = lax.broadcasted_iota(jnp.int32, (1, LANES), 1)
    head0 = lane < RWKV_HD
    rr = lax.broadcasted_iota(jnp.int32, (LANES, LANES), 0)
    cc = lax.broadcasted_iota(jnp.int32, (LANES, LANES), 1)
    seg = jnp.where((rr // RWKV_HD) == (cc // RWKV_HD), 1.0, 0.0).astype(BF16)
    eye = jnp.where(rr == cc, 1.0, 0.0)

    def seg_sum(x):
        return jnp.concatenate(
            [_mm_lhs_exact(x[:, p * LANES:(p + 1) * LANES], seg) for p in range(n_pairs)], axis=1)

    kk = k * k_k
    kkn = kk * lax.rsqrt(seg_sum(kk * kk) + 1e-12)
    kh = k * (1.0 + (a - 1.0) * k_a)
    bb = kkn * a
    bonus = seg_sum(r * kh * r_k) * v
    if n_valid < cs:
        live = lax.broadcasted_iota(jnp.int32, (cs, 1), 0) < n_valid
        lw = jnp.where(live, lw, 0.0)
        kh = jnp.where(live, kh, 0.0)
        kkn = jnp.where(live, kkn, 0.0)
        bb = jnp.where(live, bb, 0.0)
        v = jnp.where(live, v, 0.0)

    t_r = lax.broadcasted_iota(jnp.int32, (cs, cs), 0)
    t_c = lax.broadcasted_iota(jnp.int32, (cs, cs), 1)
    tri = jnp.where(t_r >= t_c, 1.0, 0.0).astype(BF16)
    cum = _mm_rhs_exact(tri, lw)
    tot = cum[cs - 1:cs, :]
    e_inc = jnp.exp(cum)
    e_inv = jnp.exp(-cum)
    e_rem = jnp.exp(tot - cum)
    q_t = r * e_inc
    a_t = -kkn * jnp.exp(cum - lw)
    k_t = kh * e_inv
    b_t = bb * e_inv
    k_hat = kh * e_rem
    b_hat = bb * e_rem
    w_tot = jnp.exp(tot)

    s2 = 2 * cs
    sr = lax.broadcasted_iota(jnp.int32, (s2, s2), 0) % cs
    sc = lax.broadcasted_iota(jnp.int32, (s2, s2), 1) % cs
    strict = sr > sc
    incl = sr >= sc
    eye2 = jnp.where(lax.broadcasted_iota(jnp.int32, (s2, s2), 0)
                     == lax.broadcasted_iota(jnp.int32, (s2, s2), 1), 1.0, 0.0)

    def stack(x):
        return jnp.concatenate([jnp.where(head0, x, 0.0), jnp.where(head0, 0.0, x)], axis=0)

    n_dbl = int(math.log2(cs)) - 1
    zeros_s = jnp.zeros((s2, LANES), F32)
    prs = range(n_pairs)
    lanes_of = lambda x: [stack(x[:, p * LANES:(p + 1) * LANES]) for p in prs]
    a_s, q_s, k_s, b_s = lanes_of(a_t), lanes_of(q_t), lanes_of(k_t), lanes_of(b_t)
    v_s, kh_s, bh_s = lanes_of(v), lanes_of(k_hat), lanes_of(b_hat)
    gram = [_mmp(jnp.concatenate([a_s[p], q_s[p]], axis=0), jnp.concatenate([b_s[p], k_s[p]], axis=0),
                 NT, RW_PASSES["gram"]) for p in prs]
    a_ab = [jnp.where(strict, gram[p][:s2, :s2], 0.0) for p in prs]
    a_ak = [jnp.where(strict, gram[p][:s2, s2:], 0.0) for p in prs]
    b_rbk = [jnp.concatenate([jnp.where(incl, gram[p][s2:, :s2], 0.0),
                              jnp.where(incl, gram[p][s2:, s2:], 0.0)], axis=1) for p in prs]
    akv = [_mmp(a_ak[p], v_s[p], passes=RW_PASSES["mix"]) for p in prs]
    tinv = [eye2 + a_ab[p] for p in prs]
    apow = a_ab
    for _ in range(n_dbl):
        apow = [_mmp(apow[p], apow[p], passes=RW_PASSES["inv"]) for p in prs]
        tinv = [tinv[p] + _mmp(apow[p], tinv[p], passes=RW_PASSES["inv"]) for p in prs]
    au = [_mmp(tinv[p], jnp.concatenate([a_s[p], akv[p]], axis=1), passes=RW_PASSES["mix"])
          for p in prs]
    ry = [_mmp(b_rbk[p], jnp.concatenate([au[p], jnp.concatenate([zeros_s, v_s[p]], axis=1)], axis=0),
               passes=RW_PASSES["mix"]) for p in prs]
    m_p = [eye * w_tot[:, p * LANES:(p + 1) * LANES]
           + _mmp(au[p][:, :LANES], bh_s[p], TN, RW_PASSES["trans"]) for p in prs]
    n_p = [_mmp(jnp.concatenate([au[p][:, LANES:], v_s[p]], axis=0),
                jnp.concatenate([bh_s[p], kh_s[p]], axis=0), TN, RW_PASSES["trans"]) for p in prs]
    s_old = [s_scr[p] for p in prs]
    y_s = [_mmp(q_s[p] + ry[p][:, :LANES], s_old[p], NT, RW_PASSES["out"]) + ry[p][:, LANES:]
           for p in prs]
    for p in prs:
        s_scr[p] = _mmp(s_old[p], m_p[p], passes=RW_PASSES["state"]) + n_p[p]
    y = jnp.concatenate([y_s[p][:cs] + y_s[p][cs:] for p in prs], axis=1)

    mean = seg_sum(y) * (1.0 / RWKV_HD)
    dy = y - mean
    var = seg_sum(dy * dy) * (1.0 / RWKV_HD)
    yn = dy * lax.rsqrt(var + GN_EPS) * ln_w + ln_b
    o_ref[...] = ((yn + bonus) * gate).astype(o_ref.dtype)

    @pl.when(ci == pl.num_programs(1) - 1)
    def _():
        sfin_ref[...] = s_scr[...]


def _rwkv(p, shift_rkv, shift_aux, s0_pairs, mu_rkv, mu_aux, vecs, ww, wa, wg, n_seq, t_len, cs, n_valid, out_dtype):
    n_chunks = t_len // cs
    n_pairs = s0_pairs.shape[1]
    dr = n_pairs * LANES
    full = lambda shape: pl.BlockSpec(shape, lambda b, c: (0,) * len(shape))
    return pl.pallas_call(
        functools.partial(_rwkv_kernel, cs=cs, n_valid=n_valid),
        grid=(n_seq, n_chunks),
        in_specs=[pl.BlockSpec((cs, 3 * dr), lambda b, c: (b * n_chunks + c, P_RKV // (3 * dr))),
                  pl.BlockSpec((cs, P_AUX_W), lambda b, c: (b * n_chunks + c, P_AUX // P_AUX_W)),
                  pl.BlockSpec((None, 1, 3 * dr), lambda b, c: (b, 0, 0)),
                  pl.BlockSpec((None, 1, P_AUX_W), lambda b, c: (b, 0, 0)),
                  pl.BlockSpec((None, n_pairs, LANES, LANES), lambda b, c: (b, 0, 0, 0)),
                  full(mu_rkv.shape), full(mu_aux.shape), full(vecs.shape),
                  full(ww.shape), full(wa.shape), full(wg.shape)],
        out_specs=[pl.BlockSpec((cs, dr), lambda b, c: (b * n_chunks + c, 0)),
                   pl.BlockSpec((None, n_pairs, LANES, LANES), lambda b, c: (b, 0, 0, 0))],
        out_shape=[jax.ShapeDtypeStruct((n_seq * t_len, dr), out_dtype),
                   jax.ShapeDtypeStruct((n_seq, n_pairs, LANES, LANES), F32)],
        scratch_shapes=[pltpu.VMEM((n_pairs, LANES, LANES), F32),
                        pltpu.VMEM((1, 3 * dr), F32), pltpu.VMEM((1, P_AUX_W), F32)],
        compiler_params=_cparams(("arbitrary", "arbitrary")),
    )(p, p, shift_rkv, shift_aux, s0_pairs, mu_rkv, mu_aux, vecs, ww, wa, wg)


def _out_kernel(x_ref, m_ref, on_ref, or_ref, wn_ref, wr_ref, o_ref):
    mix = _mm(on_ref[...], wn_ref[...]) + _mm(or_ref[...], wr_ref[...])
    o_ref[...] = x_ref[...] + m_ref[5] * mix


def _out_proj(x, mod, o_nsa, o_rw, w_out, layer, tm, tiles_per_seq):
    m_rows, d = x.shape
    dn = o_nsa.shape[1]
    mr = mod.shape[2]
    return pl.pallas_call(
        _out_kernel,
        grid=(m_rows // tm,),
        in_specs=[pl.BlockSpec((tm, d), lambda i: (i, 0)),
                  pl.BlockSpec((None, N_MOD, mr, d), lambda i: (i // tiles_per_seq, 0, 0, 0)),
                  pl.BlockSpec((tm, dn), lambda i: (i, 0)),
                  pl.BlockSpec((tm, dn), lambda i: (i, 0)),
                  pl.BlockSpec((None, dn, d), lambda i: (layer, 0, 0)),
                  pl.BlockSpec((None, dn, d), lambda i: (layer, 1, 0))],
        out_specs=pl.BlockSpec((tm, d), lambda i: (i, 0)),
        out_shape=jax.ShapeDtypeStruct((m_rows, d), F32),
        compiler_params=_cparams(("arbitrary",)),
    )(x, mod, o_nsa, o_rw, w_out, w_out)


def _pad_cols(x, n):
    return jnp.pad(x, [(0, 0)] * (x.ndim - 1) + [(0, n)])


def _reorder_cols(w, d_rwkv, n_gate):
    nsa_main = P_AUX
    rw0 = nsa_main + n_gate
    lora0 = rw0 + 3 * d_rwkv
    n_lora = w.shape[-1] - lora0
    aux = jnp.concatenate([_pad_cols(w[..., nsa_main:rw0], AUX_WD - n_gate),
                           _pad_cols(w[..., lora0:], P_AUX_W - AUX_WD - n_lora)], axis=-1)
    return jnp.concatenate([w[..., :nsa_main], aux, w[..., rw0:lora0]], axis=-1)


def _pairs_from_heads(s):
    n, h = s.shape[:2]
    s = s.reshape(n, h // 2, 2, RWKV_HD, RWKV_HD)
    z = jnp.zeros_like(s[:, :, 0])
    top = jnp.concatenate([s[:, :, 0], z], axis=-1)
    bot = jnp.concatenate([z, s[:, :, 1]], axis=-1)
    return jnp.concatenate([top, bot], axis=-2)


def _heads_from_pairs(s):
    n, hp = s.shape[:2]
    a = s[:, :, :RWKV_HD, :RWKV_HD]
    b = s[:, :, RWKV_HD:, RWKV_HD:]
    return jnp.stack([a, b], axis=2).reshape(n, 2 * hp, RWKV_HD, RWKV_HD)


def kernel(x_prompt, x_sample, cache_nsa_kv, state_win_kv, state_wkv, state_shift, page_table,
           c_prompt, c_sample, w_ada, b_ada, norm_g, ffn_wi, ffn_wo, w_in, w_out,
           q_norm_g, k_norm_g, cmp_pe, cmp_w1, cmp_b1, cmp_w2,
           rwkv_mu, rwkv_w0, rwkv_w_w2, rwkv_a0, rwkv_w_a2, rwkv_w_g2,
           rwkv_k_k, rwkv_k_a, rwkv_r_k, rwkv_ln_w, rwkv_ln_b):
    bp, tp, d = x_prompt.shape
    bs, ts, _ = x_sample.shape
    depth = w_ada.shape[0]
    n_pool = cache_nsa_kv.shape[1]
    n_pages = page_table.shape[1]
    past = n_pages * PAGE
    wbuf = state_win_kv.shape[2]
    d_rwkv = rwkv_w0.shape[1]
    n_heads = d_rwkv // RWKV_HD
    d_nsa = N_QH * NSA_HD
    n_gate = 3 * N_QH
    n_dlora = rwkv_w_w2.shape[1]
    n_alora = rwkv_w_a2.shape[1]
    n_glora = rwkv_w_g2.shape[1]
    mp_rows, ms_rows = bp * tp, bs * ts
    tm = 512
    tiles_per_seq = tp // tm
    cs = 64
    ts_pad = -(-ts // cs) * cs
    win_keep = min(WINDOW, tp)

    c_all = jnp.concatenate([c_prompt, c_sample], axis=0)
    c_rows = -(-c_all.shape[0] // 8) * 8
    c_all = jnp.pad(c_all, ((0, c_rows - c_all.shape[0]), (0, 0)))
    mod = _modulation(c_all, w_ada, b_ada).reshape(depth, c_rows, N_MOD, d)
    mod_p = mod[:, :bp].reshape(depth, bp, N_MOD, 1, d)
    mod_s = jnp.repeat(mod[:, bp:bp + bs].transpose(0, 2, 1, 3), ts, axis=2).reshape(depth, 1, N_MOD, ms_rows, d)

    xp = x_prompt.reshape(mp_rows, d)
    xs = x_sample.reshape(ms_rows, d)
    pool = cache_nsa_kv.reshape(depth * n_pool * PAGE * 4 * N_KV, NSA_HD)
    prompt_pages = jnp.arange(bp * (tp // PAGE), dtype=jnp.int32).reshape(bp, tp // PAGE)
    n_sel_s = -(-(past + ts) // SEL_BLOCK)

    kv_p, kv_s, win_p, win_s, wkv_p, wkv_s, sh_p, sh_s = [], [], [], [], [], [], [], []
    tm_ffn = 1024
    ffn_tiles_per_seq = tp // tm_ffn
    w_in_all = _reorder_cols(w_in, d_rwkv, n_gate).astype(BF16)
    w_out_all = w_out.astype(BF16)
    for l in range(depth):
        g_rows = [norm_g[l, i].reshape(1, d) for i in range(3)]
        q_g = q_norm_g[l].reshape(1, NSA_HD)
        k_g = k_norm_g[l]
        w1 = cmp_w1[l].reshape(2, 2, CMP_STRIDE * NSA_HD, CMP_HID)
        w1cat = jnp.concatenate([w1[:, 0], w1[:, 1]], axis=-1).astype(BF16)
        pe2 = cmp_pe[l].reshape(2, 2, 1, CMP_STRIDE * NSA_HD)
        b1 = cmp_b1[l].reshape(2, 1, CMP_HID)
        w2 = cmp_w2[l].astype(BF16)
        def shift_parts(sh):
            aux = jnp.pad(sh[:, 3 * d_rwkv:], ((0, 0), (AUX_WD, P_AUX_W - AUX_WD - (sh.shape[1] - 3 * d_rwkv))))
            return sh[:, None, :3 * d_rwkv], aux[:, None, :]

        mu_rkv, mu_aux = (m[:, 0] for m in shift_parts(rwkv_mu[l].reshape(1, -1)))
        vecs = jnp.stack([rwkv_w0[l], rwkv_a0[l], rwkv_k_k[l], rwkv_k_a[l], rwkv_r_k[l].reshape(-1),
                          rwkv_ln_w[l], rwkv_ln_b[l], jnp.zeros_like(rwkv_w0[l])])
        lora_rows = lambda w, off: jnp.pad(w, ((off, P_AUX_W - off - w.shape[0]), (0, 0))).astype(BF16)
        ww = lora_rows(rwkv_w_w2[l], AUX_WD)
        wa = lora_rows(rwkv_w_a2[l], AUX_WD + n_dlora)
        wg = lora_rows(rwkv_w_g2[l], AUX_WD + n_dlora + n_alora)

        def shift_out(p_last):
            return jnp.concatenate([p_last[:, P_RKV:P_RKV + 3 * d_rwkv],
                                    p_last[:, P_AUX + AUX_WD:P_AUX + AUX_WD + n_dlora + n_alora + n_glora]],
                                   axis=1)

        xp = _ffn(xp, mod_p[l], g_rows[0], ffn_wi, ffn_wo, l, 0, 0, tm_ffn, ffn_tiles_per_seq)
        xs = _ffn(xs, mod_s[l], g_rows[0], ffn_wi, ffn_wo, l, 0, 0, ms_rows, 1)

        pp = _proj(xp, mod_p[l], g_rows[1], w_in_all, l, tm, tiles_per_seq)
        qn, rows, win, selbf, winbf, gates = _nsa_prep(pp, q_g, k_g, tm, BF16)
        kc, vc = _compress(rows, prompt_pages, w1cat, pe2, b1, w2, k_g, False)
        o_nsa = _nsa_prompt(qn, gates, selbf, winbf, kc, vc, bp, tp)
        zero_rkv, zero_aux = shift_parts(jnp.zeros((bp, state_shift.shape[2]), F32))
        o_rw, s_fin = _rwkv(pp, zero_rkv, zero_aux, jnp.zeros((bp, n_heads // 2, LANES, LANES), F32),
                            mu_rkv, mu_aux, vecs, ww, wa, wg, bp, tp, cs, cs, BF16)
        xp = _out_proj(xp, mod_p[l], o_nsa, o_rw, w_out_all, l, tm, tiles_per_seq)
        kv_p.append(rows.reshape(bp, tp // PAGE, PAGE, 4, N_KV, NSA_HD))
        win_p.append(win.reshape(bp, tp, 2, N_KV, NSA_HD)[:, tp - win_keep:])
        wkv_p.append(_heads_from_pairs(s_fin))
        sh_p.append(shift_out(pp.reshape(bp, tp, P_COLS)[:, -1]))

        ps = _proj(xs, mod_s[l], g_rows[1], w_in_all, l, ms_rows, 1)
        qn_s, rows_s, win_new, _, _, gates_s = _nsa_prep(ps, q_g, k_g, ms_rows, F32)
        page_idx = page_table + l * n_pool
        kc_s, vc_s, past_sel = _compress(pool, page_idx, w1cat, pe2, b1, w2, k_g, True)
        o_cmp, sel = _nsa_sample_select(qn_s, kc_s, vc_s, ts, past, n_sel_s)
        o_nsa_s = _nsa_sample_sweep(past_sel, qn_s, gates_s, sel, o_cmp, rows_s, win_new,
                                    state_win_kv[l].reshape(bs, wbuf * 2 * N_KV, NSA_HD), ts, n_sel_s)
        sh_rkv, sh_aux = shift_parts(state_shift[l])
        ps_pad = jnp.pad(ps.reshape(bs, ts, P_COLS), ((0, 0), (0, ts_pad - ts), (0, 0)))
        o_rw_s, s_fin_s = _rwkv(ps_pad.reshape(bs * ts_pad, P_COLS), sh_rkv, sh_aux,
                                _pairs_from_heads(state_wkv[l]), mu_rkv, mu_aux, vecs, ww, wa, wg,
                                bs, ts_pad, ts_pad, ts, F32)
        o_rw_s = o_rw_s.reshape(bs, ts_pad, d_rwkv)[:, :ts].reshape(ms_rows, d_rwkv)
        xs = _out_proj(xs, mod_s[l], o_nsa_s.reshape(ms_rows, d_nsa).astype(BF16), o_rw_s.astype(BF16),
                       w_out_all, l, ms_rows, 1)
        kv_s.append(rows_s.reshape(bs, ts, 4, N_KV, NSA_HD))
        win_s.append(jnp.concatenate([state_win_kv[l][:, ts:],
                                      win_new.reshape(bs, ts, 2, N_KV, NSA_HD)], axis=1))
        wkv_s.append(_heads_from_pairs(s_fin_s))
        sh_s.append(shift_out(ps.reshape(bs, ts, P_COLS)[:, -1]))

        xp = _ffn(xp, mod_p[l], g_rows[2], ffn_wi, ffn_wo, l, 1, 2, tm_ffn, ffn_tiles_per_seq)
        xs = _ffn(xs, mod_s[l], g_rows[2], ffn_wi, ffn_wo, l, 1, 2, ms_rows, 1)

    return (xp.reshape(bp, tp, d), xs.reshape(bs, ts, d),
            jnp.stack(kv_p), jnp.stack(kv_s), jnp.stack(win_p), jnp.stack(win_s),
            jnp.stack(wkv_p), jnp.stack(wkv_s), jnp.stack(sh_p), jnp.stack(sh_s))
```

```python
import functools
import math

import jax
import jax.numpy as jnp
from jax import lax
from jax.experimental import pallas as pl
from jax.experimental.pallas import tpu as pltpu

F32 = jnp.float32
BF16 = jnp.bfloat16

NSA_HD = 128
N_KV = 2
QPG = 4
N_QH = N_KV * QPG
CMP_LEN = 32
CMP_STRIDE = 16
CMP_HID = 2 * NSA_HD
SEL_BLOCK = 64
SEL_TOPK = 16
N_LOCAL = 2
WINDOW = 512
PAGE = 128
RWKV_HD = 64
N_MOD = 9
NORM_EPS = 1e-6
GN_EPS = 64e-5
NEG = -1e30
FORCED = 1e6
INVALID = -1e6

LANES = 128
VMEM_LIMIT = 56 * 1024 * 1024
PAGES_PER_STEP = 8

P_Q = 0
P_KV = 1024
P_AUX = 2560
P_AUX_W = 512
P_RKV = 3072
P_COLS = 6144
AUX_WD = 128
AUX_AD = 192
AUX_GD = 256


def _cparams(sem):
    return pltpu.CompilerParams(dimension_semantics=sem, vmem_limit_bytes=VMEM_LIMIT)


def _mm(a, b, dims=((1,), (0,))):
    return lax.dot_general(a, b, (dims, ((), ())), preferred_element_type=F32)


NT = ((1,), (1,))
TN = ((0,), (0,))


def _split2(x):
    hi = x.astype(BF16)
    lo = (x - hi.astype(F32)).astype(BF16)
    return hi, lo


def _split3(x):
    hi = x.astype(BF16)
    r1 = x - hi.astype(F32)
    mid = r1.astype(BF16)
    lo = (r1 - mid.astype(F32)).astype(BF16)
    return hi, mid, lo


def _mm3(a, b, dims=((1,), (0,))):
    ah, al = _split2(a)
    bh, bl = _split2(b)
    return _mm(ah, bh, dims) + (_mm(ah, bl, dims) + _mm(al, bh, dims))


def _mmp(a, b, dims=((1,), (0,)), passes=1):
    if passes == 3:
        return _mm3(a, b, dims)
    return _mm(a.astype(BF16), b.astype(BF16), dims)


RW_PASSES = {"gram": 1, "inv": 1, "mix": 1, "trans": 1, "out": 1, "state": 1}


def _mm_lhs_exact(a, b_bf16, dims=((1,), (0,))):
    a1, a2, a3 = _split3(a)
    return _mm(a1, b_bf16, dims) + (_mm(a2, b_bf16, dims) + _mm(a3, b_bf16, dims))


def _mm_rhs_exact(a_bf16, b, dims=((1,), (0,))):
    b1, b2, b3 = _split3(b)
    return _mm(a_bf16, b1, dims) + (_mm(a_bf16, b2, dims) + _mm(a_bf16, b3, dims))


def _silu(x):
    return x * jax.nn.sigmoid(x)


def _rms(x, g):
    return x * lax.rsqrt(jnp.mean(x * x, axis=-1, keepdims=True) + NORM_EPS) * g


def _ada_norm(x, m_ref, slot, g):
    return _rms(x, g) * (1.0 + m_ref[3 * slot + 1]) + m_ref[3 * slot]


def _mod_kernel(c_ref, w_ref, b_ref, o_ref):
    s = _silu(c_ref[...]).astype(BF16)
    o_ref[0] = _mm(s, w_ref[0].astype(BF16)) + b_ref[0]


def _modulation(c_all, w_ada, b_ada):
    depth, d, n = w_ada.shape
    rows = c_all.shape[0]
    tn = 1024
    return pl.pallas_call(
        _mod_kernel,
        grid=(depth, n // tn),
        in_specs=[pl.BlockSpec((rows, d), lambda l, j: (0, 0)),
                  pl.BlockSpec((1, d, tn), lambda l, j: (l, 0, j)),
                  pl.BlockSpec((1, 1, tn), lambda l, j: (l, 0, j))],
        out_specs=pl.BlockSpec((1, rows, tn), lambda l, j: (l, 0, j)),
        out_shape=jax.ShapeDtypeStruct((depth, rows, n), F32),
        compiler_params=_cparams(("arbitrary", "arbitrary")),
    )(c_all, w_ada, b_ada.reshape(depth, 1, n))


def _ffn_kernel(x_ref, m_ref, g_ref, wg_ref, wu_ref, wo_ref, o_ref, h_scr, acc_scr, *, slot):
    f = pl.program_id(1)

    @pl.when(f == 0)
    def _():
        h_scr[...] = _ada_norm(x_ref[...], m_ref, slot, g_ref[...]).astype(BF16)
        acc_scr[...] = jnp.zeros_like(acc_scr)

    h = h_scr[...]
    gate = _mm(h, wg_ref[...].astype(BF16))
    up = _mm(h, wu_ref[...].astype(BF16))
    act = (_silu(gate) * up).astype(BF16)
    acc_scr[...] += _mm(act, wo_ref[...].astype(BF16))

    @pl.when(f == pl.num_programs(1) - 1)
    def _():
        o_ref[...] = x_ref[...] + 0.5 * m_ref[3 * slot + 2] * acc_scr[...]


def _ffn(x, mod, g, wi, wo, layer, half, slot, tm, tiles_per_seq):
    m_rows, d = x.shape
    d_ff = wo.shape[2]
    tf = 256
    nf = d_ff // tf
    mr = mod.shape[2]
    single = pl.Buffered(1)
    return pl.pallas_call(
        functools.partial(_ffn_kernel, slot=slot),
        grid=(m_rows // tm, nf),
        in_specs=[pl.BlockSpec((tm, d), lambda i, f: (i, 0), pipeline_mode=single),
                  pl.BlockSpec((None, N_MOD, mr, d), lambda i, f: (i // tiles_per_seq, 0, 0, 0)),
                  pl.BlockSpec((1, d), lambda i, f: (0, 0)),
                  pl.BlockSpec((None, None, d, tf), lambda i, f: (layer, half, 0, f)),
                  pl.BlockSpec((None, None, d, tf), lambda i, f: (layer, half, 0, nf + f)),
                  pl.BlockSpec((None, None, tf, d), lambda i, f: (layer, half, f, 0))],
        out_specs=pl.BlockSpec((tm, d), lambda i, f: (i, 0), pipeline_mode=single),
        out_shape=jax.ShapeDtypeStruct((m_rows, d), F32),
        scratch_shapes=[pltpu.VMEM((tm, d), BF16), pltpu.VMEM((tm, d), F32)],
        compiler_params=_cparams(("arbitrary", "arbitrary")),
    )(x, mod, g, wi, wi, wo)


def _proj_kernel(x_ref, m_ref, g_ref, w_ref, o_ref, h_scr):
    @pl.when(pl.program_id(1) == 0)
    def _():
        h_scr[...] = _ada_norm(x_ref[...], m_ref, 1, g_ref[...]).astype(BF16)

    o_ref[...] = _mm(h_scr[...], w_ref[...])


def _proj(x, mod, g, w, layer, tm, tiles_per_seq):
    m_rows, d = x.shape
    n = w.shape[2]
    tn = 1536
    mr = mod.shape[2]
    return pl.pallas_call(
        _proj_kernel,
        grid=(m_rows // tm, n // tn),
        in_specs=[pl.BlockSpec((tm, d), lambda i, j: (i, 0)),
                  pl.BlockSpec((None, N_MOD, mr, d), lambda i, j: (i // tiles_per_seq, 0, 0, 0)),
                  pl.BlockSpec((1, d), lambda i, j: (0, 0)),
                  pl.BlockSpec((None, d, tn), lambda i, j: (layer, 0, j))],
        out_specs=pl.BlockSpec((tm, tn), lambda i, j: (i, j)),
        out_shape=jax.ShapeDtypeStruct((m_rows, n), F32),
        scratch_shapes=[pltpu.VMEM((tm, d), BF16)],
        compiler_params=_cparams(("arbitrary", "arbitrary")),
    )(x, mod, g, w)


def _rms_heads(x, g):
    outs = []
    for h in range(x.shape[1] // NSA_HD):
        outs.append(_rms(x[:, h * NSA_HD:(h + 1) * NSA_HD], g))
    return jnp.concatenate(outs, axis=1)


def _nsa_prep_kernel(q_ref, kva_ref, kvb_ref, kvc_ref, gt_ref, qg_ref, kg_ref,
                     qn_ref, rows_ref, win_ref, selbf_ref, winbf_ref, gates_ref):
    qn_ref[...] = (_rms_heads(q_ref[...], qg_ref[...]) * (NSA_HD ** -0.5)).astype(qn_ref.dtype)
    kvb = kvb_ref[...]
    ksel = _rms_heads(kvb[:, :2 * NSA_HD], kg_ref[1:2, :])
    selrows = jnp.concatenate([ksel, kvb[:, 2 * NSA_HD:]], axis=1)
    rows = jnp.concatenate([kva_ref[...], selrows], axis=1)
    tm = rows.shape[0]
    for c in range(4 * N_KV):
        rows_ref[pl.ds(c, tm, stride=4 * N_KV), :] = rows[:, c * NSA_HD:(c + 1) * NSA_HD]
    selbf_ref[...] = selrows.astype(selbf_ref.dtype)
    kvc = kvc_ref[...]
    kwin = _rms_heads(kvc[:, :2 * NSA_HD], kg_ref[2:3, :])
    winrows = jnp.concatenate([kwin, kvc[:, 2 * NSA_HD:]], axis=1)
    for c in range(2 * N_KV):
        win_ref[pl.ds(c, tm, stride=2 * N_KV), :] = winrows[:, c * NSA_HD:(c + 1) * NSA_HD]
    winbf_ref[...] = winrows.astype(winbf_ref.dtype)
    gates_ref[...] = jax.nn.sigmoid(gt_ref[...])


def _nsa_prep(p, q_g, k_g, tm, act_dtype):
    m_rows = p.shape[0]
    row = lambda w, j: pl.BlockSpec((tm, w), lambda i: (i, j))
    return pl.pallas_call(
        _nsa_prep_kernel,
        grid=(m_rows // tm,),
        in_specs=[row(1024, 0), row(512, 2), row(512, 3), row(512, 4), row(LANES, P_AUX // LANES),
                  pl.BlockSpec((1, NSA_HD), lambda i: (0, 0)),
                  pl.BlockSpec((3, NSA_HD), lambda i: (0, 0))],
        out_specs=[row(1024, 0), pl.BlockSpec((tm * 8, NSA_HD), lambda i: (i, 0)),
                   pl.BlockSpec((tm * 4, NSA_HD), lambda i: (i, 0)),
                   row(512, 0), row(512, 0), row(LANES, 0)],
        out_shape=[jax.ShapeDtypeStruct((m_rows, 1024), act_dtype),
                   jax.ShapeDtypeStruct((m_rows * 8, NSA_HD), F32),
                   jax.ShapeDtypeStruct((m_rows * 4, NSA_HD), F32),
                   jax.ShapeDtypeStruct((m_rows, 512), act_dtype),
                   jax.ShapeDtypeStruct((m_rows, 512), act_dtype),
                   jax.ShapeDtypeStruct((m_rows, LANES), F32)],
        compiler_params=_cparams(("arbitrary",)),
    )(p, p, p, p, p, q_g, k_g)


def _cmp_kernel(pt_ref, *refs, n_sub, pp, emit_sel):
    pages = refs[:pp]
    if emit_sel:
        w1_ref, pe_ref, b1_ref, w2_ref, kg_ref, kc_ref, vc_ref, sel_ref, x_scr = refs[pp:]
    else:
        w1_ref, pe_ref, b1_ref, w2_ref, kg_ref, kc_ref, vc_ref, x_scr = refs[pp:]
    p = pl.program_id(1)
    spp = PAGE // CMP_STRIDE
    base = pl.multiple_of(p * (pp * spp), pp * spp)
    heads = 4 * N_KV
    by_head = [pltpu.einshape("(tc)d->ctd", pg[...], c=heads) for pg in pages]
    if emit_sel:
        for c in range(2 * N_KV):
            sel_ref[:, c * NSA_HD:(c + 1) * NSA_HD] = jnp.concatenate(
                [h[2 * N_KV + c] for h in by_head], axis=0).astype(BF16)
    for c in range(2 * N_KV):
        x_scr[c, pl.ds(base, pp * spp), :] = jnp.concatenate(
            [pltpu.einshape("(sl)d->s(ld)", h[c], l=CMP_STRIDE) for h in by_head], axis=0).astype(BF16)

    @pl.when(p == pl.num_programs(1) - 1)
    def _():
        row = lax.broadcasted_iota(jnp.int32, (n_sub, NSA_HD), 0)
        for kv in range(2):
            w1 = w1_ref[kv]
            const = b1_ref[kv]
            for j in range(2):
                pe = jnp.broadcast_to(pe_ref[kv, j], (8, CMP_STRIDE * NSA_HD)).astype(BF16)
                const = const + _mm(pe, w1[:, j * CMP_HID:(j + 1) * CMP_HID])[0:1]
            for g in range(N_KV):
                ab = _mm(x_scr[kv * 2 + g], w1)
                nxt = pltpu.roll(ab[:, CMP_HID:], n_sub - 1, 0)
                acc = ab[:, :CMP_HID] + nxt + const
                o = _mm(jax.nn.gelu(acc).astype(BF16), w2_ref[kv])
                if kv == 0:
                    o = _rms(o, kg_ref[0:1, :])
                o = jnp.where(row < n_sub - 1, o, 0.0)
                if kv == 0:
                    kc_ref[g] = o
                else:
                    vc_ref[g] = o


def _compress(pool, page_idx, w1cat, pe2, b1, w2, k_g, emit_sel):
    nb, n_pages = page_idx.shape
    n_sub = n_pages * (PAGE // CMP_STRIDE)
    pp = PAGES_PER_STEP
    page = lambda k: pl.BlockSpec((PAGE * 4 * N_KV, NSA_HD), lambda b, p, pt: (pt[b, pp * p + k], 0))
    page_specs = [page(k) for k in range(pp)]
    full = lambda shape: pl.BlockSpec(shape, lambda b, p, pt: (0,) * len(shape))
    out = pl.BlockSpec((None, N_KV, n_sub, NSA_HD), lambda b, p, pt: (b, 0, 0, 0))
    out_specs = [out, out]
    out_shape = [jax.ShapeDtypeStruct((nb, N_KV, n_sub, NSA_HD), F32)] * 2
    if emit_sel:
        out_specs.append(pl.BlockSpec((None, pp * PAGE, 512), lambda b, p, pt: (b, p, 0)))
        out_shape.append(jax.ShapeDtypeStruct((nb, n_pages * PAGE, 512), BF16))
    return pl.pallas_call(
        functools.partial(_cmp_kernel, n_sub=n_sub, pp=pp, emit_sel=emit_sel),
        grid_spec=pltpu.PrefetchScalarGridSpec(
            num_scalar_prefetch=1,
            grid=(nb, n_pages // pp),
            in_specs=page_specs + [full(w1cat.shape), full(pe2.shape), full(b1.shape),
                      full(w2.shape), full(k_g.shape)],
            out_specs=out_specs,
            scratch_shapes=[pltpu.VMEM((4, n_sub, CMP_STRIDE * NSA_HD), BF16)]),
        out_shape=out_shape,
        compiler_params=_cparams(("arbitrary", "arbitrary")),
    )(page_idx, *([pool] * pp), w1cat, pe2, b1, w2, k_g)


def _imp_matrix(n_sub, n_cmp, width):
    n = lax.broadcasted_iota(jnp.int32, (n_sub, width), 0)
    j = lax.broadcasted_iota(jnp.int32, (n_sub, width), 1)
    spb = SEL_BLOCK // CMP_STRIDE
    m = jnp.where(n // spb == j, 1.0, 0.0) + jnp.where((n + 1) // spb == j, 1.0, 0.0)
    return jnp.where(n < n_cmp, m, 0.0).astype(BF16)


def _topk_mask_t(st_scr, n_iter, width):
    jj = st_scr.shape[0]
    st = st_scr[...]
    jrow = lax.broadcasted_iota(jnp.int32, (jj, width), 0)

    def body(jp, rank):
        r = st_scr[pl.ds(jp, 1), :]
        beats = (r > st) | ((r == st) & (jp < jrow))
        return rank + jnp.where(beats, 1.0, 0.0)

    return lax.fori_loop(0, n_iter, body, jnp.zeros((jj, width), F32))


def _softmax_rows(s, ok):
    sm = jnp.where(ok, s, NEG)
    mx = jnp.max(sm, axis=-1, keepdims=True)
    e = jnp.where(ok, jnp.exp(sm - mx), 0.0)
    den = jnp.sum(e, axis=-1, keepdims=True)
    return e / jnp.where(den > 0.0, den, 1.0)


def _nsa_prompt_kernel(q_ref, gt_ref, sel_ref, win_ref, kc_ref, vc_ref, oh_ref, o_ref, st_scr,
                       *, t_len, n_sub, ck):
    tq = SEL_BLOCK
    qi = pl.program_id(1)
    s0 = qi * tq
    n_cmp = n_sub - 1
    rows = QPG * tq
    q = q_ref[...]
    rowpos = s0 + lax.broadcasted_iota(jnp.int32, (rows, 1), 0) % tq

    def q_group(g):
        return jnp.concatenate(
            [q[:, (g * QPG + h) * NSA_HD:(g * QPG + h + 1) * NSA_HD] for h in range(QPG)], axis=0)

    imp_m = _imp_matrix(n_sub, n_cmp, LANES)
    n_idx = lax.broadcasted_iota(jnp.int32, (rows, n_sub), 1)
    ok_c = (n_idx * CMP_STRIDE + (CMP_LEN - 1) <= rowpos) & (n_idx < n_cmp)
    o_cmp, scores = [], []
    jl = lax.broadcasted_iota(jnp.int32, (tq, LANES), 1)
    forced = (jl == 0) | ((jl <= qi) & (jl > qi - N_LOCAL))
    for g in range(N_KV):
        qg = q_group(g)
        s_c = _mm(qg, kc_ref[g].astype(BF16), NT)
        p_c = _softmax_rows(s_c, ok_c)
        o_cmp.append(_mm(p_c.astype(BF16), vc_ref[g].astype(BF16)))
        p_g = p_c[0:tq] + p_c[tq:2 * tq] + p_c[2 * tq:3 * tq] + p_c[3 * tq:4 * tq]
        imp = _mm_lhs_exact(p_g, imp_m)
        scores.append(jnp.where(jl <= qi, jnp.where(forced, FORCED, imp), INVALID))
    jj = st_scr.shape[0]
    st_scr[...] = jnp.concatenate(scores, axis=0).T[:jj]
    rank = _topk_mask_t(st_scr, qi + 1, N_KV * tq)
    jrow = lax.broadcasted_iota(jnp.int32, (jj, N_KV * tq), 0)
    sel_t = jnp.where((rank < SEL_TOPK) & (jrow <= qi), 1.0, 0.0)
    if jj < LANES:
        sel_t = jnp.concatenate([sel_t, jnp.zeros((LANES - jj, N_KV * tq), F32)], axis=0)
    sel = sel_t.T

    qgs = [q_group(g) for g in range(N_KV)]
    q_aug = []
    for g in range(N_KV):
        sg = jnp.concatenate([sel[g * tq:(g + 1) * tq]] * QPG, axis=0)
        q_aug.append(jnp.concatenate([qgs[g], ((sg - 1.0) * (-NEG)).astype(BF16)], axis=1))
    tokpos = s0 + lax.broadcasted_iota(jnp.int32, (1, rows), 1) % tq

    def chunks(cs, causal):
        scores_t, vals = [], []
        for c, cz in zip(cs, causal):
            k0 = pl.multiple_of(c * ck, ck)
            keypos = k0 + lax.broadcasted_iota(jnp.int32, (ck, 1), 0)
            onehot = oh_ref[pl.ds(k0, ck), :]
            for g in range(N_KV):
                kk = sel_ref[pl.ds(k0, ck), g * NSA_HD:(g + 1) * NSA_HD]
                s = _mm(jnp.concatenate([kk, onehot], axis=1), q_aug[g], NT)
                scores_t.append(jnp.where(keypos <= tokpos, s, NEG) if cz else s)
                vals.append(sel_ref[pl.ds(k0, ck), (N_KV + g) * NSA_HD:(N_KV + g + 1) * NSA_HD])
        maxes = [jnp.max(s, axis=0, keepdims=True) for s in scores_t]
        probs = [jnp.exp(s - m) for s, m in zip(scores_t, maxes)]
        sums = [jnp.sum(p, axis=0, keepdims=True) for p in probs]
        accs = [_mm(v, p.astype(BF16), TN) for v, p in zip(vals, probs)]
        parts = list(zip(maxes, sums, accs))
        return [tuple(parts[i * N_KV:(i + 1) * N_KV]) for i in range(len(cs))]

    def merge(carry, *chunks):
        new = []
        for g in range(N_KV):
            m_i, l_i, acc = carry[g]
            m_new = m_i
            for ch in chunks:
                m_new = jnp.maximum(m_new, ch[g][0])
            w_i = jnp.exp(m_i - m_new)
            l_new, acc_new = w_i * l_i, w_i * acc
            for ch in chunks:
                w_c = jnp.exp(ch[g][0] - m_new)
                l_new = l_new + w_c * ch[g][1]
                acc_new = acc_new + w_c * ch[g][2]
            new.append((m_new, l_new, acc_new))
        return tuple(new)

    neutral = tuple((jnp.full((1, rows), NEG, F32), jnp.zeros((1, rows), F32),
                     jnp.zeros((NSA_HD, rows), F32)) for _ in range(N_KV))
    c_diag = s0 // ck
    n_pairs = c_diag // 2
    mid = lax.fori_loop(
        0, n_pairs, lambda i, carry: merge(carry, *chunks([2 * i, 2 * i + 1], [False, False])), neutral)
    fin = lax.cond(c_diag % 2 == 1,
                   lambda: merge(mid, *chunks([c_diag - 1, c_diag], [False, True])),
                   lambda: merge(mid, *chunks([c_diag], [True])))

    wlen = WINDOW + 2 * tq
    w0 = pl.multiple_of(jnp.clip(s0 - WINDOW, 0, t_len - wlen), tq)
    wpos = w0 + lax.broadcasted_iota(jnp.int32, (wlen, 1), 0)
    dist = tokpos - wpos
    bias_w = jnp.where((dist >= 0) & (dist < WINDOW), 0.0, NEG)
    s_ws = [_mm(win_ref[pl.ds(w0, wlen), g * NSA_HD:(g + 1) * NSA_HD], qgs[g], NT) + bias_w
            for g in range(N_KV)]
    e_ws = [jnp.exp(s - jnp.max(s, axis=0, keepdims=True)) for s in s_ws]
    o_ws = [(_mm(win_ref[pl.ds(w0, wlen), (N_KV + g) * NSA_HD:(N_KV + g + 1) * NSA_HD],
                 e_ws[g].astype(BF16), TN) / jnp.sum(e_ws[g], axis=0, keepdims=True)).T
            for g in range(N_KV)]
    gt = gt_ref[...]
    outs = []
    for g in range(N_KV):
        o_w = o_ws[g]
        _, l_i, acc = fin[g]
        o_s = (acc / l_i).T
        for h in range(QPG):
            c0 = (g * QPG + h) * 3
            r0 = slice(h * tq, (h + 1) * tq)
            outs.append(gt[:, c0:c0 + 1] * o_cmp[g][r0] + gt[:, c0 + 1:c0 + 2] * o_s[r0]
                        + gt[:, c0 + 2:c0 + 3] * o_w[r0])
    o_ref[...] = jnp.concatenate(outs, axis=1).astype(o_ref.dtype)


def _nsa_prompt(qn, gates, selbf, winbf, kc, vc, nb, t_len):
    n_sub = kc.shape[2]
    tq = SEL_BLOCK
    nq = t_len // tq
    ck = 512
    n_blocks = -(-(t_len // SEL_BLOCK) // 8) * 8
    assert n_blocks <= LANES
    block_onehot = (jnp.arange(t_len)[:, None] // SEL_BLOCK == jnp.arange(LANES)[None, :]).astype(BF16)
    per_b = lambda w: pl.BlockSpec((None, t_len, w), lambda b, i: (b, 0, 0))
    cmp_spec = pl.BlockSpec((None, N_KV, n_sub, NSA_HD), lambda b, i: (b, 0, 0, 0))
    return pl.pallas_call(
        functools.partial(_nsa_prompt_kernel, t_len=t_len, n_sub=n_sub, ck=ck),
        grid=(nb, nq),
        in_specs=[pl.BlockSpec((tq, 1024), lambda b, i: (b * nq + i, 0)),
                  pl.BlockSpec((tq, LANES), lambda b, i: (b * nq + i, 0)),
                  per_b(512), per_b(512), cmp_spec, cmp_spec,
                  pl.BlockSpec((t_len, LANES), lambda b, i: (0, 0))],
        out_specs=pl.BlockSpec((tq, 1024), lambda b, i: (b * nq + i, 0)),
        out_shape=jax.ShapeDtypeStruct((nb * t_len, 1024), BF16),
        scratch_shapes=[pltpu.VMEM((n_blocks, N_KV * tq), F32)],
        compiler_params=_cparams(("arbitrary", "arbitrary")),
    )(qn, gates, selbf.reshape(nb, t_len, 512), winbf.reshape(nb, t_len, 512), kc, vc, block_onehot)


def _nsa_sample_select_kernel(q_ref, kc_ref, vc_ref, oc_ref, sel_ref, sc_scr, st_scr,
                              *, n_sub, n_sel, ts, past, jw):
    b = pl.program_id(0)
    nb = pl.num_programs(0)
    n_cmp = n_sub - 1
    q = q_ref[...]
    imp_m = _imp_matrix(n_sub, n_cmp, jw)
    rows = QPG * ts
    rowpos = past + lax.broadcasted_iota(jnp.int32, (rows, 1), 0) % ts
    n_idx = lax.broadcasted_iota(jnp.int32, (rows, n_sub), 1)
    ok_c = (n_idx * CMP_STRIDE + (CMP_LEN - 1) <= rowpos) & (n_idx < n_cmp)
    jl = lax.broadcasted_iota(jnp.int32, (ts, jw), 1)
    blk = (past + lax.broadcasted_iota(jnp.int32, (ts, 1), 0)) // SEL_BLOCK
    forced = (jl == 0) | ((jl <= blk) & (jl > blk - N_LOCAL))

    @pl.when(b == 0)
    def _():
        sc_scr[...] = jnp.full(sc_scr.shape, INVALID, F32)

    for g in range(N_KV):
        qg = jnp.concatenate(
            [q[:, (g * QPG + h) * NSA_HD:(g * QPG + h + 1) * NSA_HD] for h in range(QPG)],
            axis=0).astype(BF16)
        s_c = _mm(qg, kc_ref[g].astype(BF16), NT)
        p_c = _softmax_rows(s_c, ok_c)
        oc_ref[g * rows:(g + 1) * rows, :] = _mm(p_c.astype(BF16), vc_ref[g].astype(BF16))
        p_g = p_c[0:ts] + p_c[ts:2 * ts] + p_c[2 * ts:3 * ts] + p_c[3 * ts:4 * ts]
        imp = _mm_lhs_exact(p_g, imp_m)
        score = jnp.where((jl <= blk) & (jl < n_sel), jnp.where(forced, FORCED, imp), INVALID)
        sc_scr[pl.ds(pl.multiple_of((b * N_KV + g) * ts, ts), ts), :] = score

    @pl.when(b == nb - 1)
    def _():
        st_scr[...] = sc_scr[...].T
        rank = _topk_mask_t(st_scr, n_sel, LANES)
        sel_t = jnp.where((rank < SEL_TOPK) & (st_scr[...] > 0.5 * INVALID), 1.0, 0.0)
        sel_ref[...] = sel_t.T


def _nsa_sample_select(qn_s, kc, vc, ts, past, n_sel):
    nb = kc.shape[0]
    n_sub = kc.shape[2]
    jw = -(-n_sel // LANES) * LANES
    assert nb * N_KV * ts <= LANES
    cmp_spec = pl.BlockSpec((None, N_KV, n_sub, NSA_HD), lambda b: (b, 0, 0, 0))
    return pl.pallas_call(
        functools.partial(_nsa_sample_select_kernel, n_sub=n_sub, n_sel=n_sel, ts=ts, past=past, jw=jw),
        grid=(nb,),
        in_specs=[pl.BlockSpec((None, ts, 1024), lambda b: (b, 0, 0)), cmp_spec, cmp_spec],
        out_specs=[pl.BlockSpec((None, N_QH * ts, NSA_HD), lambda b: (b, 0, 0)),
                   pl.BlockSpec((LANES, jw), lambda b: (0, 0))],
        out_shape=[jax.ShapeDtypeStruct((nb, N_QH * ts, NSA_HD), F32),
                   jax.ShapeDtypeStruct((LANES, jw), F32)],
        scratch_shapes=[pltpu.VMEM((LANES, jw), F32), pltpu.VMEM((jw, LANES), F32)],
        compiler_params=_cparams(("arbitrary",)),
    )(qn_s.reshape(nb, ts, 1024), kc, vc)


def _nsa_sample_sweep_kernel(past_ref, q_ref, gt_ref, sel_ref, oc_ref, rows_ref, wnew_ref, wst_ref,
                             o_ref, m_scr, l_scr, acc_scr, *, nk, ts, past, n_sel, wbuf):
    step = pl.program_id(1)
    rows = QPG * ts
    heads = 4 * N_KV
    jw = sel_ref.shape[1]
    q = q_ref[...]
    qgs = [jnp.concatenate(
        [q[:, (g * QPG + h) * NSA_HD:(g * QPG + h + 1) * NSA_HD] for h in range(QPG)],
        axis=0).astype(BF16) for g in range(N_KV)]
    sel = sel_ref[...].astype(BF16)

    @pl.when(step == 0)
    def _():
        m_scr[...] = jnp.full(m_scr.shape, NEG, F32)
        l_scr[...] = jnp.zeros_like(l_scr)
        acc_scr[...] = jnp.zeros_like(acc_scr)

    def update(g, s, ok, vv):
        r0 = slice(g * rows, (g + 1) * rows)
        s = jnp.where(ok, s, NEG)
        m_i = m_scr[r0]
        m_new = jnp.maximum(m_i, jnp.max(s, axis=-1, keepdims=True))
        pr = jnp.where(ok, jnp.exp(s - m_new), 0.0)
        alpha = jnp.exp(m_i - m_new)
        l_scr[r0] = alpha * l_scr[r0] + jnp.sum(pr, axis=-1, keepdims=True)
        acc_scr[r0] = alpha * acc_scr[r0] + _mm(pr.astype(BF16), vv)
        m_scr[r0] = m_new

    ej = lax.broadcasted_iota(jnp.int32, (jw, nk), 0)
    kl = lax.broadcasted_iota(jnp.int32, (1, nk), 1)
    expand = jnp.where(ej == (step * nk + kl) // SEL_BLOCK, 1.0, 0.0).astype(BF16)
    mask2 = _mm(sel, expand)
    for g in range(N_KV):
        kk = past_ref[:, g * NSA_HD:(g + 1) * NSA_HD]
        vv = past_ref[:, (N_KV + g) * NSA_HD:(N_KV + g + 1) * NSA_HD]
        mg = mask2[g * ts:(g + 1) * ts]
        ok = jnp.concatenate([mg] * QPG, axis=0) > 0.5
        update(g, _mm(qgs[g], kk, NT), ok, vv)

    @pl.when(step == pl.num_programs(1) - 1)
    def _():
        tpos = lax.broadcasted_iota(jnp.int32, (rows, 1), 0) % ts
        pad = jnp.zeros((PAGE - ts, NSA_HD), F32)
        il = lax.broadcasted_iota(jnp.int32, (1, PAGE), 1)
        last_sel = sel_ref[:, n_sel - 1:n_sel]
        gt = gt_ref[...]
        wlen = wbuf + PAGE
        wl = lax.broadcasted_iota(jnp.int32, (1, wlen), 1)
        dist = (past + tpos) - (past - wbuf + wl)
        ok_w = (dist >= 0) & (dist < WINDOW) & (wl < wbuf + ts)
        outs = []
        for g in range(N_KV):
            kn = jnp.concatenate([rows_ref[pl.ds(2 * N_KV + g, ts, stride=heads), :], pad], axis=0)
            vn = jnp.concatenate([rows_ref[pl.ds(3 * N_KV + g, ts, stride=heads), :], pad], axis=0)
            lsel = jnp.concatenate([last_sel[g * ts:(g + 1) * ts]] * QPG, axis=0) > 0.5
            ok = (il <= tpos) & (il < ts) & lsel
            update(g, _mm(qgs[g], kn.astype(BF16), NT), ok, vn.astype(BF16))
            r0 = slice(g * rows, (g + 1) * rows)
            o_s = acc_scr[r0] / l_scr[r0]
            kw = jnp.concatenate([wst_ref[pl.ds(g, wbuf, stride=2 * N_KV), :],
                                  wnew_ref[pl.ds(g, ts, stride=2 * N_KV), :], pad], axis=0).astype(BF16)
            vw = jnp.concatenate([wst_ref[pl.ds(N_KV + g, wbuf, stride=2 * N_KV), :],
                                  wnew_ref[pl.ds(N_KV + g, ts, stride=2 * N_KV), :], pad],
                                 axis=0).astype(BF16)
            p_w = _softmax_rows(_mm(qgs[g], kw, NT), ok_w)
            o_w = _mm(p_w.astype(BF16), vw)
            o_c = oc_ref[r0, :]
            for h in range(QPG):
                c0 = (g * QPG + h) * 3
                rh = slice(h * ts, (h + 1) * ts)
                outs.append(gt[:, c0:c0 + 1] * o_c[rh] + gt[:, c0 + 1:c0 + 2] * o_s[rh]
                            + gt[:, c0 + 2:c0 + 3] * o_w[rh])
        o_ref[...] = jnp.concatenate(outs, axis=1)


def _nsa_sample_sweep(past_sel, qn_s, gates_s, sel, o_cmp, rows_s, win_s, win_state, ts, n_sel):
    nb, past, _ = past_sel.shape
    nk = min(past, 2048)
    wbuf = win_state.shape[1] // (2 * N_KV)
    jw = sel.shape[1]
    per_b = lambda r, w: pl.BlockSpec((None, r, w), lambda b, s: (b, 0, 0))
    return pl.pallas_call(
        functools.partial(_nsa_sample_sweep_kernel, nk=nk, ts=ts, past=past, n_sel=n_sel, wbuf=wbuf),
        grid=(nb, past // nk),
        in_specs=[pl.BlockSpec((None, nk, 512), lambda b, s: (b, s, 0)),
                  per_b(ts, 1024), per_b(ts, LANES),
                  pl.BlockSpec((N_KV * ts, jw), lambda b, s: (b, 0)),
                  per_b(N_QH * ts, NSA_HD), per_b(ts * 4 * N_KV, NSA_HD), per_b(ts * 2 * N_KV, NSA_HD),
                  per_b(wbuf * 2 * N_KV, NSA_HD)],
        out_specs=per_b(ts, 1024),
        out_shape=jax.ShapeDtypeStruct((nb, ts, 1024), F32),
        scratch_shapes=[pltpu.VMEM((N_QH * ts, 1), F32), pltpu.VMEM((N_QH * ts, 1), F32),
                        pltpu.VMEM((N_QH * ts, NSA_HD), F32)],
        compiler_params=_cparams(("arbitrary", "arbitrary")),
    )(past_sel, qn_s.reshape(nb, ts, 1024), gates_s.reshape(nb, ts, LANES), sel,
      o_cmp, rows_s.reshape(nb, ts * 4 * N_KV, NSA_HD), win_s.reshape(nb, ts * 2 * N_KV, NSA_HD),
      win_state)


def _softplus(z):
    return jnp.maximum(z, 0.0) + jnp.log1p(jnp.exp(-jnp.abs(z)))


def _rwkv_kernel(rkv_ref, aux_ref, sh_rkv_ref, sh_aux_ref, s0_ref, mu_rkv_ref, mu_aux_ref,
                 vec_ref, ww_ref, wa_ref, wg_ref, o_ref, sfin_ref, s_scr, c_rkv, c_aux, *, cs, nsub, n_valid):
    ci = pl.program_id(1)
    n_pairs = s_scr.shape[0]
    dr = n_pairs * LANES

    @pl.when(ci == 0)
    def _():
        s_scr[...] = s0_ref[...]
        c_rkv[...] = sh_rkv_ref[...]
        c_aux[...] = sh_aux_ref[...]

    def shift_mix(x, carry, mu):
        first = lax.broadcasted_iota(jnp.int32, x.shape, 0) == 0
        prev = jnp.where(first, carry, pltpu.roll(x, 1, 0))
        return x + (prev - x) * mu

    rkv = rkv_ref[...]
    aux = aux_ref[...]
    xm = shift_mix(rkv, c_rkv[...], mu_rkv_ref[...])
    xa = shift_mix(aux, c_aux[...], mu_aux_ref[...])
    n_rows = nsub * cs
    c_rkv[...] = rkv[n_rows - 1:n_rows, :]
    c_aux[...] = aux[n_rows - 1:n_rows, :]

    w0, a0, k_k, k_a = vec_ref[0:1, :], vec_ref[1:2, :], vec_ref[2:3, :], vec_ref[3:4, :]
    r_k, ln_w, ln_b = vec_ref[4:5, :], vec_ref[5:6, :], vec_ref[6:7, :]
    r = xm[:, :dr]
    k = xm[:, dr:2 * dr]
    v = xm[:, 2 * dr:]
    u = w0 + _mm(jnp.tanh(xa).astype(BF16), ww_ref[...])
    lw = -jnp.exp(-_softplus(-u) - 0.5)
    a = jax.nn.sigmoid(a0 + _mm(xa.astype(BF16), wa_ref[...]))
    gate = _mm(jax.nn.sigmoid(xa).astype(BF16), wg_ref[...])

    lane = lax.broadcasted_iota(jnp.int32, (1, LANES), 1)
    head0 = lane < RWKV_HD
    rr = lax.broadcasted_iota(jnp.int32, (LANES, LANES), 0)
    cc = lax.broadcasted_iota(jnp.int32, (LANES, LANES), 1)
    seg = jnp.where((rr // RWKV_HD) == (cc // RWKV_HD), 1.0, 0.0).astype(BF16)
    eye = jnp.where(rr == cc, 1.0, 0.0)

    def seg_sum(x):
        return jnp.concatenate(
            [_mm_lhs_exact(x[:, p * LANES:(p + 1) * LANES], seg) for p in range(n_pairs)], axis=1)

    kk = k * k_k
    kkn = kk * lax.rsqrt(seg_sum(kk * kk) + 1e-12)
    kh = k * (1.0 + (a - 1.0) * k_a)
    bb = kkn * a
    bonus = seg_sum(r * kh * r_k) * v
    if n_valid < n_rows:
        live = lax.broadcasted_iota(jnp.int32, (n_rows, 1), 0) < n_valid
        lw = jnp.where(live, lw, 0.0)
        kh = jnp.where(live, kh, 0.0)
        kkn = jnp.where(live, kkn, 0.0)
        bb = jnp.where(live, bb, 0.0)
        v = jnp.where(live, v, 0.0)

    t_r = lax.broadcasted_iota(jnp.int32, (cs, cs), 0)
    t_c = lax.broadcasted_iota(jnp.int32, (cs, cs), 1)
    tri = jnp.where(t_r >= t_c, 1.0, 0.0).astype(BF16)

    s2 = 2 * cs
    sr = lax.broadcasted_iota(jnp.int32, (s2, s2), 0) % cs
    sc = lax.broadcasted_iota(jnp.int32, (s2, s2), 1) % cs
    strict = sr > sc
    incl = sr >= sc
    eye2 = jnp.where(lax.broadcasted_iota(jnp.int32, (s2, s2), 0)
                     == lax.broadcasted_iota(jnp.int32, (s2, s2), 1), 1.0, 0.0)

    def stack(x):
        return jnp.concatenate([jnp.where(head0, x, 0.0), jnp.where(head0, 0.0, x)], axis=0)

    prs = range(n_pairs)
    ents = range(nsub * n_pairs)
    a_s, q_s, k_s, b_s, v_s, kh_s, bh_s, w_tot = [], [], [], [], [], [], [], []
    for h in range(nsub):
        rs = slice(h * cs, (h + 1) * cs)
        cum = _mm_rhs_exact(tri, lw[rs])
        tot = cum[cs - 1:cs, :]
        e_inv = jnp.exp(-cum)
        e_rem = jnp.exp(tot - cum)
        per_pair = lambda x: [stack(x[:, p * LANES:(p + 1) * LANES]) for p in prs]
        a_s += per_pair(-kkn[rs] * jnp.exp(cum - lw[rs]))
        q_s += per_pair(r[rs] * jnp.exp(cum))
        k_s += per_pair(kh[rs] * e_inv)
        b_s += per_pair(bb[rs] * e_inv)
        v_s += per_pair(v[rs])
        kh_s += per_pair(kh[rs] * e_rem)
        bh_s += per_pair(bb[rs] * e_rem)
        w_tot.append(jnp.exp(tot))

    n_dbl = int(math.log2(cs)) - 1
    zeros_s = jnp.zeros((s2, LANES), F32)
    gram = [_mmp(jnp.concatenate([a_s[e], q_s[e]], axis=0), jnp.concatenate([b_s[e], k_s[e]], axis=0),
                 NT, RW_PASSES["gram"]) for e in ents]
    a_ab = [jnp.where(strict, gram[e][:s2, :s2], 0.0) for e in ents]
    a_ak = [jnp.where(strict, gram[e][:s2, s2:], 0.0) for e in ents]
    b_rbk = [jnp.concatenate([jnp.where(incl, gram[e][s2:, :s2], 0.0),
                              jnp.where(incl, gram[e][s2:, s2:], 0.0)], axis=1) for e in ents]
    akv = [_mmp(a_ak[e], v_s[e], passes=RW_PASSES["mix"]) for e in ents]
    tinv = [eye2 + a_ab[e] for e in ents]
    apow = a_ab
    for _ in range(n_dbl):
        apow = [_mmp(apow[e], apow[e], passes=RW_PASSES["inv"]) for e in ents]
        tinv = [tinv[e] + _mmp(apow[e], tinv[e], passes=RW_PASSES["inv"]) for e in ents]
    au = [_mmp(tinv[e], jnp.concatenate([a_s[e], akv[e]], axis=1), passes=RW_PASSES["mix"])
          for e in ents]
    ry = [_mmp(b_rbk[e], jnp.concatenate([au[e], jnp.concatenate([zeros_s, v_s[e]], axis=1)], axis=0),
               passes=RW_PASSES["mix"]) for e in ents]
    m_p = [eye * w_tot[e // n_pairs][:, (e % n_pairs) * LANES:(e % n_pairs + 1) * LANES]
           + _mmp(au[e][:, :LANES], bh_s[e], TN, RW_PASSES["trans"]) for e in ents]
    n_p = [_mmp(jnp.concatenate([au[e][:, LANES:], v_s[e]], axis=0),
                jnp.concatenate([bh_s[e], kh_s[e]], axis=0), TN, RW_PASSES["trans"]) for e in ents]
    s_cur = [s_scr[p] for p in prs]
    y_rows = []
    for h in range(nsub):
        es = [h * n_pairs + p for p in prs]
        y_s = [_mmp(q_s[e] + ry[e][:, :LANES], s_cur[p], NT, RW_PASSES["out"]) + ry[e][:, LANES:]
               for p, e in zip(prs, es)]
        s_cur = [_mmp(s_cur[p], m_p[e], passes=RW_PASSES["state"]) + n_p[e] for p, e in zip(prs, es)]
        y_rows.append(jnp.concatenate([y_s[p][:cs] + y_s[p][cs:] for p in prs], axis=1))
    for p in prs:
        s_scr[p] = s_cur[p]
    y = y_rows[0] if nsub == 1 else jnp.concatenate(y_rows, axis=0)

    mean = seg_sum(y) * (1.0 / RWKV_HD)
    dy = y - mean
    var = seg_sum(dy * dy) * (1.0 / RWKV_HD)
    yn = dy * lax.rsqrt(var + GN_EPS) * ln_w + ln_b
    o_ref[...] = ((yn + bonus) * gate).astype(o_ref.dtype)

    @pl.when(ci == pl.num_programs(1) - 1)
    def _():
        sfin_ref[...] = s_scr[...]


def _rwkv(p, shift_rkv, shift_aux, s0_pairs, mu_rkv, mu_aux, vecs, ww, wa, wg, n_seq, t_len, cs, nsub,
          n_valid, out_dtype):
    rows = nsub * cs
    n_chunks = t_len // rows
    n_pairs = s0_pairs.shape[1]
    dr = n_pairs * LANES
    full = lambda shape: pl.BlockSpec(shape, lambda b, c: (0,) * len(shape))
    return pl.pallas_call(
        functools.partial(_rwkv_kernel, cs=cs, nsub=nsub, n_valid=n_valid),
        grid=(n_seq, n_chunks),
        in_specs=[pl.BlockSpec((rows, 3 * dr), lambda b, c: (b * n_chunks + c, P_RKV // (3 * dr))),
                  pl.BlockSpec((rows, P_AUX_W), lambda b, c: (b * n_chunks + c, P_AUX // P_AUX_W)),
                  pl.BlockSpec((None, 1, 3 * dr), lambda b, c: (b, 0, 0)),
                  pl.BlockSpec((None, 1, P_AUX_W), lambda b, c: (b, 0, 0)),
                  pl.BlockSpec((None, n_pairs, LANES, LANES), lambda b, c: (b, 0, 0, 0)),
                  full(mu_rkv.shape), full(mu_aux.shape), full(vecs.shape),
                  full(ww.shape), full(wa.shape), full(wg.shape)],
        out_specs=[pl.BlockSpec((rows, dr), lambda b, c: (b * n_chunks + c, 0)),
                   pl.BlockSpec((None, n_pairs, LANES, LANES), lambda b, c: (b, 0, 0, 0))],
        out_shape=[jax.ShapeDtypeStruct((n_seq * t_len, dr), out_dtype),
                   jax.ShapeDtypeStruct((n_seq, n_pairs, LANES, LANES), F32)],
        scratch_shapes=[pltpu.VMEM((n_pairs, LANES, LANES), F32),
                        pltpu.VMEM((1, 3 * dr), F32), pltpu.VMEM((1, P_AUX_W), F32)],
        compiler_params=_cparams(("arbitrary", "arbitrary")),
    )(p, p, shift_rkv, shift_aux, s0_pairs, mu_rkv, mu_aux, vecs, ww, wa, wg)


def _out_kernel(x_ref, m_ref, on_ref, or_ref, wn_ref, wr_ref, o_ref):
    mix = _mm(on_ref[...], wn_ref[...]) + _mm(or_ref[...], wr_ref[...])
    o_ref[...] = x_ref[...] + m_ref[5] * mix


def _out_proj(x, mod, o_nsa, o_rw, w_out, layer, tm, tiles_per_seq):
    m_rows, d = x.shape
    dn = o_nsa.shape[1]
    mr = mod.shape[2]
    return pl.pallas_call(
        _out_kernel,
        grid=(m_rows // tm,),
        in_specs=[pl.BlockSpec((tm, d), lambda i: (i, 0)),
                  pl.BlockSpec((None, N_MOD, mr, d), lambda i: (i // tiles_per_seq, 0, 0, 0)),
                  pl.BlockSpec((tm, dn), lambda i: (i, 0)),
                  pl.BlockSpec((tm, dn), lambda i: (i, 0)),
                  pl.BlockSpec((None, dn, d), lambda i: (layer, 0, 0)),
                  pl.BlockSpec((None, dn, d), lambda i: (layer, 1, 0))],
        out_specs=pl.BlockSpec((tm, d), lambda i: (i, 0)),
        out_shape=jax.ShapeDtypeStruct((m_rows, d), F32),
        compiler_params=_cparams(("arbitrary",)),
    )(x, mod, o_nsa, o_rw, w_out, w_out)


def _pad_cols(x, n):
    return jnp.pad(x, [(0, 0)] * (x.ndim - 1) + [(0, n)])


def _reorder_cols(w, d_rwkv, n_gate):
    nsa_main = P_AUX
    rw0 = nsa_main + n_gate
    lora0 = rw0 + 3 * d_rwkv
    n_lora = w.shape[-1] - lora0
    aux = jnp.concatenate([_pad_cols(w[..., nsa_main:rw0], AUX_WD - n_gate),
                           _pad_cols(w[..., lora0:], P_AUX_W - AUX_WD - n_lora)], axis=-1)
    return jnp.concatenate([w[..., :nsa_main], aux, w[..., rw0:lora0]], axis=-1)


def _pairs_from_heads(s):
    n, h = s.shape[:2]
    s = s.reshape(n, h // 2, 2, RWKV_HD, RWKV_HD)
    z = jnp.zeros_like(s[:, :, 0])
    top = jnp.concatenate([s[:, :, 0], z], axis=-1)
    bot = jnp.concatenate([z, s[:, :, 1]], axis=-1)
    return jnp.concatenate([top, bot], axis=-2)


def _heads_from_pairs(s):
    n, hp = s.shape[:2]
    a = s[:, :, :RWKV_HD, :RWKV_HD]
    b = s[:, :, RWKV_HD:, RWKV_HD:]
    return jnp.stack([a, b], axis=2).reshape(n, 2 * hp, RWKV_HD, RWKV_HD)


def kernel(x_prompt, x_sample, cache_nsa_kv, state_win_kv, state_wkv, state_shift, page_table,
           c_prompt, c_sample, w_ada, b_ada, norm_g, ffn_wi, ffn_wo, w_in, w_out,
           q_norm_g, k_norm_g, cmp_pe, cmp_w1, cmp_b1, cmp_w2,
           rwkv_mu, rwkv_w0, rwkv_w_w2, rwkv_a0, rwkv_w_a2, rwkv_w_g2,
           rwkv_k_k, rwkv_k_a, rwkv_r_k, rwkv_ln_w, rwkv_ln_b):
    bp, tp, d = x_prompt.shape
    bs, ts, _ = x_sample.shape
    depth = w_ada.shape[0]
    n_pool = cache_nsa_kv.shape[1]
    n_pages = page_table.shape[1]
    past = n_pages * PAGE
    wbuf = state_win_kv.shape[2]
    d_rwkv = rwkv_w0.shape[1]
    n_heads = d_rwkv // RWKV_HD
    d_nsa = N_QH * NSA_HD
    n_gate = 3 * N_QH
    n_dlora = rwkv_w_w2.shape[1]
    n_alora = rwkv_w_a2.shape[1]
    n_glora = rwkv_w_g2.shape[1]
    mp_rows, ms_rows = bp * tp, bs * ts
    tm = 512
    tiles_per_seq = tp // tm
    cs = 64
    ts_pad = -(-ts // cs) * cs
    win_keep = min(WINDOW, tp)

    c_all = jnp.concatenate([c_prompt, c_sample], axis=0)
    c_rows = -(-c_all.shape[0] // 8) * 8
    c_all = jnp.pad(c_all, ((0, c_rows - c_all.shape[0]), (0, 0)))
    mod = _modulation(c_all, w_ada, b_ada).reshape(depth, c_rows, N_MOD, d)
    mod_p = mod[:, :bp].reshape(depth, bp, N_MOD, 1, d)
    mod_s = jnp.repeat(mod[:, bp:bp + bs].transpose(0, 2, 1, 3), ts, axis=2).reshape(depth, 1, N_MOD, ms_rows, d)

    xp = x_prompt.reshape(mp_rows, d)
    xs = x_sample.reshape(ms_rows, d)
    pool = cache_nsa_kv.reshape(depth * n_pool * PAGE * 4 * N_KV, NSA_HD)
    prompt_pages = jnp.arange(bp * (tp // PAGE), dtype=jnp.int32).reshape(bp, tp // PAGE)
    n_sel_s = -(-(past + ts) // SEL_BLOCK)

    kv_p, kv_s, win_p, win_s, wkv_p, wkv_s, sh_p, sh_s = [], [], [], [], [], [], [], []
    tm_ffn = 1024
    ffn_tiles_per_seq = tp // tm_ffn
    w_in_all = _reorder_cols(w_in, d_rwkv, n_gate).astype(BF16)
    w_out_all = w_out.astype(BF16)
    for l in range(depth):
        g_rows = [norm_g[l, i].reshape(1, d) for i in range(3)]
        q_g = q_norm_g[l].reshape(1, NSA_HD)
        k_g = k_norm_g[l]
        w1 = cmp_w1[l].reshape(2, 2, CMP_STRIDE * NSA_HD, CMP_HID)
        w1cat = jnp.concatenate([w1[:, 0], w1[:, 1]], axis=-1).astype(BF16)
        pe2 = cmp_pe[l].reshape(2, 2, 1, CMP_STRIDE * NSA_HD)
        b1 = cmp_b1[l].reshape(2, 1, CMP_HID)
        w2 = cmp_w2[l].astype(BF16)
        def shift_parts(sh):
            aux = jnp.pad(sh[:, 3 * d_rwkv:], ((0, 0), (AUX_WD, P_AUX_W - AUX_WD - (sh.shape[1] - 3 * d_rwkv))))
            return sh[:, None, :3 * d_rwkv], aux[:, None, :]

        mu_rkv, mu_aux = (m[:, 0] for m in shift_parts(rwkv_mu[l].reshape(1, -1)))
        vecs = jnp.stack([rwkv_w0[l], rwkv_a0[l], rwkv_k_k[l], rwkv_k_a[l], rwkv_r_k[l].reshape(-1),
                          rwkv_ln_w[l], rwkv_ln_b[l], jnp.zeros_like(rwkv_w0[l])])
        lora_rows = lambda w, off: jnp.pad(w, ((off, P_AUX_W - off - w.shape[0]), (0, 0))).astype(BF16)
        ww = lora_rows(rwkv_w_w2[l], AUX_WD)
        wa = lora_rows(rwkv_w_a2[l], AUX_WD + n_dlora)
        wg = lora_rows(rwkv_w_g2[l], AUX_WD + n_dlora + n_alora)

        def shift_out(p_last):
            return jnp.concatenate([p_last[:, P_RKV:P_RKV + 3 * d_rwkv],
                                    p_last[:, P_AUX + AUX_WD:P_AUX + AUX_WD + n_dlora + n_alora + n_glora]],
                                   axis=1)

        xp = _ffn(xp, mod_p[l], g_rows[0], ffn_wi, ffn_wo, l, 0, 0, tm_ffn, ffn_tiles_per_seq)
        xs = _ffn(xs, mod_s[l], g_rows[0], ffn_wi, ffn_wo, l, 0, 0, ms_rows, 1)

        pp = _proj(xp, mod_p[l], g_rows[1], w_in_all, l, tm, tiles_per_seq)
        qn, rows, win, selbf, winbf, gates = _nsa_prep(pp, q_g, k_g, tm, BF16)
        kc, vc = _compress(rows, prompt_pages, w1cat, pe2, b1, w2, k_g, False)
        o_nsa = _nsa_prompt(qn, gates, selbf, winbf, kc, vc, bp, tp)
        zero_rkv, zero_aux = shift_parts(jnp.zeros((bp, state_shift.shape[2]), F32))
        o_rw, s_fin = _rwkv(pp, zero_rkv, zero_aux, jnp.zeros((bp, n_heads // 2, LANES, LANES), F32),
                            mu_rkv, mu_aux, vecs, ww, wa, wg, bp, tp, cs, 2, 2 * cs, BF16)
        xp = _out_proj(xp, mod_p[l], o_nsa, o_rw, w_out_all, l, tm, tiles_per_seq)
        kv_p.append(rows.reshape(bp, tp // PAGE, PAGE, 4, N_KV, NSA_HD))
        win_p.append(win.reshape(bp, tp, 2, N_KV, NSA_HD)[:, tp - win_keep:])
        wkv_p.append(_heads_from_pairs(s_fin))
        sh_p.append(shift_out(pp.reshape(bp, tp, P_COLS)[:, -1]))

        ps = _proj(xs, mod_s[l], g_rows[1], w_in_all, l, ms_rows, 1)
        qn_s, rows_s, win_new, _, _, gates_s = _nsa_prep(ps, q_g, k_g, ms_rows, F32)
        page_idx = page_table + l * n_pool
        kc_s, vc_s, past_sel = _compress(pool, page_idx, w1cat, pe2, b1, w2, k_g, True)
        o_cmp, sel = _nsa_sample_select(qn_s, kc_s, vc_s, ts, past, n_sel_s)
        o_nsa_s = _nsa_sample_sweep(past_sel, qn_s, gates_s, sel, o_cmp, rows_s, win_new,
                                    state_win_kv[l].reshape(bs, wbuf * 2 * N_KV, NSA_HD), ts, n_sel_s)
        sh_rkv, sh_aux = shift_parts(state_shift[l])
        ps_pad = jnp.pad(ps.reshape(bs, ts, P_COLS), ((0, 0), (0, ts_pad - ts), (0, 0)))
        o_rw_s, s_fin_s = _rwkv(ps_pad.reshape(bs * ts_pad, P_COLS), sh_rkv, sh_aux,
                                _pairs_from_heads(state_wkv[l]), mu_rkv, mu_aux, vecs, ww, wa, wg,
                                bs, ts_pad, ts_pad, 1, ts, F32)
        o_rw_s = o_rw_s.reshape(bs, ts_pad, d_rwkv)[:, :ts].reshape(ms_rows, d_rwkv)
        xs = _out_proj(xs, mod_s[l], o_nsa_s.reshape(ms_rows, d_nsa).astype(BF16), o_rw_s.astype(BF16),
                       w_out_all, l, ms_rows, 1)
        kv_s.append(rows_s.reshape(bs, ts, 4, N_KV, NSA_HD))
        win_s.append(jnp.concatenate([state_win_kv[l][:, ts:],
                                      win_new.reshape(bs, ts, 2, N_KV, NSA_HD)], axis=1))
        wkv_s.append(_heads_from_pairs(s_fin_s))
        sh_s.append(shift_out(ps.reshape(bs, ts, P_COLS)[:, -1]))

        xp = _ffn(xp, mod_p[l], g_rows[2], ffn_wi, ffn_wo, l, 1, 2, tm_ffn, ffn_tiles_per_seq)
        xs = _ffn(xs, mod_s[l], g_rows[2], ffn_wi, ffn_wo, l, 1, 2, ms_rows, 1)

    return (xp.reshape(bp, tp, d), xs.reshape(bs, ts, d),
            jnp.stack(kv_p), jnp.stack(kv_s), jnp.stack(win_p), jnp.stack(win_s),
            jnp.stack(wkv_p), jnp.stack(wkv_s), jnp.stack(sh_p), jnp.stack(sh_s))
```
